```python
import math
import jax, jax.numpy as jnp
from jax import lax
import numpy as np

D_MODEL = 1024
BATCH = 8
SEQ = 8192
DEPTH = 1
DEC_BATCH = 16
DEC_SEQ = 16
PAST_LEN = 2048

CHUNK = 64
SSD_WIDTH = D_MODEL // 2
SSD_HEAD_DIM = 64
SSD_HEADS = SSD_WIDTH // SSD_HEAD_DIM
SSD_GROUPS = 2
SSD_HEADS_PER_GROUP = SSD_HEADS // SSD_GROUPS
SSD_STATE = 128
CONV_WIDTH = 4
CONV_DIM = SSD_WIDTH + 2 * SSD_GROUPS * SSD_STATE
GMLP_WIDTH = D_MODEL - SSD_WIDTH
GMLP_GROUPS = 8
GMLP_GROUP_DIM = GMLP_WIDTH // GMLP_GROUPS
GMLP_CHUNK = 128
IN_PROJ = SSD_WIDTH + CONV_DIM + SSD_HEADS + 2 * GMLP_WIDTH
N_EXPERT_GROUPS = 4
EXPERTS_PER_GROUP = 8
N_EXPERTS = N_EXPERT_GROUPS * EXPERTS_PER_GROUP
TOP_K_IN_GROUP = 2
D_EXPERT = 256
MOE_BLOCK = 128
EPS = 1e-6

kernel_name = "hybrid_ssd_gmlp_hmoe_stream_step"


def rms_norm(x, g):
    xf = x.astype(jnp.float32)
    y = xf * lax.rsqrt(jnp.mean(xf * xf, axis=-1, keepdims=True) + EPS)
    return (y * g.astype(jnp.float32)).astype(x.dtype)


def layer_norm(x, g, b):
    xf = x.astype(jnp.float32)
    mu = jnp.mean(xf, axis=-1, keepdims=True)
    var = jnp.mean(jnp.square(xf - mu), axis=-1, keepdims=True)
    y = (xf - mu) * lax.rsqrt(var + EPS) * g.astype(jnp.float32) + b.astype(jnp.float32)
    return y.astype(x.dtype)


def causal_conv(xbc, prev, w, b):
    l = xbc.shape[1]
    xpad = jnp.concatenate([prev.astype(xbc.dtype), xbc], axis=1)
    out = b.astype(xbc.dtype) + sum(xpad[:, k:k + l] * w[k].astype(xbc.dtype) for k in range(CONV_WIDTH))
    return jax.nn.silu(out), xpad[:, -(CONV_WIDTH - 1):]


def ssd_scan(x, dt, a, bmat, cmat, h0, chunk):
    bsz, l = x.shape[:2]
    nc = l // chunk
    G, R, P, N = SSD_GROUPS, SSD_HEADS_PER_GROUP, SSD_HEAD_DIM, SSD_STATE
    f32 = jnp.float32
    xr = x.reshape(bsz, nc, chunk, G, R, P).astype(f32)
    dtr = dt.reshape(bsz, nc, chunk, G, R).astype(f32)
    bm = bmat.reshape(bsz, nc, chunk, G, N).astype(f32)
    cm = cmat.reshape(bsz, nc, chunk, G, N).astype(f32)
    a_dt = jnp.moveaxis(dtr * a.reshape(G, R).astype(f32), 2, -1)
    a_cs = jnp.cumsum(a_dt, axis=-1)
    xdt = xr * dtr[..., None]
    mask = jnp.tril(jnp.ones((chunk, chunk), dtype=bool))
    seg = a_cs[..., :, None] - a_cs[..., None, :]
    decay = jnp.where(mask, jnp.exp(jnp.where(mask, seg, 0.0)), 0.0)
    cb = jnp.einsum("bclgn,bcsgn->bcgls", cm, bm)
    y_diag = jnp.einsum("bcgls,bcgrls,bcsgrp->bclgrp", cb, decay, xdt)
    decay_to_end = jnp.exp(a_cs[..., -1:] - a_cs)
    chunk_states = jnp.einsum("bcsgn,bcgrs,bcsgrp->bcgrpn", bm, decay_to_end, xdt)
    chunk_decay = jnp.exp(a_cs[..., -1])

    def step(h, inp):
        st, dec = inp
        return h * dec[..., None, None] + st, h

    h_final, h_prev = lax.scan(step, h0.astype(f32),
                               (jnp.moveaxis(chunk_states, 1, 0), jnp.moveaxis(chunk_decay, 1, 0)))
    h_prev = jnp.moveaxis(h_prev, 0, 1)
    y_off = jnp.einsum("bclgn,bcgrpn,bcgrl->bclgrp", cm, h_prev, jnp.exp(a_cs))
    y = (y_diag + y_off).reshape(bsz, l, SSD_HEADS, P)
    return y, h_final


def hier_moe(h, w_rg, b_rg, w_re, b_re, w_gate, w_up, w_down):
    T = h.shape[0]
    f32 = jnp.float32
    hf = h.astype(f32)
    g_logits = hf @ w_rg.astype(f32) + b_rg.astype(f32)
    g_prob = jax.nn.softmax(g_logits, axis=-1)
    g_sel = jnp.argmax(g_logits, axis=-1)
    p_group = jnp.take_along_axis(g_prob, g_sel[:, None], axis=1)
    e_logits = (hf @ w_re.astype(f32) + b_re.astype(f32)).reshape(T, N_EXPERT_GROUPS, EXPERTS_PER_GROUP)
    e_sel_logits = jnp.take_along_axis(e_logits, g_sel[:, None, None], axis=1)[:, 0]
    top_val, top_idx = lax.top_k(e_sel_logits, TOP_K_IN_GROUP)
    weights = p_group * jax.nn.softmax(top_val, axis=-1)
    expert_idx = (g_sel[:, None] * EXPERTS_PER_GROUP + top_idx).astype(jnp.int32)

    S = T * TOP_K_IN_GROUP
    flat_e = expert_idx.reshape(S)
    flat_w = weights.reshape(S)
    order = jnp.argsort(flat_e)
    sorted_e = flat_e[order]
    counts = jnp.bincount(flat_e, length=N_EXPERTS)
    padded = (counts + MOE_BLOCK - 1) // MOE_BLOCK * MOE_BLOCK
    start = jnp.cumsum(counts) - counts
    pend = jnp.cumsum(padded)
    pstart = pend - padded
    dest = pstart[sorted_e] + jnp.arange(S, dtype=jnp.int32) - start[sorted_e]
    n_blocks = (S + MOE_BLOCK - 1) // MOE_BLOCK + N_EXPERTS
    n_slots = n_blocks * MOE_BLOCK
    slot_tok = jnp.full((n_slots,), T, jnp.int32).at[dest].set((order // TOP_K_IN_GROUP).astype(jnp.int32))
    slot_w = jnp.zeros((n_slots,), f32).at[dest].set(flat_w[order])
    block_e = jnp.minimum(jnp.searchsorted(pend, jnp.arange(n_blocks) * MOE_BLOCK, side="right"), N_EXPERTS - 1)
    h_pad = jnp.concatenate([h, jnp.zeros((1, h.shape[1]), h.dtype)], axis=0)

    def run_block(args):
        toks, e = args
        xb = h_pad[toks]
        return (jax.nn.silu(xb @ w_gate[e]) * (xb @ w_up[e])) @ w_down[e]

    ys = lax.map(run_block, (slot_tok.reshape(n_blocks, MOE_BLOCK), block_e))
    ys = ys.reshape(n_slots, -1) * slot_w[:, None].astype(ys.dtype)
    out = jnp.zeros((T + 1, ys.shape[1]), ys.dtype).at[slot_tok].add(ys)[:T]
    return out.astype(h.dtype)


def layer_forward(x, conv_prev, ssm_prev, norm1_g, w_in, conv_w, conv_b, dt_bias, a_log, d_skip,
                  ssm_norm_g, ln_v_g, ln_v_b, w_spatial, b_spatial, w_out, norm2_g,
                  w_rg, b_rg, w_re, b_re, w_gate, w_up, w_down):
    bsz, l = x.shape[:2]
    h = rms_norm(x, norm1_g)
    proj = h @ w_in.astype(h.dtype)
    z, xbc, dt_raw, gu, gv = jnp.split(
        proj, [SSD_WIDTH, SSD_WIDTH + CONV_DIM, SSD_WIDTH + CONV_DIM + SSD_HEADS,
               SSD_WIDTH + CONV_DIM + SSD_HEADS + GMLP_WIDTH], axis=-1)
    xbc_act, conv_new = causal_conv(xbc, conv_prev, conv_w, conv_b)
    xs, bm, cm = jnp.split(xbc_act, [SSD_WIDTH, SSD_WIDTH + SSD_GROUPS * SSD_STATE], axis=-1)
    dt = jax.nn.softplus(dt_raw.astype(jnp.float32) + dt_bias.astype(jnp.float32))
    a = -jnp.exp(a_log.astype(jnp.float32))
    h0 = ssm_prev.reshape(bsz, SSD_GROUPS, SSD_HEADS_PER_GROUP, SSD_HEAD_DIM, SSD_STATE)
    xh = xs.reshape(bsz, l, SSD_HEADS, SSD_HEAD_DIM)
    y, h_final = ssd_scan(xh, dt, a, bm.reshape(bsz, l, SSD_GROUPS, SSD_STATE),
                          cm.reshape(bsz, l, SSD_GROUPS, SSD_STATE), h0, min(CHUNK, l))
    y = y + xh.astype(jnp.float32) * d_skip.astype(jnp.float32)[:, None]
    gated = (y.reshape(bsz, l, SSD_WIDTH) * jax.nn.silu(z.astype(jnp.float32)))
    gated = gated.reshape(bsz, l, SSD_GROUPS, SSD_WIDTH // SSD_GROUPS)
    gated = gated * lax.rsqrt(jnp.mean(gated * gated, axis=-1, keepdims=True) + EPS)
    ssd_out = (gated.reshape(bsz, l, SSD_WIDTH) * ssm_norm_g.astype(jnp.float32)).astype(x.dtype)
    gu = jax.nn.gelu(gu)
    gv = layer_norm(jax.nn.gelu(gv), ln_v_g, ln_v_b)
    span = min(GMLP_CHUNK, l)
    nc = l // span
    w_s = jnp.tril(w_spatial[:, :span, :span]).astype(gv.dtype)
    v = gv.reshape(bsz, nc, span, GMLP_GROUPS, GMLP_GROUP_DIM)
    mixed = jnp.einsum("hts,bcshd->bcthd", w_s, v) + b_spatial[:, :span].T.astype(gv.dtype)[None, None, :, :, None]
    gmlp_out = (gu * mixed.reshape(bsz, l, GMLP_WIDTH)).astype(x.dtype)
    x = x + jnp.concatenate([ssd_out, gmlp_out], axis=-1) @ w_out.astype(x.dtype)
    h2 = rms_norm(x, norm2_g).reshape(bsz * l, D_MODEL)
    x = x + hier_moe(h2, w_rg, b_rg, w_re, b_re, w_gate, w_up, w_down).reshape(bsz, l, D_MODEL)
    ssm_new = h_final.reshape(bsz, SSD_HEADS, SSD_HEAD_DIM, SSD_STATE)
    return x, conv_new, ssm_new, gv


def setup_inputs(seed: int = 0) -> dict:
    key = jax.random.key(seed)
    ks = jax.random.split(key, 32)
    f32 = jnp.float32

    def nrm(k, shape, scale):
        return jax.random.normal(k, shape, f32) * scale

    dt0 = jnp.exp(jax.random.uniform(ks[6], (DEPTH, SSD_HEADS), f32, math.log(1e-3), math.log(1e-1)))
    return {
        "x_prompt": nrm(ks[0], (BATCH, SEQ, D_MODEL), 1.0),
        "x_sample": nrm(ks[1], (DEC_BATCH, DEC_SEQ, D_MODEL), 1.0),
        "cache_conv": nrm(ks[2], (DEPTH, DEC_BATCH, CONV_WIDTH - 1, CONV_DIM), 1.0),
        "state_ssm": nrm(ks[3], (DEPTH, DEC_BATCH, SSD_HEADS, SSD_HEAD_DIM, SSD_STATE), 0.5),
        "norm1_g": 1.0 + nrm(ks[4], (DEPTH, D_MODEL), 0.05),
        "w_in": nrm(ks[5], (DEPTH, D_MODEL, IN_PROJ), D_MODEL ** -0.5),
        "conv_w": nrm(ks[7], (DEPTH, CONV_WIDTH, CONV_DIM), CONV_WIDTH ** -0.5),
        "conv_b": nrm(ks[8], (DEPTH, CONV_DIM), 0.02),
        "dt_bias": dt0 + jnp.log(-jnp.expm1(-dt0)),
        "a_log": jnp.log(jax.random.uniform(ks[9], (DEPTH, SSD_HEADS), f32, 1.0, 16.0)),
        "d_skip": 1.0 + nrm(ks[10], (DEPTH, SSD_HEADS), 0.1),
        "ssm_norm_g": 1.0 + nrm(ks[11], (DEPTH, SSD_WIDTH), 0.05),
        "ln_v_g": 1.0 + nrm(ks[12], (DEPTH, GMLP_WIDTH), 0.05),
        "ln_v_b": nrm(ks[13], (DEPTH, GMLP_WIDTH), 0.02),
        "w_spatial": nrm(ks[14], (DEPTH, GMLP_GROUPS, GMLP_CHUNK, GMLP_CHUNK), GMLP_CHUNK ** -0.5),
        "b_spatial": 1.0 + nrm(ks[15], (DEPTH, GMLP_GROUPS, GMLP_CHUNK), 0.1),
        "w_out": nrm(ks[16], (DEPTH, D_MODEL, D_MODEL), D_MODEL ** -0.5),
        "norm2_g": 1.0 + nrm(ks[17], (DEPTH, D_MODEL), 0.05),
        "w_router_group": nrm(ks[18], (DEPTH, D_MODEL, N_EXPERT_GROUPS), D_MODEL ** -0.5),
        "b_router_group": nrm(ks[19], (DEPTH, N_EXPERT_GROUPS), 0.01),
        "w_router_expert": nrm(ks[20], (DEPTH, D_MODEL, N_EXPERTS), D_MODEL ** -0.5),
        "b_router_expert": nrm(ks[21], (DEPTH, N_EXPERTS), 0.01),
        "w_gate": nrm(ks[22], (DEPTH, N_EXPERTS, D_MODEL, D_EXPERT), D_MODEL ** -0.5),
        "w_up": nrm(ks[23], (DEPTH, N_EXPERTS, D_MODEL, D_EXPERT), D_MODEL ** -0.5),
        "w_down": nrm(ks[24], (DEPTH, N_EXPERTS, D_EXPERT, D_MODEL), D_EXPERT ** -0.5),
        "final_norm_g": 1.0 + nrm(ks[25], (D_MODEL,), 0.05),
    }


def reference(x_prompt, x_sample, cache_conv, state_ssm, norm1_g, w_in, conv_w, conv_b, dt_bias, a_log,
              d_skip, ssm_norm_g, ln_v_g, ln_v_b, w_spatial, b_spatial, w_out, norm2_g,
              w_router_group, b_router_group, w_router_expert, b_router_expert,
              w_gate, w_up, w_down, final_norm_g):
    yp, ys = x_prompt, x_sample
    conv_p, ssm_p, conv_s, ssm_s, v_s = [], [], [], [], []
    for i in range(DEPTH):
        lw = (norm1_g[i], w_in[i], conv_w[i], conv_b[i], dt_bias[i], a_log[i], d_skip[i], ssm_norm_g[i],
              ln_v_g[i], ln_v_b[i], w_spatial[i], b_spatial[i], w_out[i], norm2_g[i],
              w_router_group[i], b_router_group[i], w_router_expert[i], b_router_expert[i],
              w_gate[i], w_up[i], w_down[i])
        conv0 = jnp.zeros((yp.shape[0], CONV_WIDTH - 1, CONV_DIM), yp.dtype)
        ssm0 = jnp.zeros((yp.shape[0], SSD_HEADS, SSD_HEAD_DIM, SSD_STATE), jnp.float32)
        yp, cp, hp, _ = layer_forward(yp, conv0, ssm0, *lw)
        ys, cs, hs, vs = layer_forward(ys, cache_conv[i], state_ssm[i], *lw)
        conv_p.append(cp)
        ssm_p.append(hp)
        conv_s.append(cs)
        ssm_s.append(hs)
        v_s.append(vs)
    y_prompt = rms_norm(yp, final_norm_g)
    y_sample = rms_norm(ys, final_norm_g)
    conv_prompt = jnp.stack(conv_p)
    ssm_prompt = jnp.stack(ssm_p)
    conv_sample = jnp.stack(conv_s)
    ssm_sample = jnp.stack(ssm_s)
    gmlp_v_sample = jnp.stack(v_s)
    return (y_prompt, y_sample, conv_prompt, ssm_prompt, conv_sample, ssm_sample, gmlp_v_sample)
```

```python
import functools

import jax
import jax.numpy as jnp
from jax import lax
from jax.experimental import pallas as pl
from jax.experimental.pallas import tpu as pltpu

F32 = jnp.float32
BF16 = jnp.bfloat16
EPS = 1e-6

SSD_HEADS = 8
SSD_HEAD_DIM = 64
SSD_STATE = 128
SSD_GROUPS = 2
SSD_WIDTH = SSD_HEADS * SSD_HEAD_DIM
SSD_CHUNK = 64
CONV_WIDTH = 4
GMLP_GROUPS = 8
GMLP_GROUP_DIM = 64
GMLP_WIDTH = GMLP_GROUPS * GMLP_GROUP_DIM
GMLP_CHUNK = 128
N_EXPERT_GROUPS = 4
EXPERTS_PER_GROUP = 8
N_EXPERTS = N_EXPERT_GROUPS * EXPERTS_PER_GROUP
ROUTER_ROWS = 8 + N_EXPERTS

LANES = 128
SUBLANES = 8
N_PAIRS = SSD_HEADS // 2
VMEM_LIMIT = 56 * 1024 * 1024

C_Z = 0
C_XBC = C_Z + SSD_WIDTH
C_GU = C_XBC + SSD_WIDTH + 2 * SSD_GROUPS * SSD_STATE
C_GV = C_GU + GMLP_WIDTH
C_DT = C_GV + GMLP_WIDTH
C_END = C_DT + SSD_WIDTH


def _silu(v):
    return v * (1.0 / (1.0 + jnp.exp(-v)))


def _gelu_tanh(v):
    c = 0.7978845608028654
    return 0.5 * v * (1.0 + jnp.tanh(c * (v + 0.044715 * (v * v * v))))


def _softplus(v):
    return jnp.maximum(v, 0.0) + jnp.log1p(jnp.exp(-jnp.abs(v)))


def _dot(a, b):
    return jnp.dot(a, b, preferred_element_type=F32)


def _dot_nt(a, b):
    return lax.dot_general(a, b, (((1,), (1,)), ((), ())), preferred_element_type=F32)


def _dot_tn(a, b):
    return lax.dot_general(a, b, (((0,), (0,)), ((), ())), preferred_element_type=F32)


def _mixer_kernel(x_ref, conv0_ref, ssm0_ref, g1_ref, win_ref, wdt_ref, convw_ref, convb_ref,
                  dtb_rep_ref, alog_rep_ref, dtb_col_ref, alog_col_ref, dskip_ref, ssmg_ref,
                  lng_ref, lnb_ref, wsp_ref, bsp_ref, wout_ref, g2_ref, wr_hi_ref, wr_lo_ref,
                  br_ref,
                  x1_ref, logit_ref, convo_ref, ssmo_ref, *rest,
                  tl, l_valid, emit_gv):
    if emit_gv:
        gvo_ref, xpad_ref, state_ref, wspm_ref = rest
    else:
        xpad_ref, state_ref, wspm_ref = rest
    t = pl.program_id(1)
    first = jnp.logical_and(pl.program_id(0) == 0, t == 0)

    @pl.when(first)
    def _():
        r = lax.broadcasted_iota(jnp.int32, (GMLP_CHUNK, GMLP_CHUNK), 0)
        c = lax.broadcasted_iota(jnp.int32, (GMLP_CHUNK, GMLP_CHUNK), 1)
        for gi in range(GMLP_GROUPS):
            wspm_ref[gi] = jnp.where(r >= c, wsp_ref[gi], 0.0).astype(BF16)

    @pl.when(t == 0)
    def _():
        xpad_ref[0:SUBLANES, :] = conv0_ref[0]
        state_ref[...] = ssm0_ref[0]

    x = x_ref[0]
    h = x * lax.rsqrt(jnp.mean(x * x, axis=-1, keepdims=True) + EPS) * g1_ref[...]
    hb = h.astype(BF16)
    proj = _dot(hb, win_ref[...])

    xpad_ref[SUBLANES:SUBLANES + tl, :] = proj[:, C_XBC:C_GU]
    conv = convb_ref[...]
    for k in range(CONV_WIDTH):
        off = SUBLANES - (CONV_WIDTH - 1) + k
        conv = conv + xpad_ref[off:off + tl, :] * convw_ref[k:k + 1, :]
    xbc = _silu(conv)
    carry = xpad_ref[l_valid:l_valid + SUBLANES, :]
    xpad_ref[0:SUBLANES, :] = carry
    convo_ref[0] = carry

    xs = xbc[:, 0:SSD_WIDTH]
    bm = xbc[:, SSD_WIDTH:SSD_WIDTH + SSD_GROUPS * SSD_STATE].astype(BF16)
    cm = xbc[:, SSD_WIDTH + SSD_GROUPS * SSD_STATE:].astype(BF16)

    dt = _softplus(proj[:, C_DT:C_END] + dtb_rep_ref[...])
    row = lax.broadcasted_iota(jnp.int32, (tl, SSD_WIDTH), 0)
    if l_valid < tl:
        dt = jnp.where(row < l_valid, dt, 0.0)
    a_rep = -jnp.exp(alog_rep_ref[...])
    acs = dt * a_rep
    row_in_chunk = jnp.bitwise_and(row, SSD_CHUNK - 1)
    sh = 1
    while sh < SSD_CHUNK:
        acs = acs + jnp.where(row_in_chunk >= sh, pltpu.roll(acs, sh, axis=0), 0.0)
        sh *= 2

    dtt = _softplus(_dot_nt(wdt_ref[...], hb) + dtb_col_ref[...])
    lane_t = lax.broadcasted_iota(jnp.int32, (SSD_HEADS, tl), 1)
    if l_valid < tl:
        dtt = jnp.where(lane_t < l_valid, dtt, 0.0)
    acst = dtt * (-jnp.exp(alog_col_ref[...]))
    lane_in_chunk = jnp.bitwise_and(lane_t, SSD_CHUNK - 1)
    sh = 1
    while sh < SSD_CHUNK:
        acst = acst + jnp.where(lane_in_chunk >= sh, pltpu.roll(acst, sh, axis=1), 0.0)
        sh *= 2

    lane = lax.broadcasted_iota(jnp.int32, (SSD_CHUNK, LANES), 1)
    rowc = lax.broadcasted_iota(jnp.int32, (SSD_CHUNK, LANES), 0)
    lo_half = lane < SSD_HEAD_DIM
    causal = rowc >= jnp.bitwise_and(lane, SSD_CHUNK - 1)
    lane1 = lax.broadcasted_iota(jnp.int32, (1, LANES), 1)
    lo_half1 = lane1 < SSD_CHUNK

    y_chunks = []
    for c in range(tl // SSD_CHUNK):
        r0 = c * SSD_CHUNK
        v = acst[:, (c // 2) * LANES:(c // 2 + 1) * LANES]
        vr = pltpu.roll(v, SSD_CHUNK, axis=1)
        v_lo, v_hi = (v, vr) if c % 2 == 0 else (vr, v)
        cb2 = []
        for g in range(SSD_GROUPS):
            cg = cm[r0:r0 + SSD_CHUNK, g * SSD_STATE:(g + 1) * SSD_STATE]
            bg = bm[r0:r0 + SSD_CHUNK, g * SSD_STATE:(g + 1) * SSD_STATE]
            cb2.append(_dot_nt(cg, jnp.concatenate([bg, bg], axis=0)))
        y_pairs = []
        for j in range(N_PAIRS):
            g = j // (N_PAIRS // SSD_GROUPS)
            cg = cm[r0:r0 + SSD_CHUNK, g * SSD_STATE:(g + 1) * SSD_STATE]
            bg = bm[r0:r0 + SSD_CHUNK, g * SSD_STATE:(g + 1) * SSD_STATE]
            sl = slice(j * LANES, (j + 1) * LANES)
            col_a = acs[r0:r0 + SSD_CHUNK, sl]
            row_a = jnp.where(lo_half1, v_lo[2 * j:2 * j + 1, :], v_hi[2 * j + 1:2 * j + 2, :])
            decay = jnp.where(causal, jnp.exp(col_a - row_a), 0.0)
            m = (cb2[g] * decay).astype(BF16)
            xdt = xs[r0:r0 + SSD_CHUNK, sl] * dt[r0:r0 + SSD_CHUNK, sl]
            zbd = jnp.concatenate([jnp.where(lo_half, xdt, 0.0), jnp.where(lo_half, 0.0, xdt)],
                                  axis=0).astype(BF16)
            y_diag = _dot(m, zbd)
            st = state_ref[j]
            y_off = _dot(cg, st.astype(BF16)) * jnp.exp(col_a)
            a_last = acs[r0 + SSD_CHUNK - 1:r0 + SSD_CHUNK, sl]
            zdte = (xdt * jnp.exp(a_last - col_a)).astype(BF16)
            state_ref[j] = st * jnp.exp(a_last) + _dot_tn(bg, zdte)
            y_pairs.append(y_diag + y_off)
        y_chunks.append(jnp.concatenate(y_pairs, axis=1))
    y = jnp.concatenate(y_chunks, axis=0) if len(y_chunks) > 1 else y_chunks[0]
    ssmo_ref[0] = state_ref[...]

    y = y + xs * dskip_ref[...]
    z = proj[:, C_Z:C_XBC]
    gated = y * _silu(z)
    half = SSD_WIDTH // SSD_GROUPS
    outs = []
    for g in range(SSD_GROUPS):
        gg = gated[:, g * half:(g + 1) * half]
        outs.append(gg * lax.rsqrt(jnp.mean(gg * gg, axis=-1, keepdims=True) + EPS))
    ssd_out = jnp.concatenate(outs, axis=1) * ssmg_ref[...]

    gu = _gelu_tanh(proj[:, C_GU:C_GV])
    gv = _gelu_tanh(proj[:, C_GV:C_DT])
    mu = jnp.mean(gv, axis=-1, keepdims=True)
    gvc = gv - mu
    var = jnp.mean(gvc * gvc, axis=-1, keepdims=True)
    gv = gvc * lax.rsqrt(var + EPS) * lng_ref[...] + lnb_ref[...]
    if emit_gv:
        gvo_ref[0] = gv
    gvb = gv.astype(BF16)
    lane_g = lax.broadcasted_iota(jnp.int32, (GMLP_CHUNK, LANES), 1)
    lo_g = lane_g < GMLP_GROUP_DIM
    mixed_chunks = []
    for q in range(tl // GMLP_CHUNK):
        mixed_pairs = []
        for j in range(GMLP_GROUPS // 2):
            vp = gvb[q * GMLP_CHUNK:(q + 1) * GMLP_CHUNK, j * LANES:(j + 1) * LANES]
            r_even = _dot(wspm_ref[2 * j], vp)
            r_odd = _dot(wspm_ref[2 * j + 1], vp)
            mixed_pairs.append(jnp.where(lo_g, r_even, r_odd))
        mixed_chunks.append(jnp.concatenate(mixed_pairs, axis=1) + bsp_ref[...])
    mixed = jnp.concatenate(mixed_chunks, axis=0) if len(mixed_chunks) > 1 else mixed_chunks[0]
    gmlp_out = gu * mixed

    merged = jnp.concatenate([ssd_out, gmlp_out], axis=1).astype(BF16)
    x1 = x + _dot(merged, wout_ref[...])
    x1_ref[0] = x1

    h2 = x1 * lax.rsqrt(jnp.mean(x1 * x1, axis=-1, keepdims=True) + EPS) * g2_ref[...]
    h2_hi = h2.astype(BF16)
    h2_lo = (h2 - h2_hi.astype(F32)).astype(BF16)
    logit_ref[...] = (_dot_nt(wr_hi_ref[...], h2_hi) + _dot_nt(wr_lo_ref[...], h2_hi)
                      + _dot_nt(wr_hi_ref[...], h2_lo) + br_ref[...])


def _full_spec(shape):
    nd = len(shape)
    return pl.BlockSpec(shape, lambda b, t, _nd=nd: (0,) * _nd)


def _mixer(x, conv0, ssm0, wts, *, tl, l_valid, emit_gv):
    bsz, l, d = x.shape
    nt = l // tl
    assert l % tl == 0 and tl % GMLP_CHUNK == 0 and l_valid % SUBLANES == 0
    assert nt == 1 or l_valid == tl
    in_specs = [
        pl.BlockSpec((1, tl, d), lambda b, t: (b, t, 0)),
        pl.BlockSpec((1, SUBLANES, d), lambda b, t: (b, 0, 0)),
        pl.BlockSpec((1, N_PAIRS, LANES, LANES), lambda b, t: (b, 0, 0, 0)),
    ] + [_full_spec(w.shape) for w in wts]
    out_shape = [
        jax.ShapeDtypeStruct((bsz, l, d), F32),
        jax.ShapeDtypeStruct((ROUTER_ROWS, bsz * l), F32),
        jax.ShapeDtypeStruct((bsz, SUBLANES, d), F32),
        jax.ShapeDtypeStruct((bsz, N_PAIRS, LANES, LANES), F32),
    ]
    out_specs = [
        pl.BlockSpec((1, tl, d), lambda b, t: (b, t, 0)),
        pl.BlockSpec((ROUTER_ROWS, tl), lambda b, t, _nt=nt: (0, b * _nt + t)),
        pl.BlockSpec((1, SUBLANES, d), lambda b, t: (b, 0, 0)),
        pl.BlockSpec((1, N_PAIRS, LANES, LANES), lambda b, t: (b, 0, 0, 0)),
    ]
    if emit_gv:
        out_shape.append(jax.ShapeDtypeStruct((bsz, l, GMLP_WIDTH), F32))
        out_specs.append(pl.BlockSpec((1, tl, GMLP_WIDTH), lambda b, t: (b, t, 0)))
    return pl.pallas_call(
        functools.partial(_mixer_kernel, tl=tl, l_valid=l_valid, emit_gv=emit_gv),
        grid=(bsz, nt),
        in_specs=in_specs,
        out_specs=out_specs,
        out_shape=out_shape,
        scratch_shapes=[
            pltpu.VMEM((tl + SUBLANES, d), F32),
            pltpu.VMEM((N_PAIRS, LANES, LANES), F32),
            pltpu.VMEM((GMLP_GROUPS, GMLP_CHUNK, GMLP_CHUNK), BF16),
        ],
        compiler_params=pltpu.CompilerParams(
            dimension_semantics=("arbitrary", "arbitrary"), vmem_limit_bytes=VMEM_LIMIT),
        name="mixer",
    )(x, conv0, ssm0, *wts)


def _route_kernel(logit_ref, tri_ref, ri_ref, rf_ref, cnt_ref, base_ref, *, tr):
    @pl.when(pl.program_id(0) == 0)
    def _():
        base_ref[...] = jnp.zeros_like(base_ref)

    lg = logit_ref[...]
    sub8 = lax.broadcasted_iota(jnp.int32, (SUBLANES, tr), 0).astype(F32)
    big = float(SUBLANES)
    gl = jnp.where(sub8 < N_EXPERT_GROUPS, lg[0:SUBLANES], -jnp.inf)
    gmax = jnp.max(gl, axis=0, keepdims=True)
    g_sel = jnp.min(jnp.where(gl == gmax, sub8, big), axis=0, keepdims=True)
    p_group = 1.0 / jnp.sum(jnp.exp(gl - gmax), axis=0, keepdims=True)
    el = lg[SUBLANES:2 * SUBLANES]
    for g in range(1, N_EXPERT_GROUPS):
        el = jnp.where(g_sel == g, lg[(g + 1) * SUBLANES:(g + 2) * SUBLANES], el)
    top1 = jnp.max(el, axis=0, keepdims=True)
    i1 = jnp.min(jnp.where(el == top1, sub8, big), axis=0, keepdims=True)
    el2 = jnp.where(sub8 == i1, -jnp.inf, el)
    top2 = jnp.max(el2, axis=0, keepdims=True)
    i2 = jnp.min(jnp.where(el2 == top2, sub8, big), axis=0, keepdims=True)
    ex = jnp.exp(top2 - top1)
    w1 = p_group * (1.0 / (1.0 + ex))
    w2 = p_group * (ex / (1.0 + ex))
    e1 = g_sel * EXPERTS_PER_GROUP + i1
    e2 = g_sel * EXPERTS_PER_GROUP + i2

    sube = lax.broadcasted_iota(jnp.int32, (N_EXPERTS, tr), 0).astype(F32)
    oh1 = (sube == e1)
    oh2 = (sube == e2)
    tri = tri_ref[...]
    cum1 = _dot(jnp.where(oh1, 1.0, 0.0).astype(BF16), tri)
    cum2 = _dot(jnp.where(oh2, 1.0, 0.0).astype(BF16), tri)
    tot1 = jnp.sum(jnp.where(oh1, 1.0, 0.0), axis=1, keepdims=True)
    tot2 = jnp.sum(jnp.where(oh2, 1.0, 0.0), axis=1, keepdims=True)
    base = base_ref[:, 0:1]
    r1 = jnp.sum(jnp.where(oh1, cum1 + base, 0.0), axis=0, keepdims=True)
    r2 = jnp.sum(jnp.where(oh2, cum2 + base + tot1, 0.0), axis=0, keepdims=True)
    new_base = base + tot1 + tot2
    base_ref[...] = jnp.broadcast_to(new_base, base_ref.shape)
    cnt_ref[...] = jnp.broadcast_to(new_base, cnt_ref.shape).astype(jnp.int32)

    rows = jnp.where(sub8 == 0, e1, jnp.where(sub8 == 1, e2, jnp.where(sub8 == 2, r1,
                     jnp.where(sub8 == 3, r2, 0.0))))
    ri_ref[...] = rows.astype(jnp.int32)
    rf_ref[...] = jnp.where(sub8 == 0, w1, jnp.where(sub8 == 1, w2, 0.0))


def _route(logits_t, tr):
    rows, t = logits_t.shape
    assert t % tr == 0
    tri = jnp.triu(jnp.ones((tr, tr), BF16), k=1)
    return pl.pallas_call(
        functools.partial(_route_kernel, tr=tr),
        grid=(t // tr,),
        in_specs=[pl.BlockSpec((rows, tr), lambda i: (0, i)),
                  pl.BlockSpec((tr, tr), lambda i: (0, 0))],
        out_specs=[pl.BlockSpec((SUBLANES, tr), lambda i: (0, i)),
                   pl.BlockSpec((SUBLANES, tr), lambda i: (0, i)),
                   pl.BlockSpec((N_EXPERTS, LANES), lambda i: (0, 0))],
        out_shape=[jax.ShapeDtypeStruct((SUBLANES, t), jnp.int32),
                   jax.ShapeDtypeStruct((SUBLANES, t), F32),
                   jax.ShapeDtypeStruct((N_EXPERTS, LANES), jnp.int32)],
        scratch_shapes=[pltpu.VMEM((N_EXPERTS, LANES), F32)],
        compiler_params=pltpu.CompilerParams(dimension_semantics=("arbitrary",)),
        name="route",
    )(logits_t, tri)


def _gather_rows(idx_ref, n_rows, src_hbm, dst_ref, sem):
    def body(r, carry):
        pltpu.make_async_copy(src_hbm.at[pl.ds(idx_ref[0, 0, r], 1)],
                              dst_ref.at[pl.ds(r, 1)], sem).start()
        return carry
    lax.fori_loop(0, n_rows, body, 0, unroll=8)


def _expert_kernel(be_ref, nused_ref, tokc_ref, tokn_ref, x1_hbm, g2_ref, wg_ref, wu_ref, wd_ref,
                   ys_ref, xbuf, sem, wgb, wub, wdb, *, blk):
    i = pl.program_id(0)
    nused = nused_ref[0]
    slot = i % 2

    @pl.when(jnp.logical_and(i == 0, nused > 0))
    def _():
        _gather_rows(tokc_ref, blk, x1_hbm, xbuf.at[0], sem.at[0])

    @pl.when(i + 1 < nused)
    def _():
        _gather_rows(tokn_ref, blk, x1_hbm, xbuf.at[1 - slot], sem.at[1 - slot])

    changed = jnp.logical_or(i == 0, be_ref[i] != be_ref[jnp.maximum(i - 1, 0)])

    @pl.when(jnp.logical_and(i < nused, changed))
    def _():
        wgb[...] = wg_ref[0].astype(BF16)
        wub[...] = wu_ref[0].astype(BF16)
        wdb[...] = wd_ref[0].astype(BF16)

    @pl.when(i < nused)
    def _():
        pltpu.make_async_copy(x1_hbm.at[pl.ds(0, blk)], xbuf.at[slot], sem.at[slot]).wait()
        xr = xbuf[slot]
        hb = (xr * lax.rsqrt(jnp.mean(xr * xr, axis=-1, keepdims=True) + EPS)
              * g2_ref[...]).astype(BF16)
        act = (_silu(_dot(hb, wgb[...])) * _dot(hb, wub[...])).astype(BF16)
        ys_ref[...] = _dot(act, wdb[...])

    @pl.when(i >= nused)
    def _():
        ys_ref[...] = jnp.zeros_like(ys_ref)


def _experts(x1_flat, slot_tok, blk_expert, n_used, g2, w_gate, w_up, w_down, blk):
    t, d = x1_flat.shape
    nb = blk_expert.shape[0]
    de = w_gate.shape[-1]
    tok3 = slot_tok.reshape(nb, 1, blk)
    grid_spec = pltpu.PrefetchScalarGridSpec(
        num_scalar_prefetch=2,
        grid=(nb,),
        in_specs=[
            pl.BlockSpec((1, 1, blk), lambda i, be, nu: (i, 0, 0), memory_space=pltpu.SMEM),
            pl.BlockSpec((1, 1, blk), lambda i, be, nu, _nb=nb: (jnp.minimum(i + 1, _nb - 1), 0, 0),
                         memory_space=pltpu.SMEM),
            pl.BlockSpec(memory_space=pl.ANY),
            pl.BlockSpec((1, d), lambda i, be, nu: (0, 0)),
            pl.BlockSpec((1, d, de), lambda i, be, nu: (be[i], 0, 0)),
            pl.BlockSpec((1, d, de), lambda i, be, nu: (be[i], 0, 0)),
            pl.BlockSpec((1, de, d), lambda i, be, nu: (be[i], 0, 0)),
        ],
        out_specs=pl.BlockSpec((blk, d), lambda i, be, nu: (i, 0)),
        scratch_shapes=[
            pltpu.VMEM((2, blk, d), F32),
            pltpu.SemaphoreType.DMA((2,)),
            pltpu.VMEM((d, de), BF16),
            pltpu.VMEM((d, de), BF16),
            pltpu.VMEM((de, d), BF16),
        ],
    )
    return pl.pallas_call(
        functools.partial(_expert_kernel, blk=blk),
        grid_spec=grid_spec,
        out_shape=jax.ShapeDtypeStruct((nb * blk, d), F32),
        compiler_params=pltpu.CompilerParams(
            dimension_semantics=("arbitrary",), vmem_limit_bytes=VMEM_LIMIT),
        name="experts",
    )(blk_expert, n_used, tok3, tok3, x1_flat, g2, w_gate, w_up, w_down)


def _combine_kernel(posc_ref, posn_ref, x1_ref, w_ref, gf_ref, ys_hbm, y_ref, ybuf, sem, *, tm, nsteps):
    i = pl.program_id(0)
    slot = i % 2

    @pl.when(i == 0)
    def _():
        _gather_rows(posc_ref, 2 * tm, ys_hbm, ybuf.at[0], sem.at[0])

    @pl.when(i + 1 < nsteps)
    def _():
        _gather_rows(posn_ref, 2 * tm, ys_hbm, ybuf.at[1 - slot], sem.at[1 - slot])

    pltpu.make_async_copy(ys_hbm.at[pl.ds(0, 2 * tm)], ybuf.at[slot], sem.at[slot]).wait()
    w = w_ref[...]
    xo = (x1_ref[...] + w[:, 0:1] * ybuf[slot, 0:tm, :] + w[:, 1:2] * ybuf[slot, tm:2 * tm, :])
    y_ref[...] = xo * lax.rsqrt(jnp.mean(xo * xo, axis=-1, keepdims=True) + EPS) * gf_ref[...]


def _combine(x1_flat, pos, w_col, ys, gf, tm):
    t, d = x1_flat.shape
    nsteps = t // tm
    pos3 = pos.reshape(nsteps, 1, 2 * tm)
    return pl.pallas_call(
        functools.partial(_combine_kernel, tm=tm, nsteps=nsteps),
        grid=(nsteps,),
        in_specs=[
            pl.BlockSpec((1, 1, 2 * tm), lambda i: (i, 0, 0), memory_space=pltpu.SMEM),
            pl.BlockSpec((1, 1, 2 * tm), lambda i, _n=nsteps: (jnp.minimum(i + 1, _n - 1), 0, 0),
                         memory_space=pltpu.SMEM),
            pl.BlockSpec((tm, d), lambda i: (i, 0)),
            pl.BlockSpec((tm, 2), lambda i: (i, 0)),
            pl.BlockSpec((1, d), lambda i: (0, 0)),
            pl.BlockSpec(memory_space=pl.ANY),
        ],
        out_specs=pl.BlockSpec((tm, d), lambda i: (i, 0)),
        out_shape=jax.ShapeDtypeStruct((t, d), F32),
        scratch_shapes=[pltpu.VMEM((2, 2 * tm, d), F32), pltpu.SemaphoreType.DMA((2,))],
        compiler_params=pltpu.CompilerParams(
            dimension_semantics=("arbitrary",), vmem_limit_bytes=VMEM_LIMIT),
        name="combine",
    )(pos3, pos3, x1_flat, w_col, gf, ys)


def _moe_and_norm(x1_flat, logits_t, g2, w_gate, w_up, w_down, gf, *, tr, blk, tm):
    t, d = x1_flat.shape
    ri, rf, cnt = _route(logits_t, tr)
    counts = cnt[:, 0]
    padded = (counts + blk - 1) // blk * blk
    pend = jnp.cumsum(padded)
    pstart = pend - padded
    dest = pstart[ri[0:2]] + ri[2:4]
    nb = (2 * t + blk - 1) // blk + N_EXPERTS
    tok = jnp.arange(t, dtype=jnp.int32)
    slot_tok = jnp.zeros((nb * blk,), jnp.int32).at[dest[0]].set(tok).at[dest[1]].set(tok)
    blk_expert = jnp.minimum(
        jnp.searchsorted(pend, jnp.arange(nb, dtype=jnp.int32) * blk, side="right"),
        N_EXPERTS - 1).astype(jnp.int32)
    n_used = (pend[-1] // blk).astype(jnp.int32).reshape(1)
    ys = _experts(x1_flat, slot_tok, blk_expert, n_used, g2, w_gate, w_up, w_down, blk)
    nsteps = t // tm
    pos = dest.reshape(2, nsteps, tm).transpose(1, 0, 2).reshape(nsteps * 2 * tm)
    return _combine(x1_flat, pos, rf[0:2].T, ys, gf, tm)


def _prep_layer_weights(norm1_g, w_in, conv_w, conv_b, dt_bias, a_log, d_skip, ssm_norm_g, ln_v_g,
                        ln_v_b, w_spatial, b_spatial, w_out, norm2_g, w_rg, b_rg, w_re, b_re):
    d = w_in.shape[0]
    conv_dim = SSD_WIDTH + 2 * SSD_GROUPS * SSD_STATE
    o_xbc = SSD_WIDTH
    o_dt = o_xbc + conv_dim
    o_gu = o_dt + SSD_HEADS
    o_gv = o_gu + GMLP_WIDTH
    w_dt = w_in[:, o_dt:o_gu]
    w_in_r = jnp.concatenate(
        [w_in[:, :o_xbc], w_in[:, o_xbc:o_dt], w_in[:, o_gu:o_gv], w_in[:, o_gv:],
         jnp.repeat(w_dt, SSD_HEAD_DIM, axis=1)], axis=1).astype(BF16)
    rep = lambda v: jnp.repeat(v, SSD_HEAD_DIM)[None, :]
    col = lambda v: v[:, None]
    w_r = jnp.concatenate([w_rg, jnp.zeros((d, SUBLANES - N_EXPERT_GROUPS), F32), w_re], axis=1).T
    b_r = jnp.concatenate([b_rg, jnp.zeros((SUBLANES - N_EXPERT_GROUPS,), F32), b_re])[:, None]
    w_r_hi = w_r.astype(BF16)
    w_r_lo = (w_r - w_r_hi.astype(F32)).astype(BF16)
    bsp = jnp.repeat(b_spatial.T, GMLP_GROUP_DIM, axis=1)
    return (norm1_g[None, :], w_in_r, w_dt.T.astype(BF16), conv_w, conv_b[None, :],
            rep(dt_bias), rep(a_log), col(dt_bias), col(a_log), rep(d_skip), ssm_norm_g[None, :],
            ln_v_g[None, :], ln_v_b[None, :], w_spatial, bsp, w_out.astype(BF16), norm2_g[None, :],
            w_r_hi, w_r_lo, b_r)


def _state_to_pairs(s):
    b = s.shape[0]
    return s.reshape(b, N_PAIRS, 2, SSD_HEAD_DIM, SSD_STATE).transpose(0, 1, 4, 2, 3).reshape(
        b, N_PAIRS, SSD_STATE, 2 * SSD_HEAD_DIM)


def _pairs_to_state(s):
    b = s.shape[0]
    return s.reshape(b, N_PAIRS, SSD_STATE, 2, SSD_HEAD_DIM).transpose(0, 1, 3, 4, 2).reshape(
        b, SSD_HEADS, SSD_HEAD_DIM, SSD_STATE)


def _pick(n, prefs):
    for p in prefs:
        if n % p == 0:
            return p
    return n


def kernel(x_prompt, x_sample, cache_conv, state_ssm, norm1_g, w_in, conv_w, conv_b, dt_bias, a_log, d_skip, ssm_norm_g, ln_v_g, ln_v_b, w_spatial, b_spatial, w_out, norm2_g, w_router_group, b_router_group, w_router_expert, b_router_expert, w_gate, w_up, w_down, final_norm_g):
    depth = w_in.shape[0]
    assert depth == 1, "the combine kernel fuses the final norm, so only a single layer is supported"
    bp, lp, d = x_prompt.shape
    bs, ls, _ = x_sample.shape
    conv_dim = cache_conv.shape[-1]
    ls_pad = -(-ls // GMLP_CHUNK) * GMLP_CHUNK
    tl_p = _pick(lp, (256, 128))
    gf = final_norm_g[None, :]

    yp = x_prompt
    ys = jnp.pad(x_sample, ((0, 0), (0, ls_pad - ls), (0, 0)))
    conv_p, ssm_p, conv_s, ssm_s, v_s = [], [], [], [], []
    for i in range(depth):
        wts = _prep_layer_weights(
            norm1_g[i], w_in[i], conv_w[i], conv_b[i], dt_bias[i], a_log[i], d_skip[i], ssm_norm_g[i],
            ln_v_g[i], ln_v_b[i], w_spatial[i], b_spatial[i], w_out[i], norm2_g[i],
            w_router_group[i], b_router_group[i], w_router_expert[i], b_router_expert[i])
        g2 = norm2_g[i][None, :]
        gfi = gf

        conv0 = jnp.zeros((bp, SUBLANES, conv_dim), F32)
        ssm0 = jnp.zeros((bp, N_PAIRS, LANES, LANES), F32)
        x1, lt, cpo, hpo = _mixer(yp, conv0, ssm0, wts, tl=tl_p, l_valid=tl_p, emit_gv=False)
        tp = bp * lp
        outp = _moe_and_norm(x1.reshape(tp, d), lt, g2, w_gate[i], w_up[i], w_down[i], gfi,
                             tr=_pick(tp, (512, 256, 128)), blk=256, tm=_pick(tp, (256, 128)))
        yp = outp.reshape(bp, lp, d)
        conv_p.append(cpo[:, SUBLANES - (CONV_WIDTH - 1):, :])
        ssm_p.append(_pairs_to_state(hpo))

        conv0 = jnp.pad(cache_conv[i], ((0, 0), (SUBLANES - (CONV_WIDTH - 1), 0), (0, 0)))
        x1, lt, cso, hso, gv = _mixer(ys, conv0, _state_to_pairs(state_ssm[i]), wts,
                                      tl=ls_pad, l_valid=ls, emit_gv=True)
        tsn = bs * ls
        x1v = x1[:, :ls].reshape(tsn, d)
        ltv = lt.reshape(ROUTER_ROWS, bs, ls_pad)[:, :, :ls].reshape(ROUTER_ROWS, tsn)
        outs = _moe_and_norm(x1v, ltv, g2, w_gate[i], w_up[i], w_down[i], gfi,
                             tr=_pick(tsn, (512, 256, 128)), blk=256, tm=_pick(tsn, (256, 128)))
        ys = outs.reshape(bs, ls, d)
        conv_s.append(cso[:, SUBLANES - (CONV_WIDTH - 1):, :])
        ssm_s.append(_pairs_to_state(hso))
        v_s.append(gv[:, :ls])
    return (yp, ys, jnp.stack(conv_p), jnp.stack(ssm_p), jnp.stack(conv_s), jnp.stack(ssm_s),
            jnp.stack(v_s))
```

```python
import functools

import jax
import jax.numpy as jnp
from jax import lax
from jax.experimental import pallas as pl
from jax.experimental.pallas import tpu as pltpu

F32 = jnp.float32
BF16 = jnp.bfloat16
EPS = 1e-6

SSD_HEADS = 8
SSD_HEAD_DIM = 64
SSD_STATE = 128
SSD_GROUPS = 2
SSD_WIDTH = SSD_HEADS * SSD_HEAD_DIM
SSD_CHUNK = 64
CONV_WIDTH = 4
GMLP_GROUPS = 8
GMLP_GROUP_DIM = 64
GMLP_WIDTH = GMLP_GROUPS * GMLP_GROUP_DIM
GMLP_CHUNK = 128
N_EXPERT_GROUPS = 4
EXPERTS_PER_GROUP = 8
N_EXPERTS = N_EXPERT_GROUPS * EXPERTS_PER_GROUP
ROUTER_ROWS = 8 + N_EXPERTS
MOE_BLOCK = 256
DISPATCH_RING = 3

LANES = 128
SUBLANES = 8
N_PAIRS = SSD_HEADS // 2
VMEM_LIMIT = 56 * 1024 * 1024

C_Z = 0
C_XBC = C_Z + SSD_WIDTH
C_GU = C_XBC + SSD_WIDTH + 2 * SSD_GROUPS * SSD_STATE
C_GV = C_GU + GMLP_WIDTH
C_DT = C_GV + GMLP_WIDTH
C_END = C_DT + SSD_WIDTH


def _silu(v):
    return v * (1.0 / (1.0 + jnp.exp(-v)))


def _gelu_tanh(v):
    c = 0.7978845608028654
    return 0.5 * v * (1.0 + jnp.tanh(c * (v + 0.044715 * (v * v * v))))


def _softplus(v):
    return jnp.maximum(v, 0.0) + jnp.log1p(jnp.exp(-jnp.abs(v)))


def _dot(a, b):
    return jnp.dot(a, b, preferred_element_type=F32)


def _dot_nt(a, b):
    return lax.dot_general(a, b, (((1,), (1,)), ((), ())), preferred_element_type=F32)


def _dot_tn(a, b):
    return lax.dot_general(a, b, (((0,), (0,)), ((), ())), preferred_element_type=F32)


def _mixer_kernel(x_ref, conv0_ref, ssm0_ref, g1_ref, win_ref, wdt_ref, convw_ref, convb_ref,
                  dtb_rep_ref, alog_rep_ref, dtb_col_ref, alog_col_ref, dskip_ref, ssmg_ref,
                  lng_ref, lnb_ref, wsp_ref, bsp_ref, wout_ref, g2_ref, wr_hi_ref, wr_lo_ref,
                  br_ref,
                  x1_ref, logit_ref, convo_ref, ssmo_ref, *rest,
                  tl, l_valid, emit_gv):
    if emit_gv:
        gvo_ref, xpad_ref, state_ref, wspm_ref = rest
    else:
        xpad_ref, state_ref, wspm_ref = rest
    t = pl.program_id(1)
    first = jnp.logical_and(pl.program_id(0) == 0, t == 0)

    @pl.when(first)
    def _():
        r = lax.broadcasted_iota(jnp.int32, (GMLP_CHUNK, GMLP_CHUNK), 0)
        c = lax.broadcasted_iota(jnp.int32, (GMLP_CHUNK, GMLP_CHUNK), 1)
        for gi in range(GMLP_GROUPS):
            wspm_ref[gi] = jnp.where(r >= c, wsp_ref[gi], 0.0).astype(BF16)

    @pl.when(t == 0)
    def _():
        xpad_ref[0:SUBLANES, :] = conv0_ref[0]
        state_ref[...] = ssm0_ref[0]

    x = x_ref[0]
    h = x * lax.rsqrt(jnp.mean(x * x, axis=-1, keepdims=True) + EPS) * g1_ref[...]
    hb = h.astype(BF16)
    proj = _dot(hb, win_ref[...])

    xpad_ref[SUBLANES:SUBLANES + tl, :] = proj[:, C_XBC:C_GU]
    conv = convb_ref[...]
    for k in range(CONV_WIDTH):
        off = SUBLANES - (CONV_WIDTH - 1) + k
        conv = conv + xpad_ref[off:off + tl, :] * convw_ref[k:k + 1, :]
    xbc = _silu(conv)
    carry = xpad_ref[l_valid:l_valid + SUBLANES, :]
    xpad_ref[0:SUBLANES, :] = carry
    convo_ref[0] = carry

    xs = xbc[:, 0:SSD_WIDTH]
    bm = xbc[:, SSD_WIDTH:SSD_WIDTH + SSD_GROUPS * SSD_STATE].astype(BF16)
    cm = xbc[:, SSD_WIDTH + SSD_GROUPS * SSD_STATE:].astype(BF16)

    dt = _softplus(proj[:, C_DT:C_END] + dtb_rep_ref[...])
    row = lax.broadcasted_iota(jnp.int32, (tl, SSD_WIDTH), 0)
    if l_valid < tl:
        dt = jnp.where(row < l_valid, dt, 0.0)
    a_rep = -jnp.exp(alog_rep_ref[...])
    acs = dt * a_rep
    row_in_chunk = jnp.bitwise_and(row, SSD_CHUNK - 1)
    sh = 1
    while sh < SSD_CHUNK:
        acs = acs + jnp.where(row_in_chunk >= sh, pltpu.roll(acs, sh, axis=0), 0.0)
        sh *= 2

    dtt = _softplus(_dot_nt(wdt_ref[...], hb) + dtb_col_ref[...])
    lane_t = lax.broadcasted_iota(jnp.int32, (SSD_HEADS, tl), 1)
    if l_valid < tl:
        dtt = jnp.where(lane_t < l_valid, dtt, 0.0)
    acst = dtt * (-jnp.exp(alog_col_ref[...]))
    lane_in_chunk = jnp.bitwise_and(lane_t, SSD_CHUNK - 1)
    sh = 1
    while sh < SSD_CHUNK:
        acst = acst + jnp.where(lane_in_chunk >= sh, pltpu.roll(acst, sh, axis=1), 0.0)
        sh *= 2

    lane = lax.broadcasted_iota(jnp.int32, (SSD_CHUNK, LANES), 1)
    rowc = lax.broadcasted_iota(jnp.int32, (SSD_CHUNK, LANES), 0)
    lo_half = lane < SSD_HEAD_DIM
    causal = rowc >= jnp.bitwise_and(lane, SSD_CHUNK - 1)
    lane1 = lax.broadcasted_iota(jnp.int32, (1, LANES), 1)
    lo_half1 = lane1 < SSD_CHUNK

    y_chunks = []
    for c in range(tl // SSD_CHUNK):
        r0 = c * SSD_CHUNK
        v = acst[:, (c // 2) * LANES:(c // 2 + 1) * LANES]
        vr = pltpu.roll(v, SSD_CHUNK, axis=1)
        v_lo, v_hi = (v, vr) if c % 2 == 0 else (vr, v)
        cb2 = []
        for g in range(SSD_GROUPS):
            cg = cm[r0:r0 + SSD_CHUNK, g * SSD_STATE:(g + 1) * SSD_STATE]
            bg = bm[r0:r0 + SSD_CHUNK, g * SSD_STATE:(g + 1) * SSD_STATE]
            cb2.append(_dot_nt(cg, jnp.concatenate([bg, bg], axis=0)))
        y_pairs = []
        for j in range(N_PAIRS):
            g = j // (N_PAIRS // SSD_GROUPS)
            cg = cm[r0:r0 + SSD_CHUNK, g * SSD_STATE:(g + 1) * SSD_STATE]
            bg = bm[r0:r0 + SSD_CHUNK, g * SSD_STATE:(g + 1) * SSD_STATE]
            sl = slice(j * LANES, (j + 1) * LANES)
            col_a = acs[r0:r0 + SSD_CHUNK, sl]
            row_a = jnp.where(lo_half1, v_lo[2 * j:2 * j + 1, :], v_hi[2 * j + 1:2 * j + 2, :])
            decay = jnp.where(causal, jnp.exp(col_a - row_a), 0.0)
            m = (cb2[g] * decay).astype(BF16)
            xdt = xs[r0:r0 + SSD_CHUNK, sl] * dt[r0:r0 + SSD_CHUNK, sl]
            zbd = jnp.concatenate([jnp.where(lo_half, xdt, 0.0), jnp.where(lo_half, 0.0, xdt)],
                                  axis=0).astype(BF16)
            y_diag = _dot(m, zbd)
            st = state_ref[j]
            y_off = _dot(cg, st.astype(BF16)) * jnp.exp(col_a)
            a_last = acs[r0 + SSD_CHUNK - 1:r0 + SSD_CHUNK, sl]
            zdte = (xdt * jnp.exp(a_last - col_a)).astype(BF16)
            state_ref[j] = st * jnp.exp(a_last) + _dot_tn(bg, zdte)
            y_pairs.append(y_diag + y_off)
        y_chunks.append(jnp.concatenate(y_pairs, axis=1))
    y = jnp.concatenate(y_chunks, axis=0) if len(y_chunks) > 1 else y_chunks[0]
    ssmo_ref[0] = state_ref[...]

    y = y + xs * dskip_ref[...]
    z = proj[:, C_Z:C_XBC]
    gated = y * _silu(z)
    half = SSD_WIDTH // SSD_GROUPS
    outs = []
    for g in range(SSD_GROUPS):
        gg = gated[:, g * half:(g + 1) * half]
        outs.append(gg * lax.rsqrt(jnp.mean(gg * gg, axis=-1, keepdims=True) + EPS))
    ssd_out = jnp.concatenate(outs, axis=1) * ssmg_ref[...]

    gu = _gelu_tanh(proj[:, C_GU:C_GV])
    gv = _gelu_tanh(proj[:, C_GV:C_DT])
    mu = jnp.mean(gv, axis=-1, keepdims=True)
    gvc = gv - mu
    var = jnp.mean(gvc * gvc, axis=-1, keepdims=True)
    gv = gvc * lax.rsqrt(var + EPS) * lng_ref[...] + lnb_ref[...]
    if emit_gv:
        gvo_ref[0] = gv
    gvb = gv.astype(BF16)
    lane_g = lax.broadcasted_iota(jnp.int32, (GMLP_CHUNK, LANES), 1)
    lo_g = lane_g < GMLP_GROUP_DIM
    mixed_chunks = []
    for q in range(tl // GMLP_CHUNK):
        mixed_pairs = []
        for j in range(GMLP_GROUPS // 2):
            vp = gvb[q * GMLP_CHUNK:(q + 1) * GMLP_CHUNK, j * LANES:(j + 1) * LANES]
            r_even = _dot(wspm_ref[2 * j], vp)
            r_odd = _dot(wspm_ref[2 * j + 1], vp)
            mixed_pairs.append(jnp.where(lo_g, r_even, r_odd))
        mixed_chunks.append(jnp.concatenate(mixed_pairs, axis=1) + bsp_ref[...])
    mixed = jnp.concatenate(mixed_chunks, axis=0) if len(mixed_chunks) > 1 else mixed_chunks[0]
    gmlp_out = gu * mixed

    merged = jnp.concatenate([ssd_out, gmlp_out], axis=1).astype(BF16)
    x1 = x + _dot(merged, wout_ref[...])
    x1_ref[0] = x1

    h2 = x1 * lax.rsqrt(jnp.mean(x1 * x1, axis=-1, keepdims=True) + EPS) * g2_ref[...]
    h2_hi = h2.astype(BF16)
    h2_lo = (h2 - h2_hi.astype(F32)).astype(BF16)
    logit_ref[...] = (_dot_nt(wr_hi_ref[...], h2_hi) + _dot_nt(wr_lo_ref[...], h2_hi)
                      + _dot_nt(wr_hi_ref[...], h2_lo) + br_ref[...])


def _full_spec(shape):
    nd = len(shape)
    return pl.BlockSpec(shape, lambda b, t, _nd=nd: (0,) * _nd)


def _mixer(x, conv0, ssm0, wts, *, tl, l_valid, emit_gv):
    bsz, l, d = x.shape
    nt = l // tl
    assert l % tl == 0 and tl % GMLP_CHUNK == 0 and l_valid % SUBLANES == 0
    assert nt == 1 or l_valid == tl
    in_specs = [
        pl.BlockSpec((1, tl, d), lambda b, t: (b, t, 0)),
        pl.BlockSpec((1, SUBLANES, d), lambda b, t: (b, 0, 0)),
        pl.BlockSpec((1, N_PAIRS, LANES, LANES), lambda b, t: (b, 0, 0, 0)),
    ] + [_full_spec(w.shape) for w in wts]
    out_shape = [
        jax.ShapeDtypeStruct((bsz, l, d), F32),
        jax.ShapeDtypeStruct((ROUTER_ROWS, bsz * l), F32),
        jax.ShapeDtypeStruct((bsz, SUBLANES, d), F32),
        jax.ShapeDtypeStruct((bsz, N_PAIRS, LANES, LANES), F32),
    ]
    out_specs = [
        pl.BlockSpec((1, tl, d), lambda b, t: (b, t, 0)),
        pl.BlockSpec((ROUTER_ROWS, tl), lambda b, t, _nt=nt: (0, b * _nt + t)),
        pl.BlockSpec((1, SUBLANES, d), lambda b, t: (b, 0, 0)),
        pl.BlockSpec((1, N_PAIRS, LANES, LANES), lambda b, t: (b, 0, 0, 0)),
    ]
    if emit_gv:
        out_shape.append(jax.ShapeDtypeStruct((bsz, l, GMLP_WIDTH), F32))
        out_specs.append(pl.BlockSpec((1, tl, GMLP_WIDTH), lambda b, t: (b, t, 0)))
    return pl.pallas_call(
        functools.partial(_mixer_kernel, tl=tl, l_valid=l_valid, emit_gv=emit_gv),
        grid=(bsz, nt),
        in_specs=in_specs,
        out_specs=out_specs,
        out_shape=out_shape,
        scratch_shapes=[
            pltpu.VMEM((tl + SUBLANES, d), F32),
            pltpu.VMEM((N_PAIRS, LANES, LANES), F32),
            pltpu.VMEM((GMLP_GROUPS, GMLP_CHUNK, GMLP_CHUNK), BF16),
        ],
        compiler_params=pltpu.CompilerParams(
            dimension_semantics=("arbitrary", "arbitrary"), vmem_limit_bytes=VMEM_LIMIT),
        name="mixer",
    )(x, conv0, ssm0, *wts)


def _route_kernel(logit_ref, tri_ref, dest_ref, w_ref, meta_ref, base_ref, keep_ref, *, tr, blk):
    p = pl.program_id(0)
    i = pl.program_id(1)
    off = pl.multiple_of(i * tr, tr)
    sub8 = lax.broadcasted_iota(jnp.int32, (SUBLANES, tr), 0).astype(F32)
    sube = lax.broadcasted_iota(jnp.int32, (N_EXPERTS, tr), 0).astype(F32)

    @pl.when(jnp.logical_and(p == 0, i == 0))
    def _():
        base_ref[...] = jnp.zeros_like(base_ref)

    @pl.when(p == 0)
    def _():
        _route_pass0(logit_ref, tri_ref, base_ref, keep_ref, off, sub8, sube, tr)

    @pl.when(p == 1)
    def _():
        counts = base_ref[...]
        padded = jnp.floor((counts + float(blk - 1)) * (1.0 / blk)) * float(blk)
        sub_e = lax.broadcasted_iota(jnp.int32, (N_EXPERTS, LANES), 0)
        pend = padded
        sh = 1
        while sh < N_EXPERTS:
            pend = pend + jnp.where(sub_e >= sh, pltpu.roll(pend, sh, axis=0), 0.0)
            sh *= 2
        pstart = pend - padded
        kept = keep_ref[:, pl.ds(off, tr)]
        ps1 = jnp.sum(jnp.where(sube == kept[0:1], pstart[:, 0:1], 0.0), axis=0, keepdims=True)
        ps2 = jnp.sum(jnp.where(sube == kept[1:2], pstart[:, 0:1], 0.0), axis=0, keepdims=True)
        dest = jnp.where(sub8 == 0, ps1 + kept[2:3], jnp.where(sub8 == 1, ps2 + kept[3:4], 0.0))
        dest_ref[...] = dest.astype(jnp.int32)
        w_ref[...] = jnp.where(sub8 == 0, kept[4:5], jnp.where(sub8 == 1, kept[5:6], 0.0))
        lane_e = lax.broadcasted_iota(jnp.int32, (N_EXPERTS, LANES), 1)
        meta = jnp.where(lane_e == 0, counts, jnp.where(lane_e == 1, pstart,
                         jnp.where(lane_e == 2, pend, 0.0)))
        meta_ref[...] = meta.astype(jnp.int32)


def _route_pass0(logit_ref, tri_ref, base_ref, keep_ref, off, sub8, sube, tr):
    lg = logit_ref[...]
    big = float(SUBLANES)
    gl = jnp.where(sub8 < N_EXPERT_GROUPS, lg[0:SUBLANES], -jnp.inf)
    gmax = jnp.max(gl, axis=0, keepdims=True)
    g_sel = jnp.min(jnp.where(gl == gmax, sub8, big), axis=0, keepdims=True)
    p_group = 1.0 / jnp.sum(jnp.exp(gl - gmax), axis=0, keepdims=True)
    el = lg[SUBLANES:2 * SUBLANES]
    for g in range(1, N_EXPERT_GROUPS):
        el = jnp.where(g_sel == g, lg[(g + 1) * SUBLANES:(g + 2) * SUBLANES], el)
    top1 = jnp.max(el, axis=0, keepdims=True)
    i1 = jnp.min(jnp.where(el == top1, sub8, big), axis=0, keepdims=True)
    el2 = jnp.where(sub8 == i1, -jnp.inf, el)
    top2 = jnp.max(el2, axis=0, keepdims=True)
    i2 = jnp.min(jnp.where(el2 == top2, sub8, big), axis=0, keepdims=True)
    ex = jnp.exp(top2 - top1)
    w1 = p_group * (1.0 / (1.0 + ex))
    w2 = p_group * (ex / (1.0 + ex))
    e1 = g_sel * EXPERTS_PER_GROUP + i1
    e2 = g_sel * EXPERTS_PER_GROUP + i2

    oh1 = (sube == e1)
    oh2 = (sube == e2)
    tri = tri_ref[...]
    cum1 = _dot(jnp.where(oh1, 1.0, 0.0).astype(BF16), tri)
    cum2 = _dot(jnp.where(oh2, 1.0, 0.0).astype(BF16), tri)
    tot1 = jnp.sum(jnp.where(oh1, 1.0, 0.0), axis=1, keepdims=True)
    tot2 = jnp.sum(jnp.where(oh2, 1.0, 0.0), axis=1, keepdims=True)
    base = base_ref[:, 0:1]
    r1 = jnp.sum(jnp.where(oh1, cum1 + base, 0.0), axis=0, keepdims=True)
    r2 = jnp.sum(jnp.where(oh2, cum2 + base + tot1, 0.0), axis=0, keepdims=True)
    base_ref[...] = jnp.broadcast_to(base + tot1 + tot2, base_ref.shape)
    keep_ref[:, pl.ds(off, tr)] = jnp.where(
        sub8 == 0, e1, jnp.where(sub8 == 1, e2, jnp.where(sub8 == 2, r1, jnp.where(
            sub8 == 3, r2, jnp.where(sub8 == 4, w1, jnp.where(sub8 == 5, w2, 0.0))))))


def _route(logits_t, tr, blk):
    rows, t = logits_t.shape
    assert t % tr == 0
    nt = t // tr
    tri = jnp.triu(jnp.ones((tr, tr), BF16), k=1)
    return pl.pallas_call(
        functools.partial(_route_kernel, tr=tr, blk=blk),
        grid=(2, nt),
        in_specs=[pl.BlockSpec((rows, tr), lambda p, i, _n=nt: (0, i * (1 - p) + (_n - 1) * p)),
                  pl.BlockSpec((tr, tr), lambda p, i: (0, 0))],
        out_specs=[pl.BlockSpec((SUBLANES, tr), lambda p, i: (0, i * p)),
                   pl.BlockSpec((SUBLANES, tr), lambda p, i: (0, i * p)),
                   pl.BlockSpec((N_EXPERTS, LANES), lambda p, i: (0, 0))],
        out_shape=[jax.ShapeDtypeStruct((SUBLANES, t), jnp.int32),
                   jax.ShapeDtypeStruct((SUBLANES, t), F32),
                   jax.ShapeDtypeStruct((N_EXPERTS, LANES), jnp.int32)],
        scratch_shapes=[pltpu.VMEM((N_EXPERTS, LANES), F32), pltpu.VMEM((SUBLANES, t), F32)],
        compiler_params=pltpu.CompilerParams(dimension_semantics=("arbitrary", "arbitrary")),
        name="route",
    )(logits_t, tri)


def _dispatch_kernel(meta_ref, dest_ref, x1_hbm, xs_hbm, xin, zbuf, lsem, sem, zsem, *,
                     tmd, nsteps, blk, nb):
    i = pl.program_id(0)

    def load_tile(j, slot):
        return pltpu.make_async_copy(x1_hbm.at[pl.ds(pl.multiple_of(j * tmd, tmd), tmd)],
                                     xin.at[slot], lsem.at[slot])

    def step_bytes(slot):
        return pltpu.make_async_copy(x1_hbm.at[pl.ds(0, 2 * tmd)], xs_hbm.at[pl.ds(0, 2 * tmd)],
                                     sem.at[slot])

    def pad_rows(e, fn):
        start = meta_ref[N_EXPERTS + e] + meta_ref[e]
        lax.fori_loop(start, meta_ref[2 * N_EXPERTS + e], fn, 0)

    def zero_row(row):
        return pltpu.make_async_copy(zbuf.at[pl.ds(0, 1)], xs_hbm.at[pl.ds(row, 1)], zsem)

    def zero_block(b):
        return pltpu.make_async_copy(zbuf, xs_hbm.at[pl.ds(pl.multiple_of(b * blk, blk), blk)], zsem)

    @pl.when(i == 0)
    def _():
        zbuf[...] = jnp.zeros_like(zbuf)
        n_used = lax.div(meta_ref[3 * N_EXPERTS - 1], blk)

        def start_row(row, c):
            zero_row(row).start()
            return c

        def wait_row(row, c):
            zero_row(row).wait()
            return c

        def start_blk(b, c):
            zero_block(b).start()
            return c

        def wait_blk(b, c):
            zero_block(b).wait()
            return c

        for e in range(N_EXPERTS):
            pad_rows(e, start_row)
        lax.fori_loop(n_used, nb, start_blk, 0)
        for e in range(N_EXPERTS):
            pad_rows(e, wait_row)
        lax.fori_loop(n_used, nb, wait_blk, 0)

        load_tile(0, 0).start()
        if nsteps > 1:
            load_tile(1, 1).start()

    slot = lax.rem(i, DISPATCH_RING)
    load_tile(i, slot).wait()
    for r in range(tmd):
        for k in range(2):
            pltpu.make_async_copy(xin.at[slot, pl.ds(r, 1)],
                                  xs_hbm.at[pl.ds(dest_ref[0, 0, k * tmd + r], 1)],
                                  sem.at[slot]).start(priority=k)

    @pl.when(i > 0)
    def _():
        step_bytes(lax.rem(i + DISPATCH_RING - 1, DISPATCH_RING)).wait()

    @pl.when(i + 2 < nsteps)
    def _():
        load_tile(i + 2, lax.rem(i + 2, DISPATCH_RING)).start()

    @pl.when(i == nsteps - 1)
    def _():
        step_bytes(slot).wait()


def _dispatch(x1_flat, dest, meta_s, n_slots, tmd, blk):
    t, d = x1_flat.shape
    nsteps = t // tmd
    dest3 = dest.reshape(2, nsteps, tmd).transpose(1, 0, 2).reshape(nsteps, 1, 2 * tmd)
    grid_spec = pltpu.PrefetchScalarGridSpec(
        num_scalar_prefetch=1,
        grid=(nsteps,),
        in_specs=[pl.BlockSpec((1, 1, 2 * tmd), lambda i, m: (i, 0, 0), memory_space=pltpu.SMEM),
                  pl.BlockSpec(memory_space=pl.ANY)],
        out_specs=pl.BlockSpec(memory_space=pl.ANY),
        scratch_shapes=[pltpu.VMEM((DISPATCH_RING, tmd, d), F32), pltpu.VMEM((blk, d), F32),
                        pltpu.SemaphoreType.DMA((DISPATCH_RING,)),
                        pltpu.SemaphoreType.DMA((DISPATCH_RING,)), pltpu.SemaphoreType.DMA(())],
    )
    return pl.pallas_call(
        functools.partial(_dispatch_kernel, tmd=tmd, nsteps=nsteps, blk=blk, nb=n_slots // blk),
        grid_spec=grid_spec,
        out_shape=jax.ShapeDtypeStruct((n_slots, d), F32),
        compiler_params=pltpu.CompilerParams(dimension_semantics=("arbitrary",)),
        name="dispatch",
    )(meta_s, dest3, x1_flat)


def _expert_kernel(be_ref, nused_ref, xs_ref, g2_ref, wg_ref, wu_ref, wd_ref, ys_ref, wgb, wub, wdb):
    i = pl.program_id(0)
    nused = nused_ref[0]
    changed = jnp.logical_or(i == 0, be_ref[i] != be_ref[jnp.maximum(i - 1, 0)])

    @pl.when(jnp.logical_and(i < nused, changed))
    def _():
        wgb[...] = wg_ref[0].astype(BF16)
        wub[...] = wu_ref[0].astype(BF16)
        wdb[...] = wd_ref[0].astype(BF16)

    @pl.when(i < nused)
    def _():
        xr = xs_ref[...]
        hb = (xr * lax.rsqrt(jnp.mean(xr * xr, axis=-1, keepdims=True) + EPS)
              * g2_ref[...]).astype(BF16)
        act = (_silu(_dot(hb, wgb[...])) * _dot(hb, wub[...])).astype(BF16)
        ys_ref[...] = _dot(act, wdb[...])

    @pl.when(i >= nused)
    def _():
        ys_ref[...] = jnp.zeros_like(ys_ref)


def _experts(xs, blk_expert, n_used, g2, w_gate, w_up, w_down, blk):
    n_slots, d = xs.shape
    nb = n_slots // blk
    de = w_gate.shape[-1]
    grid_spec = pltpu.PrefetchScalarGridSpec(
        num_scalar_prefetch=2,
        grid=(nb,),
        in_specs=[
            pl.BlockSpec((blk, d), lambda i, be, nu: (jnp.minimum(i, jnp.maximum(nu[0] - 1, 0)), 0)),
            pl.BlockSpec((1, d), lambda i, be, nu: (0, 0)),
            pl.BlockSpec((1, d, de), lambda i, be, nu: (be[i], 0, 0)),
            pl.BlockSpec((1, d, de), lambda i, be, nu: (be[i], 0, 0)),
            pl.BlockSpec((1, de, d), lambda i, be, nu: (be[i], 0, 0)),
        ],
        out_specs=pl.BlockSpec((blk, d), lambda i, be, nu: (i, 0)),
        scratch_shapes=[
            pltpu.VMEM((d, de), BF16),
            pltpu.VMEM((d, de), BF16),
            pltpu.VMEM((de, d), BF16),
        ],
    )
    return pl.pallas_call(
        _expert_kernel,
        grid_spec=grid_spec,
        out_shape=jax.ShapeDtypeStruct((n_slots, d), F32),
        compiler_params=pltpu.CompilerParams(
            dimension_semantics=("arbitrary",), vmem_limit_bytes=VMEM_LIMIT),
        name="experts",
    )(blk_expert, n_used, xs, g2, w_gate, w_up, w_down)


def _gather_rows(idx_ref, n_rows, src_hbm, dst_ref, sem):
    for r in range(n_rows):
        pltpu.make_async_copy(src_hbm.at[pl.ds(idx_ref[0, 0, r], 1)],
                              dst_ref.at[pl.ds(r, 1)], sem).start(priority=r % 2)


def _combine_kernel(posc_ref, posn_ref, x1_ref, w_ref, gf_ref, ys_hbm, y_ref, ybuf, sem, *, tm, nsteps):
    i = pl.program_id(0)
    slot = i % 2

    @pl.when(i == 0)
    def _():
        _gather_rows(posc_ref, 2 * tm, ys_hbm, ybuf.at[0], sem.at[0])

    @pl.when(i + 1 < nsteps)
    def _():
        _gather_rows(posn_ref, 2 * tm, ys_hbm, ybuf.at[1 - slot], sem.at[1 - slot])

    pltpu.make_async_copy(ys_hbm.at[pl.ds(0, 2 * tm)], ybuf.at[slot], sem.at[slot]).wait()
    w = w_ref[...]
    xo = (x1_ref[...] + w[:, 0:1] * ybuf[slot, 0:tm, :] + w[:, 1:2] * ybuf[slot, tm:2 * tm, :])
    y_ref[...] = xo * lax.rsqrt(jnp.mean(xo * xo, axis=-1, keepdims=True) + EPS) * gf_ref[...]


def _combine(x1_flat, pos, w_col, ys, gf, tm):
    t, d = x1_flat.shape
    nsteps = t // tm
    pos3 = pos.reshape(nsteps, 1, 2 * tm)
    return pl.pallas_call(
        functools.partial(_combine_kernel, tm=tm, nsteps=nsteps),
        grid=(nsteps,),
        in_specs=[
            pl.BlockSpec((1, 1, 2 * tm), lambda i: (i, 0, 0), memory_space=pltpu.SMEM),
            pl.BlockSpec((1, 1, 2 * tm), lambda i, _n=nsteps: (jnp.minimum(i + 1, _n - 1), 0, 0),
                         memory_space=pltpu.SMEM),
            pl.BlockSpec((tm, d), lambda i: (i, 0)),
            pl.BlockSpec((tm, 2), lambda i: (i, 0)),
            pl.BlockSpec((1, d), lambda i: (0, 0)),
            pl.BlockSpec(memory_space=pl.ANY),
        ],
        out_specs=pl.BlockSpec((tm, d), lambda i: (i, 0)),
        out_shape=jax.ShapeDtypeStruct((t, d), F32),
        scratch_shapes=[pltpu.VMEM((2, 2 * tm, d), F32), pltpu.SemaphoreType.DMA((2,))],
        compiler_params=pltpu.CompilerParams(
            dimension_semantics=("arbitrary",), vmem_limit_bytes=VMEM_LIMIT),
        name="combine",
    )(pos3, pos3, x1_flat, w_col, gf, ys)


def _moe_and_norm(x1_flat, logits_t, g2, w_gate, w_up, w_down, gf):
    t, d = x1_flat.shape
    tr = _pick(t, (512, 256, 128))
    tm = _pick(t, (256, 128))
    tmd = _pick(t, (256, 128))
    blk = MOE_BLOCK
    dest8, w8, meta = _route(logits_t, tr, blk)
    dest = dest8[0:2]
    nb = (2 * t + blk - 1) // blk + N_EXPERTS
    pend = meta[:, 2]
    blk_expert = jnp.minimum(
        jnp.sum(jnp.arange(nb, dtype=jnp.int32)[:, None] * blk >= pend[None, :], axis=1),
        N_EXPERTS - 1).astype(jnp.int32)
    n_used = (pend[-1:] // blk).astype(jnp.int32)
    xs = _dispatch(x1_flat, dest, meta[:, 0:3].T.reshape(3 * N_EXPERTS), nb * blk, tmd, blk)
    ys = _experts(xs, blk_expert, n_used, g2, w_gate, w_up, w_down, blk)
    nsteps = t // tm
    pos = dest.reshape(2, nsteps, tm).transpose(1, 0, 2).reshape(nsteps * 2 * tm)
    return _combine(x1_flat, pos, w8[0:2].T, ys, gf, tm)


def _prep_layer_weights(norm1_g, w_in, conv_w, conv_b, dt_bias, a_log, d_skip, ssm_norm_g, ln_v_g,
                        ln_v_b, w_spatial, b_spatial, w_out, norm2_g, w_rg, b_rg, w_re, b_re):
    d = w_in.shape[0]
    conv_dim = SSD_WIDTH + 2 * SSD_GROUPS * SSD_STATE
    o_xbc = SSD_WIDTH
    o_dt = o_xbc + conv_dim
    o_gu = o_dt + SSD_HEADS
    o_gv = o_gu + GMLP_WIDTH
    w_dt = w_in[:, o_dt:o_gu]
    w_in_r = jnp.concatenate(
        [w_in[:, :o_xbc], w_in[:, o_xbc:o_dt], w_in[:, o_gu:o_gv], w_in[:, o_gv:],
         jnp.repeat(w_dt, SSD_HEAD_DIM, axis=1)], axis=1).astype(BF16)
    rep = lambda v: jnp.repeat(v, SSD_HEAD_DIM)[None, :]
    col = lambda v: v[:, None]
    w_r = jnp.concatenate([w_rg, jnp.zeros((d, SUBLANES - N_EXPERT_GROUPS), F32), w_re], axis=1).T
    b_r = jnp.concatenate([b_rg, jnp.zeros((SUBLANES - N_EXPERT_GROUPS,), F32), b_re])[:, None]
    w_r_hi = w_r.astype(BF16)
    w_r_lo = (w_r - w_r_hi.astype(F32)).astype(BF16)
    bsp = jnp.repeat(b_spatial.T, GMLP_GROUP_DIM, axis=1)
    return (norm1_g[None, :], w_in_r, w_dt.T.astype(BF16), conv_w, conv_b[None, :],
            rep(dt_bias), rep(a_log), col(dt_bias), col(a_log), rep(d_skip), ssm_norm_g[None, :],
            ln_v_g[None, :], ln_v_b[None, :], w_spatial, bsp, w_out.astype(BF16), norm2_g[None, :],
            w_r_hi, w_r_lo, b_r)


def _state_to_pairs(s):
    b = s.shape[0]
    return s.reshape(b, N_PAIRS, 2, SSD_HEAD_DIM, SSD_STATE).transpose(0, 1, 4, 2, 3).reshape(
        b, N_PAIRS, SSD_STATE, 2 * SSD_HEAD_DIM)


def _pairs_to_state(s):
    b = s.shape[0]
    return s.reshape(b, N_PAIRS, SSD_STATE, 2, SSD_HEAD_DIM).transpose(0, 1, 3, 4, 2).reshape(
        b, SSD_HEADS, SSD_HEAD_DIM, SSD_STATE)


def _pick(n, prefs):
    for p in prefs:
        if n % p == 0:
            return p
    return n


def kernel(x_prompt, x_sample, cache_conv, state_ssm, norm1_g, w_in, conv_w, conv_b, dt_bias, a_log, d_skip, ssm_norm_g, ln_v_g, ln_v_b, w_spatial, b_spatial, w_out, norm2_g, w_router_group, b_router_group, w_router_expert, b_router_expert, w_gate, w_up, w_down, final_norm_g):
    depth = w_in.shape[0]
    assert depth == 1, "the combine kernel fuses the final norm, so only a single layer is supported"
    bp, lp, d = x_prompt.shape
    bs, ls, _ = x_sample.shape
    conv_dim = cache_conv.shape[-1]
    ls_pad = -(-ls // GMLP_CHUNK) * GMLP_CHUNK
    tl_p = _pick(lp, (256, 128))
    gf = final_norm_g[None, :]

    yp = x_prompt
    ys = jnp.pad(x_sample, ((0, 0), (0, ls_pad - ls), (0, 0)))
    conv_p, ssm_p, conv_s, ssm_s, v_s = [], [], [], [], []
    for i in range(depth):
        wts = _prep_layer_weights(
            norm1_g[i], w_in[i], conv_w[i], conv_b[i], dt_bias[i], a_log[i], d_skip[i], ssm_norm_g[i],
            ln_v_g[i], ln_v_b[i], w_spatial[i], b_spatial[i], w_out[i], norm2_g[i],
            w_router_group[i], b_router_group[i], w_router_expert[i], b_router_expert[i])
        g2 = norm2_g[i][None, :]
        gfi = gf

        conv0 = jnp.zeros((bp, SUBLANES, conv_dim), F32)
        ssm0 = jnp.zeros((bp, N_PAIRS, LANES, LANES), F32)
        x1, lt, cpo, hpo = _mixer(yp, conv0, ssm0, wts, tl=tl_p, l_valid=tl_p, emit_gv=False)
        tp = bp * lp
        outp = _moe_and_norm(x1.reshape(tp, d), lt, g2, w_gate[i], w_up[i], w_down[i], gfi)
        yp = outp.reshape(bp, lp, d)
        conv_p.append(cpo[:, SUBLANES - (CONV_WIDTH - 1):, :])
        ssm_p.append(_pairs_to_state(hpo))

        conv0 = jnp.pad(cache_conv[i], ((0, 0), (SUBLANES - (CONV_WIDTH - 1), 0), (0, 0)))
        x1, lt, cso, hso, gv = _mixer(ys, conv0, _state_to_pairs(state_ssm[i]), wts,
                                      tl=ls_pad, l_valid=ls, emit_gv=True)
        tsn = bs * ls
        x1v = x1[:, :ls].reshape(tsn, d)
        ltv = lt.reshape(ROUTER_ROWS, bs, ls_pad)[:, :, :ls].reshape(ROUTER_ROWS, tsn)
        outs = _moe_and_norm(x1v, ltv, g2, w_gate[i], w_up[i], w_down[i], gfi)
        ys = outs.reshape(bs, ls, d)
        conv_s.append(cso[:, SUBLANES - (CONV_WIDTH - 1):, :])
        ssm_s.append(_pairs_to_state(hso))
        v_s.append(gv[:, :ls])
    return (yp, ys, jnp.stack(conv_p), jnp.stack(ssm_p), jnp.stack(conv_s), jnp.stack(ssm_s),
            jnp.stack(v_s))
```

```python
import functools

import jax
import jax.numpy as jnp
from jax import lax
from jax.experimental import pallas as pl
from jax.experimental.pallas import tpu as pltpu

F32 = jnp.float32
BF16 = jnp.bfloat16
EPS = 1e-6

SSD_HEADS = 8
SSD_HEAD_DIM = 64
SSD_STATE = 128
SSD_GROUPS = 2
SSD_WIDTH = SSD_HEADS * SSD_HEAD_DIM
SSD_CHUNK = 64
CONV_WIDTH = 4
GMLP_GROUPS = 8
GMLP_GROUP_DIM = 64
GMLP_WIDTH = GMLP_GROUPS * GMLP_GROUP_DIM
GMLP_CHUNK = 128
N_EXPERT_GROUPS = 4
EXPERTS_PER_GROUP = 8
N_EXPERTS = N_EXPERT_GROUPS * EXPERTS_PER_GROUP
ROUTER_ROWS = 8 + N_EXPERTS
MOE_BLOCK = 512
MOE_BLOCK_SMALL = 32
DISPATCH_RING = 3

LANES = 128
SUBLANES = 8
N_PAIRS = SSD_HEADS // 2
VMEM_LIMIT = 56 * 1024 * 1024

C_Z = 0
C_XBC = C_Z + SSD_WIDTH
C_GU = C_XBC + SSD_WIDTH + 2 * SSD_GROUPS * SSD_STATE
C_GV = C_GU + GMLP_WIDTH
C_DT = C_GV + GMLP_WIDTH
C_END = C_DT + LANES
PROJ_SLAB = 512


def _silu(v):
    return v * (1.0 / (1.0 + jnp.exp(-v)))


def _gelu_tanh(v):
    c = 0.7978845608028654
    hv = 0.5 * v
    return hv + hv * jnp.tanh(v * (c + (c * 0.044715) * (v * v)))


def _softplus(v):
    return jnp.maximum(v, 0.0) + jnp.log1p(jnp.exp(-jnp.abs(v)))


def _dot(a, b):
    return jnp.dot(a, b, preferred_element_type=F32)


def _dot_nt(a, b):
    return lax.dot_general(a, b, (((1,), (1,)), ((), ())), preferred_element_type=F32)


def _dot_tn(a, b):
    return lax.dot_general(a, b, (((0,), (0,)), ((), ())), preferred_element_type=F32)


def _replicate_heads(v, sel_ref):
    hi = v.astype(BF16)
    r1 = v - hi.astype(F32)
    mid = r1.astype(BF16)
    lo = (r1 - mid.astype(F32)).astype(BF16)
    sel = sel_ref[...]
    return _dot(hi, sel) + _dot(mid, sel) + _dot(lo, sel)


def _mixer_kernel(xb_ref, xf_ref, conv0a_ref, conv0b_ref, ssm0a_ref, ssm0b_ref,
                  g1_ref, win_ref, wdt_ref, convw_ref, convb_ref,
                  dtb_n_ref, alog_n_ref, sel_ref, dtb_col_ref, alog_col_ref, dskip_ref, ssmg_ref,
                  lng_ref, lnb_ref, wsp_ref, bsp_ref, wout_ref, g2_ref, wr_hi_ref, wr_lo_ref,
                  br_ref,
                  x1_ref, logit_ref, convo_ref, ssmo_ref, *rest,
                  tl, nt, l_valid, emit_gv):
    if emit_gv:
        gvo_ref = rest[0]
        rest = rest[1:]
    else:
        gvo_ref = None
    pa_ref, pb_ref, da_ref, db_ref, xpad_ref, state_ref, wspm_ref, zs_ref, gus_ref, gvs_ref = rest
    s = pl.program_id(0)

    def front_pieces(x, pbuf, dbuf):
        hb = []

        def norm():
            h = x * lax.rsqrt(jnp.mean(x * x, axis=-1, keepdims=True) + EPS) * g1_ref[...]
            hb.append(h.astype(BF16))
            dbuf[...] = _dot_nt(wdt_ref[...], hb[0])

        def slab(c0):
            c1 = min(c0 + PROJ_SLAB, C_END)

            def run():
                pbuf[:, c0:c1] = _dot(hb[0], win_ref[:, c0:c1])
            return run
        return [norm] + [slab(c0) for c0 in range(0, C_END, PROJ_SLAB)]

    def front(x, pbuf, dbuf):
        for piece in front_pieces(x, pbuf, dbuf):
            piece()

    @pl.when(s == 0)
    def _():
        r = lax.broadcasted_iota(jnp.int32, (GMLP_CHUNK, GMLP_CHUNK), 0)
        c = lax.broadcasted_iota(jnp.int32, (GMLP_CHUNK, GMLP_CHUNK), 1)
        for gi in range(GMLP_GROUPS):
            wspm_ref[gi] = jnp.where(r >= c, wsp_ref[gi], 0.0).astype(BF16)
        xpad_ref[0:SUBLANES, :] = jnp.zeros((SUBLANES, xpad_ref.shape[1]), F32)
        state_ref[...] = jnp.zeros_like(state_ref)
        front(xb_ref[0], pa_ref, da_ref)
        front(xb_ref[1], pb_ref, db_ref)

    for u, (pbuf, dbuf, conv0_ref, ssm0_ref) in enumerate(
            ((pa_ref, da_ref, conv0a_ref, ssm0a_ref), (pb_ref, db_ref, conv0b_ref, ssm0b_ref))):
        is_first = lax.rem(2 * s + u, nt) == 0
        xpad_ref[0:SUBLANES, :] = jnp.where(is_first, conv0_ref[0], xpad_ref[0:SUBLANES, :])
        for j in range(N_PAIRS):
            state_ref[j] = jnp.where(is_first, ssm0_ref[0, j], state_ref[j])

        xpad_ref[SUBLANES:SUBLANES + tl, :] = pbuf[:, C_XBC:C_GU]
        dtn = _softplus(pbuf[:, C_DT:C_END] + dtb_n_ref[...])
        row = lax.broadcasted_iota(jnp.int32, (tl, LANES), 0)
        if l_valid < tl:
            dtn = jnp.where(row < l_valid, dtn, 0.0)
        acsn = dtn * (-jnp.exp(alog_n_ref[...]))
        row_in_chunk = jnp.bitwise_and(row, SSD_CHUNK - 1)
        sh = 1
        while sh < SSD_CHUNK:
            acsn = acsn + jnp.where(row_in_chunk >= sh, pltpu.roll(acsn, sh, axis=0), 0.0)
            sh *= 2
        dt = _replicate_heads(dtn, sel_ref)
        acs = _replicate_heads(acsn, sel_ref)
        dtt = _softplus(dbuf[...] + dtb_col_ref[...])
        zs_ref[...] = _silu(pbuf[:, C_Z:C_XBC])
        gus_ref[...] = _gelu_tanh(pbuf[:, C_GU:C_GV])
        gv = _gelu_tanh(pbuf[:, C_GV:C_DT])
        mu = jnp.mean(gv, axis=-1, keepdims=True)
        gvc = gv - mu
        var = jnp.mean(gvc * gvc, axis=-1, keepdims=True)
        gv = gvc * lax.rsqrt(var + EPS) * lng_ref[...] + lnb_ref[...]
        if gvo_ref is not None:
            gvo_ref[u] = gv
        gvs_ref[...] = gv.astype(BF16)

        _mixer_back(
            xb_ref[u], dt, acs, dtt, front_pieces(xf_ref[u], pbuf, dbuf), convw_ref=convw_ref,
            convb_ref=convb_ref, alog_col_ref=alog_col_ref, dskip_ref=dskip_ref,
            ssmg_ref=ssmg_ref, bsp_ref=bsp_ref, wout_ref=wout_ref, g2_ref=g2_ref,
            wr_hi_ref=wr_hi_ref, wr_lo_ref=wr_lo_ref, br_ref=br_ref, x1_ref=x1_ref,
            logit_ref=logit_ref, convo_ref=convo_ref, ssmo_ref=ssmo_ref, xpad_ref=xpad_ref,
            state_ref=state_ref, wspm_ref=wspm_ref, zs_ref=zs_ref, gus_ref=gus_ref,
            gvs_ref=gvs_ref, u=u, tl=tl, l_valid=l_valid)


def _mixer_back(x, dt, acs, dtt, fillers, *, convw_ref, convb_ref, alog_col_ref, dskip_ref,
                ssmg_ref, bsp_ref, wout_ref, g2_ref, wr_hi_ref, wr_lo_ref, br_ref, x1_ref, logit_ref,
                convo_ref, ssmo_ref, xpad_ref, state_ref, wspm_ref, zs_ref, gus_ref, gvs_ref,
                u, tl, l_valid):
    fillers = list(fillers)
    n_slots = [tl // SSD_CHUNK + 3]

    def fill():
        for _ in range(-(-len(fillers) // n_slots[0])):
            fillers.pop(0)()
        n_slots[0] -= 1

    fill()
    conv = convb_ref[...]
    for k in range(CONV_WIDTH):
        off = SUBLANES - (CONV_WIDTH - 1) + k
        conv = conv + xpad_ref[off:off + tl, :] * convw_ref[k:k + 1, :]
    xbc = _silu(conv)
    carry = xpad_ref[l_valid:l_valid + SUBLANES, :]
    xpad_ref[0:SUBLANES, :] = carry
    convo_ref[u] = carry

    xs = xbc[:, 0:SSD_WIDTH]
    bm = xbc[:, SSD_WIDTH:SSD_WIDTH + SSD_GROUPS * SSD_STATE].astype(BF16)
    cm = xbc[:, SSD_WIDTH + SSD_GROUPS * SSD_STATE:].astype(BF16)

    lane_t = lax.broadcasted_iota(jnp.int32, (SSD_HEADS, tl), 1)
    if l_valid < tl:
        dtt = jnp.where(lane_t < l_valid, dtt, 0.0)
    acst = dtt * (-jnp.exp(alog_col_ref[...]))
    lane_in_chunk = jnp.bitwise_and(lane_t, SSD_CHUNK - 1)
    sh = 1
    while sh < SSD_CHUNK:
        acst = acst + jnp.where(lane_in_chunk >= sh, pltpu.roll(acst, sh, axis=1), 0.0)
        sh *= 2

    lane = lax.broadcasted_iota(jnp.int32, (SSD_CHUNK, LANES), 1)
    rowc = lax.broadcasted_iota(jnp.int32, (SSD_CHUNK, LANES), 0)
    lo_half = lane < SSD_HEAD_DIM
    causal = rowc >= jnp.bitwise_and(lane, SSD_CHUNK - 1)
    lane1 = lax.broadcasted_iota(jnp.int32, (1, LANES), 1)
    lo_half1 = lane1 < SSD_CHUNK

    y_chunks = []
    for c in range(tl // SSD_CHUNK):
        r0 = c * SSD_CHUNK
        v = acst[:, (c // 2) * LANES:(c // 2 + 1) * LANES]
        vr = pltpu.roll(v, SSD_CHUNK, axis=1)
        v_lo, v_hi = (v, vr) if c % 2 == 0 else (vr, v)
        cb2 = []
        for g in range(SSD_GROUPS):
            cg = cm[r0:r0 + SSD_CHUNK, g * SSD_STATE:(g + 1) * SSD_STATE]
            bg = bm[r0:r0 + SSD_CHUNK, g * SSD_STATE:(g + 1) * SSD_STATE]
            cb2.append(_dot_nt(cg, jnp.concatenate([bg, bg], axis=0)))
        y_pairs = []
        for j in range(N_PAIRS):
            g = j // (N_PAIRS // SSD_GROUPS)
            cg = cm[r0:r0 + SSD_CHUNK, g * SSD_STATE:(g + 1) * SSD_STATE]
            bg = bm[r0:r0 + SSD_CHUNK, g * SSD_STATE:(g + 1) * SSD_STATE]
            sl = slice(j * LANES, (j + 1) * LANES)
            col_a = acs[r0:r0 + SSD_CHUNK, sl]
            row_a = jnp.where(lo_half1, v_lo[2 * j:2 * j + 1, :], v_hi[2 * j + 1:2 * j + 2, :])
            decay = jnp.where(causal, jnp.exp(col_a - row_a), 0.0)
            m = (cb2[g] * decay).astype(BF16)
            xdt = xs[r0:r0 + SSD_CHUNK, sl] * dt[r0:r0 + SSD_CHUNK, sl]
            zbd = jnp.concatenate([jnp.where(lo_half, xdt, 0.0), jnp.where(lo_half, 0.0, xdt)],
                                  axis=0).astype(BF16)
            y_diag = _dot(m, zbd)
            st = state_ref[j]
            y_off = _dot(cg, st.astype(BF16)) * jnp.exp(col_a)
            a_last = acs[r0 + SSD_CHUNK - 1:r0 + SSD_CHUNK, sl]
            zdte = (xdt * jnp.exp(a_last - col_a)).astype(BF16)
            state_ref[j] = st * jnp.exp(a_last) + _dot_tn(bg, zdte)
            y_pairs.append(y_diag + y_off)
        y_chunks.append(jnp.concatenate(y_pairs, axis=1))
        fill()
    y = jnp.concatenate(y_chunks, axis=0) if len(y_chunks) > 1 else y_chunks[0]
    ssmo_ref[u] = state_ref[...]

    y = y + xs * dskip_ref[...]
    gated = y * zs_ref[...]
    half = SSD_WIDTH // SSD_GROUPS
    outs = []
    for g in range(SSD_GROUPS):
        gg = gated[:, g * half:(g + 1) * half]
        outs.append(gg * lax.rsqrt(jnp.mean(gg * gg, axis=-1, keepdims=True) + EPS))
    ssd_out = jnp.concatenate(outs, axis=1) * ssmg_ref[...]
    fill()

    lane_g = lax.broadcasted_iota(jnp.int32, (GMLP_CHUNK, LANES), 1)
    lo_g = lane_g < GMLP_GROUP_DIM
    mixed_chunks = []
    for q in range(tl // GMLP_CHUNK):
        mixed_pairs = []
        for j in range(GMLP_GROUPS // 2):
            vp = gvs_ref[q * GMLP_CHUNK:(q + 1) * GMLP_CHUNK, j * LANES:(j + 1) * LANES]
            r_even = _dot(wspm_ref[2 * j], vp)
            r_odd = _dot(wspm_ref[2 * j + 1], vp)
            mixed_pairs.append(jnp.where(lo_g, r_even, r_odd))
        mixed_chunks.append(jnp.concatenate(mixed_pairs, axis=1) + bsp_ref[...])
    mixed = jnp.concatenate(mixed_chunks, axis=0) if len(mixed_chunks) > 1 else mixed_chunks[0]
    gmlp_out = gus_ref[...] * mixed
    fill()
    assert not fillers

    merged = jnp.concatenate([ssd_out, gmlp_out], axis=1).astype(BF16)
    x1 = x + _dot(merged, wout_ref[...])
    x1_ref[u] = x1

    h2 = x1 * lax.rsqrt(jnp.mean(x1 * x1, axis=-1, keepdims=True) + EPS) * g2_ref[...]
    h2_hi = h2.astype(BF16)
    h2_lo = (h2 - h2_hi.astype(F32)).astype(BF16)
    logit_ref[:, u * tl:(u + 1) * tl] = (
        _dot_nt(wr_hi_ref[...], h2_hi) + _dot_nt(wr_lo_ref[...], h2_hi)
        + _dot_nt(wr_hi_ref[...], h2_lo) + br_ref[...])


def _full_spec(shape):
    nd = len(shape)
    return pl.BlockSpec(shape, lambda s, _nd=nd: (0,) * _nd)


def _mixer(x, conv0, ssm0, wts, *, tl, l_valid, emit_gv):
    bsz, l, d = x.shape
    nt = l // tl
    g = bsz * nt
    assert l % tl == 0 and tl % GMLP_CHUNK == 0 and l_valid % SUBLANES == 0
    assert (nt == 1 or l_valid == tl) and g % 2 == 0
    steps = g // 2
    tile = lambda f: (lambda s: (f(s), 0, 0))
    seq = lambda f: (lambda s: (f(s) // nt,) + (0,) * 2)
    seq4 = lambda f: (lambda s: (f(s) // nt,) + (0,) * 3)
    in_specs = [
        pl.BlockSpec((2, tl, d), tile(lambda s: s)),
        pl.BlockSpec((2, tl, d), tile(lambda s: jnp.minimum(s + 1, steps - 1))),
        pl.BlockSpec((1, SUBLANES, d), seq(lambda s: 2 * s)),
        pl.BlockSpec((1, SUBLANES, d), seq(lambda s: 2 * s + 1)),
        pl.BlockSpec((1, N_PAIRS, LANES, LANES), seq4(lambda s: 2 * s)),
        pl.BlockSpec((1, N_PAIRS, LANES, LANES), seq4(lambda s: 2 * s + 1)),
    ] + [_full_spec(w.shape) for w in wts]
    out_shape = [
        jax.ShapeDtypeStruct((g, tl, d), F32),
        jax.ShapeDtypeStruct((ROUTER_ROWS, g * tl), F32),
        jax.ShapeDtypeStruct((g, SUBLANES, d), F32),
        jax.ShapeDtypeStruct((g, N_PAIRS, LANES, LANES), F32),
    ]
    out_specs = [
        pl.BlockSpec((2, tl, d), lambda s: (s, 0, 0)),
        pl.BlockSpec((ROUTER_ROWS, 2 * tl), lambda s: (0, s)),
        pl.BlockSpec((2, SUBLANES, d), lambda s: (s, 0, 0)),
        pl.BlockSpec((2, N_PAIRS, LANES, LANES), lambda s: (s, 0, 0, 0)),
    ]
    if emit_gv:
        out_shape.append(jax.ShapeDtypeStruct((g, tl, GMLP_WIDTH), F32))
        out_specs.append(pl.BlockSpec((2, tl, GMLP_WIDTH), lambda s: (s, 0, 0)))
    xt = x.reshape(g, tl, d)
    outs = pl.pallas_call(
        functools.partial(_mixer_kernel, tl=tl, nt=nt, l_valid=l_valid, emit_gv=emit_gv),
        grid=(steps,),
        in_specs=in_specs,
        out_specs=out_specs,
        out_shape=out_shape,
        scratch_shapes=[
            pltpu.VMEM((tl, C_END), F32),
            pltpu.VMEM((tl, C_END), F32),
            pltpu.VMEM((SSD_HEADS, tl), F32),
            pltpu.VMEM((SSD_HEADS, tl), F32),
            pltpu.VMEM((tl + SUBLANES, d), F32),
            pltpu.VMEM((N_PAIRS, LANES, LANES), F32),
            pltpu.VMEM((GMLP_GROUPS, GMLP_CHUNK, GMLP_CHUNK), BF16),
            pltpu.VMEM((tl, SSD_WIDTH), F32),
            pltpu.VMEM((tl, GMLP_WIDTH), F32),
            pltpu.VMEM((tl, GMLP_WIDTH), BF16),
        ],
        compiler_params=pltpu.CompilerParams(
            dimension_semantics=("arbitrary",), vmem_limit_bytes=VMEM_LIMIT),
        name="mixer",
    )(xt, xt, conv0, conv0, ssm0, ssm0, *wts)
    x1, lt, cvo, sso = outs[:4]
    last = slice(nt - 1, None, nt)
    res = (x1.reshape(bsz, l, d), lt, cvo[last], sso[last])
    if emit_gv:
        res += (outs[4].reshape(bsz, l, GMLP_WIDTH),)
    return res


def _route_kernel(logit_ref, tri_ref, dest_ref, w_ref, meta_ref, base_ref, keep_ref, *, tr, blk):
    p = pl.program_id(0)
    i = pl.program_id(1)
    off = pl.multiple_of(i * tr, tr)
    sub8 = lax.broadcasted_iota(jnp.int32, (SUBLANES, tr), 0).astype(F32)
    sube = lax.broadcasted_iota(jnp.int32, (N_EXPERTS, tr), 0).astype(F32)

    @pl.when(jnp.logical_and(p == 0, i == 0))
    def _():
        base_ref[...] = jnp.zeros_like(base_ref)

    @pl.when(p == 0)
    def _():
        _route_pass0(logit_ref, tri_ref, base_ref, keep_ref, off, sub8, sube, tr)

    @pl.when(p == 1)
    def _():
        counts = base_ref[...]
        padded = jnp.floor((counts + float(blk - 1)) * (1.0 / blk)) * float(blk)
        sub_e = lax.broadcasted_iota(jnp.int32, (N_EXPERTS, LANES), 0)
        pend = padded
        sh = 1
        while sh < N_EXPERTS:
            pend = pend + jnp.where(sub_e >= sh, pltpu.roll(pend, sh, axis=0), 0.0)
            sh *= 2
        pstart = pend - padded
        kept = keep_ref[:, pl.ds(off, tr)]
        ps1 = jnp.sum(jnp.where(sube == kept[0:1], pstart[:, 0:1], 0.0), axis=0, keepdims=True)
        ps2 = jnp.sum(jnp.where(sube == kept[1:2], pstart[:, 0:1], 0.0), axis=0, keepdims=True)
        dest = jnp.where(sub8 == 0, ps1 + kept[2:3], jnp.where(sub8 == 1, ps2 + kept[3:4], 0.0))
        dest_ref[...] = dest.astype(jnp.int32)
        w_ref[...] = jnp.where(sub8 == 0, kept[4:5], jnp.where(sub8 == 1, kept[5:6], 0.0))
        lane_e = lax.broadcasted_iota(jnp.int32, (N_EXPERTS, LANES), 1)
        meta = jnp.where(lane_e == 0, counts, jnp.where(lane_e == 1, pstart,
                         jnp.where(lane_e == 2, pend, 0.0)))
        meta_ref[...] = meta.astype(jnp.int32)


def _route_pass0(logit_ref, tri_ref, base_ref, keep_ref, off, sub8, sube, tr):
    lg = logit_ref[...]
    big = float(SUBLANES)
    gl = jnp.where(sub8 < N_EXPERT_GROUPS, lg[0:SUBLANES], -jnp.inf)
    gmax = jnp.max(gl, axis=0, keepdims=True)
    g_sel = jnp.min(jnp.where(gl == gmax, sub8, big), axis=0, keepdims=True)
    p_group = 1.0 / jnp.sum(jnp.exp(gl - gmax), axis=0, keepdims=True)
    el = lg[SUBLANES:2 * SUBLANES]
    for g in range(1, N_EXPERT_GROUPS):
        el = jnp.where(g_sel == g, lg[(g + 1) * SUBLANES:(g + 2) * SUBLANES], el)
    top1 = jnp.max(el, axis=0, keepdims=True)
    i1 = jnp.min(jnp.where(el == top1, sub8, big), axis=0, keepdims=True)
    el2 = jnp.where(sub8 == i1, -jnp.inf, el)
    top2 = jnp.max(el2, axis=0, keepdims=True)
    i2 = jnp.min(jnp.where(el2 == top2, sub8, big), axis=0, keepdims=True)
    ex = jnp.exp(top2 - top1)
    w1 = p_group * (1.0 / (1.0 + ex))
    w2 = p_group * (ex / (1.0 + ex))
    e1 = g_sel * EXPERTS_PER_GROUP + i1
    e2 = g_sel * EXPERTS_PER_GROUP + i2

    oh1 = (sube == e1)
    oh2 = (sube == e2)
    tri = tri_ref[...]
    cum1 = _dot(jnp.where(oh1, 1.0, 0.0).astype(BF16), tri)
    cum2 = _dot(jnp.where(oh2, 1.0, 0.0).astype(BF16), tri)
    tot1 = jnp.sum(jnp.where(oh1, 1.0, 0.0), axis=1, keepdims=True)
    tot2 = jnp.sum(jnp.where(oh2, 1.0, 0.0), axis=1, keepdims=True)
    base = base_ref[:, 0:1]
    r1 = jnp.sum(jnp.where(oh1, cum1 + base, 0.0), axis=0, keepdims=True)
    r2 = jnp.sum(jnp.where(oh2, cum2 + base + tot1, 0.0), axis=0, keepdims=True)
    base_ref[...] = jnp.broadcast_to(base + tot1 + tot2, base_ref.shape)
    keep_ref[:, pl.ds(off, tr)] = jnp.where(
        sub8 == 0, e1, jnp.where(sub8 == 1, e2, jnp.where(sub8 == 2, r1, jnp.where(
            sub8 == 3, r2, jnp.where(sub8 == 4, w1, jnp.where(sub8 == 5, w2, 0.0))))))


def _route(logits_t, tr, blk):
    rows, t = logits_t.shape
    assert t % tr == 0
    nt = t // tr
    tri = jnp.triu(jnp.ones((tr, tr), BF16), k=1)
    return pl.pallas_call(
        functools.partial(_route_kernel, tr=tr, blk=blk),
        grid=(2, nt),
        in_specs=[pl.BlockSpec((rows, tr), lambda p, i, _n=nt: (0, i * (1 - p) + (_n - 1) * p)),
                  pl.BlockSpec((tr, tr), lambda p, i: (0, 0))],
        out_specs=[pl.BlockSpec((SUBLANES, tr), lambda p, i: (0, i * p)),
                   pl.BlockSpec((SUBLANES, tr), lambda p, i: (0, i * p)),
                   pl.BlockSpec((N_EXPERTS, LANES), lambda p, i: (0, 0))],
        out_shape=[jax.ShapeDtypeStruct((SUBLANES, t), jnp.int32),
                   jax.ShapeDtypeStruct((SUBLANES, t), F32),
                   jax.ShapeDtypeStruct((N_EXPERTS, LANES), jnp.int32)],
        scratch_shapes=[pltpu.VMEM((N_EXPERTS, LANES), F32), pltpu.VMEM((SUBLANES, t), F32)],
        compiler_params=pltpu.CompilerParams(dimension_semantics=("arbitrary", "arbitrary")),
        name="route",
    )(logits_t, tri)


def _dispatch_kernel(meta_ref, dest_ref, x1_hbm, xs_hbm, xin, zbuf, lsem, sem, zsem, *,
                     tmd, nsteps, blk, nb):
    i = pl.program_id(0)

    def load_tile(j, slot):
        return pltpu.make_async_copy(x1_hbm.at[pl.ds(pl.multiple_of(j * tmd, tmd), tmd)],
                                     xin.at[slot], lsem.at[slot])

    def step_bytes(slot):
        return pltpu.make_async_copy(x1_hbm.at[pl.ds(0, 2 * tmd)], xs_hbm.at[pl.ds(0, 2 * tmd)],
                                     sem.at[slot])

    def pad_rows(e, fn):
        start = meta_ref[N_EXPERTS + e] + meta_ref[e]
        lax.fori_loop(start, meta_ref[2 * N_EXPERTS + e], fn, 0)

    def zero_row(row):
        return pltpu.make_async_copy(zbuf.at[pl.ds(0, 1)], xs_hbm.at[pl.ds(row, 1)], zsem)

    def zero_block(b):
        return pltpu.make_async_copy(zbuf, xs_hbm.at[pl.ds(pl.multiple_of(b * blk, blk), blk)], zsem)

    @pl.when(i == 0)
    def _():
        zbuf[...] = jnp.zeros_like(zbuf)
        n_used = lax.div(meta_ref[3 * N_EXPERTS - 1], blk)

        def start_row(row, c):
            zero_row(row).start()
            return c

        def wait_row(row, c):
            zero_row(row).wait()
            return c

        def start_blk(b, c):
            zero_block(b).start()
            return c

        def wait_blk(b, c):
            zero_block(b).wait()
            return c

        for e in range(N_EXPERTS):
            pad_rows(e, start_row)
        lax.fori_loop(n_used, nb, start_blk, 0)
        for e in range(N_EXPERTS):
            pad_rows(e, wait_row)
        lax.fori_loop(n_used, nb, wait_blk, 0)

        load_tile(0, 0).start()
        if nsteps > 1:
            load_tile(1, 1).start()

    slot = lax.rem(i, DISPATCH_RING)
    load_tile(i, slot).wait()
    for r in range(tmd):
        for k in range(2):
            pltpu.make_async_copy(xin.at[slot, pl.ds(r, 1)],
                                  xs_hbm.at[pl.ds(dest_ref[0, 0, k * tmd + r], 1)],
                                  sem.at[slot]).start(priority=k)

    @pl.when(i > 0)
    def _():
        step_bytes(lax.rem(i + DISPATCH_RING - 1, DISPATCH_RING)).wait()

    @pl.when(i + 2 < nsteps)
    def _():
        load_tile(i + 2, lax.rem(i + 2, DISPATCH_RING)).start()

    @pl.when(i == nsteps - 1)
    def _():
        step_bytes(slot).wait()


def _dispatch(x1_flat, dest, meta_s, n_slots, tmd, blk):
    t, d = x1_flat.shape
    nsteps = t // tmd
    dest3 = dest.reshape(2, nsteps, tmd).transpose(1, 0, 2).reshape(nsteps, 1, 2 * tmd)
    grid_spec = pltpu.PrefetchScalarGridSpec(
        num_scalar_prefetch=1,
        grid=(nsteps,),
        in_specs=[pl.BlockSpec((1, 1, 2 * tmd), lambda i, m: (i, 0, 0), memory_space=pltpu.SMEM),
                  pl.BlockSpec(memory_space=pl.ANY)],
        out_specs=pl.BlockSpec(memory_space=pl.ANY),
        scratch_shapes=[pltpu.VMEM((DISPATCH_RING, tmd, d), F32), pltpu.VMEM((blk, d), F32),
                        pltpu.SemaphoreType.DMA((DISPATCH_RING,)),
                        pltpu.SemaphoreType.DMA((DISPATCH_RING,)), pltpu.SemaphoreType.DMA(())],
    )
    return pl.pallas_call(
        functools.partial(_dispatch_kernel, tmd=tmd, nsteps=nsteps, blk=blk, nb=n_slots // blk),
        grid_spec=grid_spec,
        out_shape=jax.ShapeDtypeStruct((n_slots, d), F32),
        compiler_params=pltpu.CompilerParams(dimension_semantics=("arbitrary",)),
        name="dispatch",
    )(meta_s, dest3, x1_flat)


def _expert_kernel(be_ref, nused_ref, xs_ref, g2_ref, wg_ref, wu_ref, wd_ref, ys_ref, wgb, wub, wdb):
    i = pl.program_id(0)
    nused = nused_ref[0]
    changed = jnp.logical_or(i == 0, be_ref[i] != be_ref[jnp.maximum(i - 1, 0)])

    @pl.when(jnp.logical_and(i < nused, changed))
    def _():
        wgb[...] = wg_ref[0].astype(BF16)
        wub[...] = wu_ref[0].astype(BF16)
        wdb[...] = wd_ref[0].astype(BF16)

    @pl.when(i < nused)
    def _():
        xr = xs_ref[...]
        hb = (xr * lax.rsqrt(jnp.mean(xr * xr, axis=-1, keepdims=True) + EPS)
              * g2_ref[...]).astype(BF16)
        act = (_silu(_dot(hb, wgb[...])) * _dot(hb, wub[...])).astype(BF16)
        ys_ref[...] = _dot(act, wdb[...])

    @pl.when(i >= nused)
    def _():
        ys_ref[...] = jnp.zeros_like(ys_ref)


def _experts(xs, blk_expert, n_used, g2, w_gate, w_up, w_down, blk):
    n_slots, d = xs.shape
    nb = n_slots // blk
    de = w_gate.shape[-1]
    grid_spec = pltpu.PrefetchScalarGridSpec(
        num_scalar_prefetch=2,
        grid=(nb,),
        in_specs=[
            pl.BlockSpec((blk, d), lambda i, be, nu: (jnp.minimum(i, jnp.maximum(nu[0] - 1, 0)), 0)),
            pl.BlockSpec((1, d), lambda i, be, nu: (0, 0)),
            pl.BlockSpec((1, d, de), lambda i, be, nu: (be[i], 0, 0)),
            pl.BlockSpec((1, d, de), lambda i, be, nu: (be[i], 0, 0)),
            pl.BlockSpec((1, de, d), lambda i, be, nu: (be[i], 0, 0)),
        ],
        out_specs=pl.BlockSpec((blk, d), lambda i, be, nu: (i, 0)),
        scratch_shapes=[
            pltpu.VMEM((d, de), BF16),
            pltpu.VMEM((d, de), BF16),
            pltpu.VMEM((de, d), BF16),
        ],
    )
    return pl.pallas_call(
        _expert_kernel,
        grid_spec=grid_spec,
        out_shape=jax.ShapeDtypeStruct((n_slots, d), F32),
        compiler_params=pltpu.CompilerParams(
            dimension_semantics=("arbitrary",), vmem_limit_bytes=VMEM_LIMIT),
        name="experts",
    )(blk_expert, n_used, xs, g2, w_gate, w_up, w_down)


def _gather_rows(idx_ref, n_rows, src_hbm, dst_ref, sem):
    for r in range(n_rows):
        pltpu.make_async_copy(src_hbm.at[pl.ds(idx_ref[0, 0, r], 1)],
                              dst_ref.at[pl.ds(r, 1)], sem).start(priority=r % 2)


def _combine_kernel(posc_ref, posn_ref, x1_ref, w_ref, gf_ref, ys_hbm, y_ref, ybuf, sem, *, tm, nsteps):
    i = pl.program_id(0)
    slot = i % 2

    @pl.when(i == 0)
    def _():
        _gather_rows(posc_ref, 2 * tm, ys_hbm, ybuf.at[0], sem.at[0])

    @pl.when(i + 1 < nsteps)
    def _():
        _gather_rows(posn_ref, 2 * tm, ys_hbm, ybuf.at[1 - slot], sem.at[1 - slot])

    pltpu.make_async_copy(ys_hbm.at[pl.ds(0, 2 * tm)], ybuf.at[slot], sem.at[slot]).wait()
    w = w_ref[...]
    xo = (x1_ref[...] + w[:, 0:1] * ybuf[slot, 0:tm, :] + w[:, 1:2] * ybuf[slot, tm:2 * tm, :])
    y_ref[...] = xo * lax.rsqrt(jnp.mean(xo * xo, axis=-1, keepdims=True) + EPS) * gf_ref[...]


def _combine(x1_flat, pos, w_col, ys, gf, tm):
    t, d = x1_flat.shape
    nsteps = t // tm
    pos3 = pos.reshape(nsteps, 1, 2 * tm)
    return pl.pallas_call(
        functools.partial(_combine_kernel, tm=tm, nsteps=nsteps),
        grid=(nsteps,),
        in_specs=[
            pl.BlockSpec((1, 1, 2 * tm), lambda i: (i, 0, 0), memory_space=pltpu.SMEM),
            pl.BlockSpec((1, 1, 2 * tm), lambda i, _n=nsteps: (jnp.minimum(i + 1, _n - 1), 0, 0),
                         memory_space=pltpu.SMEM),
            pl.BlockSpec((tm, d), lambda i: (i, 0)),
            pl.BlockSpec((tm, 2), lambda i: (i, 0)),
            pl.BlockSpec((1, d), lambda i: (0, 0)),
            pl.BlockSpec(memory_space=pl.ANY),
        ],
        out_specs=pl.BlockSpec((tm, d), lambda i: (i, 0)),
        out_shape=jax.ShapeDtypeStruct((t, d), F32),
        scratch_shapes=[pltpu.VMEM((2, 2 * tm, d), F32), pltpu.SemaphoreType.DMA((2,))],
        compiler_params=pltpu.CompilerParams(
            dimension_semantics=("arbitrary",), vmem_limit_bytes=VMEM_LIMIT),
        name="combine",
    )(pos3, pos3, x1_flat, w_col, gf, ys)


def _moe_and_norm(x1_flat, logits_t, g2, w_gate, w_up, w_down, gf):
    t, d = x1_flat.shape
    tr = _pick(t, (512, 256, 128))
    tm = _pick(t, (256, 128))
    tmd = _pick(t, (256, 128))
    blk = MOE_BLOCK if 2 * t >= 4 * N_EXPERTS * MOE_BLOCK else MOE_BLOCK_SMALL
    dest8, w8, meta = _route(logits_t, tr, blk)
    dest = dest8[0:2]
    nb = (2 * t + blk - 1) // blk + N_EXPERTS
    pend = meta[:, 2]
    blk_expert = jnp.minimum(
        jnp.sum(jnp.arange(nb, dtype=jnp.int32)[:, None] * blk >= pend[None, :], axis=1),
        N_EXPERTS - 1).astype(jnp.int32)
    n_used = (pend[-1:] // blk).astype(jnp.int32)
    xs = _dispatch(x1_flat, dest, meta[:, 0:3].T.reshape(3 * N_EXPERTS), nb * blk, tmd, blk)
    ys = _experts(xs, blk_expert, n_used, g2, w_gate, w_up, w_down, blk)
    nsteps = t // tm
    pos = dest.reshape(2, nsteps, tm).transpose(1, 0, 2).reshape(nsteps * 2 * tm)
    return _combine(x1_flat, pos, w8[0:2].T, ys, gf, tm)


def _prep_layer_weights(norm1_g, w_in, conv_w, conv_b, dt_bias, a_log, d_skip, ssm_norm_g, ln_v_g,
                        ln_v_b, w_spatial, b_spatial, w_out, norm2_g, w_rg, b_rg, w_re, b_re):
    d = w_in.shape[0]
    conv_dim = SSD_WIDTH + 2 * SSD_GROUPS * SSD_STATE
    o_xbc = SSD_WIDTH
    o_dt = o_xbc + conv_dim
    o_gu = o_dt + SSD_HEADS
    o_gv = o_gu + GMLP_WIDTH
    w_dt = w_in[:, o_dt:o_gu]
    w_in_r = jnp.concatenate(
        [w_in[:, :o_xbc], w_in[:, o_xbc:o_dt], w_in[:, o_gu:o_gv], w_in[:, o_gv:],
         jnp.pad(w_dt, ((0, 0), (0, LANES - SSD_HEADS)))], axis=1).astype(BF16)
    rep = lambda v: jnp.repeat(v, SSD_HEAD_DIM)[None, :]
    col = lambda v: v[:, None]
    narrow = lambda v: jnp.pad(v, (0, LANES - SSD_HEADS))[None, :]
    sel = (jnp.arange(LANES)[:, None] == jnp.arange(SSD_WIDTH)[None, :] // SSD_HEAD_DIM).astype(BF16)
    w_r = jnp.concatenate([w_rg, jnp.zeros((d, SUBLANES - N_EXPERT_GROUPS), F32), w_re], axis=1).T
    b_r = jnp.concatenate([b_rg, jnp.zeros((SUBLANES - N_EXPERT_GROUPS,), F32), b_re])[:, None]
    w_r_hi = w_r.astype(BF16)
    w_r_lo = (w_r - w_r_hi.astype(F32)).astype(BF16)
    bsp = jnp.repeat(b_spatial.T, GMLP_GROUP_DIM, axis=1)
    return (norm1_g[None, :], w_in_r, w_dt.T.astype(BF16), conv_w, conv_b[None, :],
            narrow(dt_bias), narrow(a_log), sel, col(dt_bias), col(a_log), rep(d_skip),
            ssm_norm_g[None, :],
            ln_v_g[None, :], ln_v_b[None, :], w_spatial, bsp, w_out.astype(BF16), norm2_g[None, :],
            w_r_hi, w_r_lo, b_r)


def _state_to_pairs(s):
    b = s.shape[0]
    return s.reshape(b, N_PAIRS, 2, SSD_HEAD_DIM, SSD_STATE).transpose(0, 1, 4, 2, 3).reshape(
        b, N_PAIRS, SSD_STATE, 2 * SSD_HEAD_DIM)


def _pairs_to_state(s):
    b = s.shape[0]
    return s.reshape(b, N_PAIRS, SSD_STATE, 2, SSD_HEAD_DIM).transpose(0, 1, 3, 4, 2).reshape(
        b, SSD_HEADS, SSD_HEAD_DIM, SSD_STATE)


def _pick(n, prefs):
    for p in prefs:
        if n % p == 0:
            return p
    return n


def kernel(x_prompt, x_sample, cache_conv, state_ssm, norm1_g, w_in, conv_w, conv_b, dt_bias, a_log, d_skip, ssm_norm_g, ln_v_g, ln_v_b, w_spatial, b_spatial, w_out, norm2_g, w_router_group, b_router_group, w_router_expert, b_router_expert, w_gate, w_up, w_down, final_norm_g):
    depth = w_in.shape[0]
    assert depth == 1, "the combine kernel fuses the final norm, so only a single layer is supported"
    bp, lp, d = x_prompt.shape
    bs, ls, _ = x_sample.shape
    conv_dim = cache_conv.shape[-1]
    ls_pad = -(-ls // GMLP_CHUNK) * GMLP_CHUNK
    tl_p = _pick(lp, (256, 128))
    gf = final_norm_g[None, :]

    yp = x_prompt
    ys = jnp.pad(x_sample, ((0, 0), (0, ls_pad - ls), (0, 0)))
    conv_p, ssm_p, conv_s, ssm_s, v_s = [], [], [], [], []
    for i in range(depth):
        wts = _prep_layer_weights(
            norm1_g[i], w_in[i], conv_w[i], conv_b[i], dt_bias[i], a_log[i], d_skip[i], ssm_norm_g[i],
            ln_v_g[i], ln_v_b[i], w_spatial[i], b_spatial[i], w_out[i], norm2_g[i],
            w_router_group[i], b_router_group[i], w_router_expert[i], b_router_expert[i])
        g2 = norm2_g[i][None, :]
        gfi = gf

        conv0 = jnp.zeros((bp, SUBLANES, conv_dim), F32)
        ssm0 = jnp.zeros((bp, N_PAIRS, LANES, LANES), F32)
        x1, lt, cpo, hpo = _mixer(yp, conv0, ssm0, wts, tl=tl_p, l_valid=tl_p, emit_gv=False)
        tp = bp * lp
        outp = _moe_and_norm(x1.reshape(tp, d), lt, g2, w_gate[i], w_up[i], w_down[i], gfi)
        yp = outp.reshape(bp, lp, d)
        conv_p.append(cpo[:, SUBLANES - (CONV_WIDTH - 1):, :])
        ssm_p.append(_pairs_to_state(hpo))

        conv0 = jnp.pad(cache_conv[i], ((0, 0), (SUBLANES - (CONV_WIDTH - 1), 0), (0, 0)))
        x1, lt, cso, hso, gv = _mixer(ys, conv0, _state_to_pairs(state_ssm[i]), wts,
                                      tl=ls_pad, l_valid=ls, emit_gv=True)
        tsn = bs * ls
        x1v = x1[:, :ls].reshape(tsn, d)
        ltv = lt.reshape(ROUTER_ROWS, bs, ls_pad)[:, :, :ls].reshape(ROUTER_ROWS, tsn)
        outs = _moe_and_norm(x1v, ltv, g2, w_gate[i], w_up[i], w_down[i], gfi)
        ys = outs.reshape(bs, ls, d)
        conv_s.append(cso[:, SUBLANES - (CONV_WIDTH - 1):, :])
        ssm_s.append(_pairs_to_state(hso))
        v_s.append(gv[:, :ls])
    return (yp, ys, jnp.stack(conv_p), jnp.stack(ssm_p), jnp.stack(conv_s), jnp.stack(ssm_s),
            jnp.stack(v_s))
```

```python
import functools

import jax
import jax.numpy as jnp
from jax import lax
from jax.experimental import pallas as pl
from jax.experimental.pallas import tpu as pltpu

F32 = jnp.float32
BF16 = jnp.bfloat16
EPS = 1e-6

SSD_HEADS = 8
SSD_HEAD_DIM = 64
SSD_STATE = 128
SSD_GROUPS = 2
SSD_WIDTH = SSD_HEADS * SSD_HEAD_DIM
SSD_CHUNK = 64
CONV_WIDTH = 4
GMLP_GROUPS = 8
GMLP_GROUP_DIM = 64
GMLP_WIDTH = GMLP_GROUPS * GMLP_GROUP_DIM
GMLP_CHUNK = 128
N_EXPERT_GROUPS = 4
EXPERTS_PER_GROUP = 8
N_EXPERTS = N_EXPERT_GROUPS * EXPERTS_PER_GROUP
ROUTER_ROWS = 8 + N_EXPERTS
MOE_BLOCK = 512
MOE_BLOCK_SMALL = 32
DISPATCH_RING = 3

LANES = 128
SUBLANES = 8
N_PAIRS = SSD_HEADS // 2
VMEM_LIMIT = 56 * 1024 * 1024

C_Z = 0
C_XBC = C_Z + SSD_WIDTH
C_GU = C_XBC + SSD_WIDTH + 2 * SSD_GROUPS * SSD_STATE
C_GV = C_GU + GMLP_WIDTH
C_DT = C_GV + GMLP_WIDTH
C_END = C_DT + LANES
PROJ_SLAB = 256


def _silu(v):
    return v * (1.0 / (1.0 + jnp.exp(-v)))


def _gelu_tanh(v):
    c = 0.7978845608028654
    hv = 0.5 * v
    return hv + hv * jnp.tanh(v * (c + (c * 0.044715) * (v * v)))


def _softplus(v):
    return jnp.maximum(v, 0.0) + jnp.log1p(jnp.exp(-jnp.abs(v)))


def _dot(a, b):
    return jnp.dot(a, b, preferred_element_type=F32)


def _dot_nt(a, b):
    return lax.dot_general(a, b, (((1,), (1,)), ((), ())), preferred_element_type=F32)


def _dot_tn(a, b):
    return lax.dot_general(a, b, (((0,), (0,)), ((), ())), preferred_element_type=F32)


def _replicate_heads(v, sel_ref):
    hi = v.astype(BF16)
    r1 = v - hi.astype(F32)
    mid = r1.astype(BF16)
    lo = (r1 - mid.astype(F32)).astype(BF16)
    sel = sel_ref[...]
    return _dot(hi, sel) + _dot(mid, sel) + _dot(lo, sel)


def _mixer_kernel(xb_ref, xf_ref, conv0a_ref, conv0b_ref, ssm0a_ref, ssm0b_ref,
                  g1_ref, win_ref, wdt_ref, convw_ref, convb_ref,
                  dtb_n_ref, alog_n_ref, sel_ref, dtb_col_ref, alog_col_ref, dskip_ref, ssmg_ref,
                  lng_ref, lnb_ref, wsp_ref, bsp_ref, wout_ref, g2_ref, wr_hi_ref, wr_lo_ref,
                  br_ref,
                  x1_ref, logit_ref, convo_ref, ssmo_ref, *rest,
                  tl, nt, l_valid, emit_gv):
    if emit_gv:
        gvo_ref = rest[0]
        rest = rest[1:]
    else:
        gvo_ref = None
    (pa_ref, pb_ref, da_ref, db_ref, xpad_ref, state_ref, wspm_ref, zs_ref, gus_ref, gvs_ref,
     gvf_ref, dtc_ref, acs_ref, dtt_ref) = rest
    s = pl.program_id(0)

    def drain(pbuf, dbuf):
        xpad_ref[SUBLANES:SUBLANES + tl, :] = pbuf[:, C_XBC:C_GU]
        dtn = _softplus(pbuf[:, C_DT:C_END] + dtb_n_ref[...])
        row = lax.broadcasted_iota(jnp.int32, (tl, LANES), 0)
        if l_valid < tl:
            dtn = jnp.where(row < l_valid, dtn, 0.0)
        acsn = dtn * (-jnp.exp(alog_n_ref[...]))
        row_in_chunk = jnp.bitwise_and(row, SSD_CHUNK - 1)
        sh = 1
        while sh < SSD_CHUNK:
            acsn = acsn + jnp.where(row_in_chunk >= sh, pltpu.roll(acsn, sh, axis=0), 0.0)
            sh *= 2
        dtc_ref[...] = _replicate_heads(dtn, sel_ref)
        acs_ref[...] = _replicate_heads(acsn, sel_ref)
        dtt_ref[...] = _softplus(dbuf[...] + dtb_col_ref[...])
        zs_ref[...] = _silu(pbuf[:, C_Z:C_XBC])
        gus_ref[...] = _gelu_tanh(pbuf[:, C_GU:C_GV])
        gv = _gelu_tanh(pbuf[:, C_GV:C_DT])
        mu = jnp.mean(gv, axis=-1, keepdims=True)
        gvc = gv - mu
        var = jnp.mean(gvc * gvc, axis=-1, keepdims=True)
        gv = gvc * lax.rsqrt(var + EPS) * lng_ref[...] + lnb_ref[...]
        if gvo_ref is not None:
            gvf_ref[...] = gv
        gvs_ref[...] = gv.astype(BF16)

    def front_pieces(x, pbuf, dbuf):
        hb = []

        def norm():
            h = x * lax.rsqrt(jnp.mean(x * x, axis=-1, keepdims=True) + EPS) * g1_ref[...]
            hb.append(h.astype(BF16))
            dbuf[...] = _dot_nt(wdt_ref[...], hb[0])

        def slab(c0):
            c1 = min(c0 + PROJ_SLAB, C_END)

            def run():
                pbuf[:, c0:c1] = _dot(hb[0], win_ref[:, c0:c1])
            return run
        return [norm] + [slab(c0) for c0 in range(0, C_END, PROJ_SLAB)]

    def front(x, pbuf, dbuf):
        for piece in front_pieces(x, pbuf, dbuf):
            piece()

    @pl.when(s == 0)
    def _():
        r = lax.broadcasted_iota(jnp.int32, (GMLP_CHUNK, GMLP_CHUNK), 0)
        c = lax.broadcasted_iota(jnp.int32, (GMLP_CHUNK, GMLP_CHUNK), 1)
        for gi in range(GMLP_GROUPS):
            wspm_ref[gi] = jnp.where(r >= c, wsp_ref[gi], 0.0).astype(BF16)
        xpad_ref[0:SUBLANES, :] = jnp.zeros((SUBLANES, xpad_ref.shape[1]), F32)
        state_ref[...] = jnp.zeros_like(state_ref)
        front(xb_ref[0], pa_ref, da_ref)
        front(xb_ref[1], pb_ref, db_ref)
        drain(pa_ref, da_ref)

    bufs = ((pa_ref, da_ref, conv0a_ref, ssm0a_ref), (pb_ref, db_ref, conv0b_ref, ssm0b_ref))
    for u, (pbuf, dbuf, conv0_ref, ssm0_ref) in enumerate(bufs):
        is_first = lax.rem(2 * s + u, nt) == 0
        xpad_ref[0:SUBLANES, :] = jnp.where(is_first, conv0_ref[0], xpad_ref[0:SUBLANES, :])
        for j in range(N_PAIRS):
            state_ref[j] = jnp.where(is_first, ssm0_ref[0, j], state_ref[j])
        if gvo_ref is not None:
            gvo_ref[u] = gvf_ref[...]

        _mixer_back(
            xb_ref[u], front_pieces(xf_ref[u], pbuf, dbuf), convw_ref=convw_ref,
            convb_ref=convb_ref, alog_col_ref=alog_col_ref, dskip_ref=dskip_ref,
            ssmg_ref=ssmg_ref, bsp_ref=bsp_ref, wout_ref=wout_ref, g2_ref=g2_ref,
            wr_hi_ref=wr_hi_ref, wr_lo_ref=wr_lo_ref, br_ref=br_ref, x1_ref=x1_ref,
            logit_ref=logit_ref, convo_ref=convo_ref, ssmo_ref=ssmo_ref, xpad_ref=xpad_ref,
            state_ref=state_ref, wspm_ref=wspm_ref, zs_ref=zs_ref, gus_ref=gus_ref,
            gvs_ref=gvs_ref, dtc_ref=dtc_ref, acs_ref=acs_ref, dtt_ref=dtt_ref,
            u=u, tl=tl, l_valid=l_valid)
        nxt = bufs[(u + 1) % 2]
        drain(nxt[0], nxt[1])


def _mixer_back(x, fillers, *, convw_ref, convb_ref, alog_col_ref, dskip_ref,
                ssmg_ref, bsp_ref, wout_ref, g2_ref, wr_hi_ref, wr_lo_ref, br_ref, x1_ref, logit_ref,
                convo_ref, ssmo_ref, xpad_ref, state_ref, wspm_ref, zs_ref, gus_ref, gvs_ref,
                dtc_ref, acs_ref, dtt_ref, u, tl, l_valid):
    fillers = list(fillers)
    dt = dtc_ref[...]
    acs = acs_ref[...]
    dtt = dtt_ref[...]

    def fill(n):
        for _ in range(min(n, len(fillers))):
            fillers.pop(0)()

    fill(1)
    conv = convb_ref[...]
    for k in range(CONV_WIDTH):
        off = SUBLANES - (CONV_WIDTH - 1) + k
        conv = conv + xpad_ref[off:off + tl, :] * convw_ref[k:k + 1, :]
        fill(1)
    xbc = _silu(conv)
    carry = xpad_ref[l_valid:l_valid + SUBLANES, :]
    xpad_ref[0:SUBLANES, :] = carry
    convo_ref[u] = carry

    xs = xbc[:, 0:SSD_WIDTH]
    bm = xbc[:, SSD_WIDTH:SSD_WIDTH + SSD_GROUPS * SSD_STATE].astype(BF16)
    cm = xbc[:, SSD_WIDTH + SSD_GROUPS * SSD_STATE:].astype(BF16)

    lane_t = lax.broadcasted_iota(jnp.int32, (SSD_HEADS, tl), 1)
    if l_valid < tl:
        dtt = jnp.where(lane_t < l_valid, dtt, 0.0)
    acst = dtt * (-jnp.exp(alog_col_ref[...]))
    lane_in_chunk = jnp.bitwise_and(lane_t, SSD_CHUNK - 1)
    sh = 1
    while sh < SSD_CHUNK:
        acst = acst + jnp.where(lane_in_chunk >= sh, pltpu.roll(acst, sh, axis=1), 0.0)
        sh *= 2

    lane = lax.broadcasted_iota(jnp.int32, (SSD_CHUNK, LANES), 1)
    rowc = lax.broadcasted_iota(jnp.int32, (SSD_CHUNK, LANES), 0)
    lo_half = lane < SSD_HEAD_DIM
    causal = rowc >= jnp.bitwise_and(lane, SSD_CHUNK - 1)
    lane1 = lax.broadcasted_iota(jnp.int32, (1, LANES), 1)
    lo_half1 = lane1 < SSD_CHUNK

    y_chunks = []
    for c in range(tl // SSD_CHUNK):
        r0 = c * SSD_CHUNK
        v = acst[:, (c // 2) * LANES:(c // 2 + 1) * LANES]
        vr = pltpu.roll(v, SSD_CHUNK, axis=1)
        v_lo, v_hi = (v, vr) if c % 2 == 0 else (vr, v)
        cb2 = []
        for g in range(SSD_GROUPS):
            cg = cm[r0:r0 + SSD_CHUNK, g * SSD_STATE:(g + 1) * SSD_STATE]
            bg = bm[r0:r0 + SSD_CHUNK, g * SSD_STATE:(g + 1) * SSD_STATE]
            cb2.append(_dot_nt(cg, jnp.concatenate([bg, bg], axis=0)))
        y_pairs = []
        for j in range(N_PAIRS):
            g = j // (N_PAIRS // SSD_GROUPS)
            cg = cm[r0:r0 + SSD_CHUNK, g * SSD_STATE:(g + 1) * SSD_STATE]
            bg = bm[r0:r0 + SSD_CHUNK, g * SSD_STATE:(g + 1) * SSD_STATE]
            sl = slice(j * LANES, (j + 1) * LANES)
            col_a = acs[r0:r0 + SSD_CHUNK, sl]
            row_a = jnp.where(lo_half1, v_lo[2 * j:2 * j + 1, :], v_hi[2 * j + 1:2 * j + 2, :])
            decay = jnp.where(causal, jnp.exp(col_a - row_a), 0.0)
            m = (cb2[g] * decay).astype(BF16)
            xdt = xs[r0:r0 + SSD_CHUNK, sl] * dt[r0:r0 + SSD_CHUNK, sl]
            zbd = jnp.concatenate([jnp.where(lo_half, xdt, 0.0), jnp.where(lo_half, 0.0, xdt)],
                                  axis=0).astype(BF16)
            y_diag = _dot(m, zbd)
            st = state_ref[j]
            y_off = _dot(cg, st.astype(BF16)) * jnp.exp(col_a)
            a_last = acs[r0 + SSD_CHUNK - 1:r0 + SSD_CHUNK, sl]
            zdte = (xdt * jnp.exp(a_last - col_a)).astype(BF16)
            state_ref[j] = st * jnp.exp(a_last) + _dot_tn(bg, zdte)
            y_pairs.append(y_diag + y_off)
        y_chunks.append(jnp.concatenate(y_pairs, axis=1))
        fill(1)
    y = jnp.concatenate(y_chunks, axis=0) if len(y_chunks) > 1 else y_chunks[0]
    ssmo_ref[u] = state_ref[...]

    y = y + xs * dskip_ref[...]
    gated = y * zs_ref[...]
    half = SSD_WIDTH // SSD_GROUPS
    outs = []
    for g in range(SSD_GROUPS):
        gg = gated[:, g * half:(g + 1) * half]
        outs.append(gg * lax.rsqrt(jnp.mean(gg * gg, axis=-1, keepdims=True) + EPS))
    ssd_out = jnp.concatenate(outs, axis=1) * ssmg_ref[...]
    fill(1)

    lane_g = lax.broadcasted_iota(jnp.int32, (GMLP_CHUNK, LANES), 1)
    lo_g = lane_g < GMLP_GROUP_DIM
    mixed_chunks = []
    for q in range(tl // GMLP_CHUNK):
        mixed_pairs = []
        for j in range(GMLP_GROUPS // 2):
            vp = gvs_ref[q * GMLP_CHUNK:(q + 1) * GMLP_CHUNK, j * LANES:(j + 1) * LANES]
            r_even = _dot(wspm_ref[2 * j], vp)
            r_odd = _dot(wspm_ref[2 * j + 1], vp)
            mixed_pairs.append(jnp.where(lo_g, r_even, r_odd))
        mixed_chunks.append(jnp.concatenate(mixed_pairs, axis=1) + bsp_ref[...])
    mixed = jnp.concatenate(mixed_chunks, axis=0) if len(mixed_chunks) > 1 else mixed_chunks[0]
    gmlp_out = gus_ref[...] * mixed

    merged = jnp.concatenate([ssd_out, gmlp_out], axis=1).astype(BF16)
    x1 = x + _dot(merged, wout_ref[...])
    x1_ref[u] = x1
    fill(len(fillers))

    h2 = x1 * lax.rsqrt(jnp.mean(x1 * x1, axis=-1, keepdims=True) + EPS) * g2_ref[...]
    h2_hi = h2.astype(BF16)
    h2_lo = (h2 - h2_hi.astype(F32)).astype(BF16)
    logit_ref[:, u * tl:(u + 1) * tl] = (
        _dot_nt(wr_hi_ref[...], h2_hi) + _dot_nt(wr_lo_ref[...], h2_hi)
        + _dot_nt(wr_hi_ref[...], h2_lo) + br_ref[...])


def _full_spec(shape):
    nd = len(shape)
    return pl.BlockSpec(shape, lambda s, _nd=nd: (0,) * _nd)


def _mixer(x, conv0, ssm0, wts, *, tl, l_valid, emit_gv):
    bsz, l, d = x.shape
    nt = l // tl
    g = bsz * nt
    assert l % tl == 0 and tl % GMLP_CHUNK == 0 and l_valid % SUBLANES == 0
    assert (nt == 1 or l_valid == tl) and g % 2 == 0
    steps = g // 2
    tile = lambda f: (lambda s: (f(s), 0, 0))
    seq = lambda f: (lambda s: (f(s) // nt,) + (0,) * 2)
    seq4 = lambda f: (lambda s: (f(s) // nt,) + (0,) * 3)
    in_specs = [
        pl.BlockSpec((2, tl, d), tile(lambda s: s)),
        pl.BlockSpec((2, tl, d), tile(lambda s: jnp.minimum(s + 1, steps - 1))),
        pl.BlockSpec((1, SUBLANES, d), seq(lambda s: 2 * s)),
        pl.BlockSpec((1, SUBLANES, d), seq(lambda s: 2 * s + 1)),
        pl.BlockSpec((1, N_PAIRS, LANES, LANES), seq4(lambda s: 2 * s)),
        pl.BlockSpec((1, N_PAIRS, LANES, LANES), seq4(lambda s: 2 * s + 1)),
    ] + [_full_spec(w.shape) for w in wts]
    out_shape = [
        jax.ShapeDtypeStruct((g, tl, d), F32),
        jax.ShapeDtypeStruct((ROUTER_ROWS, g * tl), F32),
        jax.ShapeDtypeStruct((g, SUBLANES, d), F32),
        jax.ShapeDtypeStruct((g, N_PAIRS, LANES, LANES), F32),
    ]
    out_specs = [
        pl.BlockSpec((2, tl, d), lambda s: (s, 0, 0)),
        pl.BlockSpec((ROUTER_ROWS, 2 * tl), lambda s: (0, s)),
        pl.BlockSpec((2, SUBLANES, d), lambda s: (s, 0, 0)),
        pl.BlockSpec((2, N_PAIRS, LANES, LANES), lambda s: (s, 0, 0, 0)),
    ]
    if emit_gv:
        out_shape.append(jax.ShapeDtypeStruct((g, tl, GMLP_WIDTH), F32))
        out_specs.append(pl.BlockSpec((2, tl, GMLP_WIDTH), lambda s: (s, 0, 0)))
    xt = x.reshape(g, tl, d)
    outs = pl.pallas_call(
        functools.partial(_mixer_kernel, tl=tl, nt=nt, l_valid=l_valid, emit_gv=emit_gv),
        grid=(steps,),
        in_specs=in_specs,
        out_specs=out_specs,
        out_shape=out_shape,
        scratch_shapes=[
            pltpu.VMEM((tl, C_END), F32),
            pltpu.VMEM((tl, C_END), F32),
            pltpu.VMEM((SSD_HEADS, tl), F32),
            pltpu.VMEM((SSD_HEADS, tl), F32),
            pltpu.VMEM((tl + SUBLANES, d), F32),
            pltpu.VMEM((N_PAIRS, LANES, LANES), F32),
            pltpu.VMEM((GMLP_GROUPS, GMLP_CHUNK, GMLP_CHUNK), BF16),
            pltpu.VMEM((tl, SSD_WIDTH), F32),
            pltpu.VMEM((tl, GMLP_WIDTH), F32),
            pltpu.VMEM((tl, GMLP_WIDTH), BF16),
            pltpu.VMEM((tl, GMLP_WIDTH), F32),
            pltpu.VMEM((tl, SSD_WIDTH), F32),
            pltpu.VMEM((tl, SSD_WIDTH), F32),
            pltpu.VMEM((SSD_HEADS, tl), F32),
        ],
        compiler_params=pltpu.CompilerParams(
            dimension_semantics=("arbitrary",), vmem_limit_bytes=VMEM_LIMIT),
        name="mixer",
    )(xt, xt, conv0, conv0, ssm0, ssm0, *wts)
    x1, lt, cvo, sso = outs[:4]
    last = slice(nt - 1, None, nt)
    res = (x1.reshape(bsz, l, d), lt, cvo[last], sso[last])
    if emit_gv:
        res += (outs[4].reshape(bsz, l, GMLP_WIDTH),)
    return res


def _route_kernel(logit_ref, tri_ref, dest_ref, w_ref, meta_ref, base_ref, keep_ref, *, tr, nt, blk):
    i = pl.program_id(0)
    sub8 = lax.broadcasted_iota(jnp.int32, (SUBLANES, tr), 0).astype(F32)
    sube = lax.broadcasted_iota(jnp.int32, (N_EXPERTS, tr), 0).astype(F32)

    @pl.when(i == 0)
    def _():
        base_ref[...] = jnp.zeros_like(base_ref)

    @pl.when(i < nt)
    def _():
        _route_pass0(logit_ref, tri_ref, base_ref, keep_ref, pl.multiple_of(i * tr, tr), sub8, sube, tr)

    @pl.when(i == nt)
    def _():
        counts = base_ref[...]
        padded = jnp.floor((counts + float(blk - 1)) * (1.0 / blk)) * float(blk)
        sub_e = lax.broadcasted_iota(jnp.int32, (N_EXPERTS, LANES), 0)
        pend = padded
        sh = 1
        while sh < N_EXPERTS:
            pend = pend + jnp.where(sub_e >= sh, pltpu.roll(pend, sh, axis=0), 0.0)
            sh *= 2
        pstart = pend - padded
        lane_e = lax.broadcasted_iota(jnp.int32, (N_EXPERTS, LANES), 1)
        meta = jnp.where(lane_e == 0, counts, jnp.where(lane_e == 1, pstart,
                         jnp.where(lane_e == 2, pend, 0.0)))
        meta_ref[...] = meta.astype(jnp.int32)

        def chunk(c, carry):
            sl = pl.ds(pl.multiple_of(c * tr, tr), tr)
            kept = keep_ref[:, sl]
            ps1 = jnp.sum(jnp.where(sube == kept[0:1], pstart[:, 0:1], 0.0), axis=0, keepdims=True)
            ps2 = jnp.sum(jnp.where(sube == kept[1:2], pstart[:, 0:1], 0.0), axis=0, keepdims=True)
            dest = jnp.where(sub8 == 0, ps1 + kept[2:3],
                             jnp.where(sub8 == 1, ps2 + kept[3:4], 0.0))
            dest_ref[:, sl] = dest.astype(jnp.int32)
            w_ref[:, sl] = jnp.where(sub8 == 0, kept[4:5], jnp.where(sub8 == 1, kept[5:6], 0.0))
            return carry
        lax.fori_loop(0, nt, chunk, 0)


def _route_pass0(logit_ref, tri_ref, base_ref, keep_ref, off, sub8, sube, tr):
    lg = logit_ref[...]
    big = float(SUBLANES)
    gl = jnp.where(sub8 < N_EXPERT_GROUPS, lg[0:SUBLANES], -jnp.inf)
    gmax = jnp.max(gl, axis=0, keepdims=True)
    g_sel = jnp.min(jnp.where(gl == gmax, sub8, big), axis=0, keepdims=True)
    p_group = 1.0 / jnp.sum(jnp.exp(gl - gmax), axis=0, keepdims=True)
    el = lg[SUBLANES:2 * SUBLANES]
    for g in range(1, N_EXPERT_GROUPS):
        el = jnp.where(g_sel == g, lg[(g + 1) * SUBLANES:(g + 2) * SUBLANES], el)
    top1 = jnp.max(el, axis=0, keepdims=True)
    i1 = jnp.min(jnp.where(el == top1, sub8, big), axis=0, keepdims=True)
    el2 = jnp.where(sub8 == i1, -jnp.inf, el)
    top2 = jnp.max(el2, axis=0, keepdims=True)
    i2 = jnp.min(jnp.where(el2 == top2, sub8, big), axis=0, keepdims=True)
    ex = jnp.exp(top2 - top1)
    w1 = p_group * (1.0 / (1.0 + ex))
    w2 = p_group * (ex / (1.0 + ex))
    e1 = g_sel * EXPERTS_PER_GROUP + i1
    e2 = g_sel * EXPERTS_PER_GROUP + i2

    oh1 = (sube == e1)
    oh2 = (sube == e2)
    tri = tri_ref[...]
    cum1 = _dot(jnp.where(oh1, 1.0, 0.0).astype(BF16), tri)
    cum2 = _dot(jnp.where(oh2, 1.0, 0.0).astype(BF16), tri)
    tot1 = jnp.sum(jnp.where(oh1, 1.0, 0.0), axis=1, keepdims=True)
    tot2 = jnp.sum(jnp.where(oh2, 1.0, 0.0), axis=1, keepdims=True)
    base = base_ref[:, 0:1]
    r1 = jnp.sum(jnp.where(oh1, cum1 + base, 0.0), axis=0, keepdims=True)
    r2 = jnp.sum(jnp.where(oh2, cum2 + base + tot1, 0.0), axis=0, keepdims=True)
    base_ref[...] = jnp.broadcast_to(base + tot1 + tot2, base_ref.shape)
    keep_ref[:, pl.ds(off, tr)] = jnp.where(
        sub8 == 0, e1, jnp.where(sub8 == 1, e2, jnp.where(sub8 == 2, r1, jnp.where(
            sub8 == 3, r2, jnp.where(sub8 == 4, w1, jnp.where(sub8 == 5, w2, 0.0))))))


def _route(logits_t, tr, blk):
    rows, t = logits_t.shape
    assert t % tr == 0
    nt = t // tr
    tri = jnp.triu(jnp.ones((tr, tr), BF16), k=1)
    return pl.pallas_call(
        functools.partial(_route_kernel, tr=tr, nt=nt, blk=blk),
        grid=(nt + 1,),
        in_specs=[pl.BlockSpec((rows, tr), lambda i, _n=nt: (0, jnp.minimum(i, _n - 1))),
                  pl.BlockSpec((tr, tr), lambda i: (0, 0))],
        out_specs=[pl.BlockSpec((SUBLANES, t), lambda i: (0, 0)),
                   pl.BlockSpec((SUBLANES, t), lambda i: (0, 0)),
                   pl.BlockSpec((N_EXPERTS, LANES), lambda i: (0, 0))],
        out_shape=[jax.ShapeDtypeStruct((SUBLANES, t), jnp.int32),
                   jax.ShapeDtypeStruct((SUBLANES, t), F32),
                   jax.ShapeDtypeStruct((N_EXPERTS, LANES), jnp.int32)],
        scratch_shapes=[pltpu.VMEM((N_EXPERTS, LANES), F32), pltpu.VMEM((SUBLANES, t), F32)],
        compiler_params=pltpu.CompilerParams(
            dimension_semantics=("arbitrary",), vmem_limit_bytes=VMEM_LIMIT),
        name="route",
    )(logits_t, tri)


def _dispatch_kernel(meta_ref, dest_ref, x1_hbm, xs_hbm, xin, zbuf, lsem, sem, zsem, *,
                     tmd, nsteps, blk, nb):
    i = pl.program_id(0)

    def load_tile(j, slot):
        return pltpu.make_async_copy(x1_hbm.at[pl.ds(pl.multiple_of(j * tmd, tmd), tmd)],
                                     xin.at[slot], lsem.at[slot])

    def step_bytes(slot):
        return pltpu.make_async_copy(x1_hbm.at[pl.ds(0, 2 * tmd)], xs_hbm.at[pl.ds(0, 2 * tmd)],
                                     sem.at[slot])

    def pad_rows(e, row_fn, oct_fn):
        start = meta_ref[N_EXPERTS + e] + meta_ref[e]
        start8 = lax.shift_right_logical(start + (SUBLANES - 1), 3)
        lax.fori_loop(start, start8 * SUBLANES, row_fn, 0)
        lax.fori_loop(start8, lax.shift_right_logical(meta_ref[2 * N_EXPERTS + e], 3), oct_fn, 0)

    def zero_row(row):
        return pltpu.make_async_copy(zbuf.at[pl.ds(0, 1)], xs_hbm.at[pl.ds(row, 1)], zsem)

    def zero_oct(o):
        return pltpu.make_async_copy(
            zbuf.at[pl.ds(0, SUBLANES)],
            xs_hbm.at[pl.ds(pl.multiple_of(o * SUBLANES, SUBLANES), SUBLANES)], zsem)

    def zero_block(b):
        return pltpu.make_async_copy(zbuf, xs_hbm.at[pl.ds(pl.multiple_of(b * blk, blk), blk)], zsem)

    @pl.when(i == 0)
    def _():
        zbuf[...] = jnp.zeros_like(zbuf)
        n_used = lax.div(meta_ref[3 * N_EXPERTS - 1], blk)

        def start_row(row, c):
            zero_row(row).start()
            return c

        def wait_row(row, c):
            zero_row(row).wait()
            return c

        def start_oct(o, c):
            zero_oct(o).start()
            return c

        def wait_oct(o, c):
            zero_oct(o).wait()
            return c

        def start_blk(b, c):
            zero_block(b).start()
            return c

        def wait_blk(b, c):
            zero_block(b).wait()
            return c

        for e in range(N_EXPERTS):
            pad_rows(e, start_row, start_oct)
        lax.fori_loop(n_used, nb, start_blk, 0)
        for e in range(N_EXPERTS):
            pad_rows(e, wait_row, wait_oct)
        lax.fori_loop(n_used, nb, wait_blk, 0)

        load_tile(0, 0).start()
        if nsteps > 1:
            load_tile(1, 1).start()

    slot = lax.rem(i, DISPATCH_RING)
    load_tile(i, slot).wait()
    for r in range(tmd):
        for k in range(2):
            pltpu.make_async_copy(xin.at[slot, pl.ds(r, 1)],
                                  xs_hbm.at[pl.ds(dest_ref[0, 0, k * tmd + r], 1)],
                                  sem.at[slot]).start(priority=k)

    @pl.when(i > 0)
    def _():
        step_bytes(lax.rem(i + DISPATCH_RING - 1, DISPATCH_RING)).wait()

    @pl.when(i + 2 < nsteps)
    def _():
        load_tile(i + 2, lax.rem(i + 2, DISPATCH_RING)).start()

    @pl.when(i == nsteps - 1)
    def _():
        step_bytes(slot).wait()


def _dispatch(x1_flat, dest, meta_s, n_slots, tmd, blk):
    t, d = x1_flat.shape
    nsteps = t // tmd
    dest3 = dest.reshape(2, nsteps, tmd).transpose(1, 0, 2).reshape(nsteps, 1, 2 * tmd)
    grid_spec = pltpu.PrefetchScalarGridSpec(
        num_scalar_prefetch=1,
        grid=(nsteps,),
        in_specs=[pl.BlockSpec((1, 1, 2 * tmd), lambda i, m: (i, 0, 0), memory_space=pltpu.SMEM),
                  pl.BlockSpec(memory_space=pl.ANY)],
        out_specs=pl.BlockSpec(memory_space=pl.ANY),
        scratch_shapes=[pltpu.VMEM((DISPATCH_RING, tmd, d), F32), pltpu.VMEM((blk, d), F32),
                        pltpu.SemaphoreType.DMA((DISPATCH_RING,)),
                        pltpu.SemaphoreType.DMA((DISPATCH_RING,)), pltpu.SemaphoreType.DMA(())],
    )
    return pl.pallas_call(
        functools.partial(_dispatch_kernel, tmd=tmd, nsteps=nsteps, blk=blk, nb=n_slots // blk),
        grid_spec=grid_spec,
        out_shape=jax.ShapeDtypeStruct((n_slots, d), F32),
        compiler_params=pltpu.CompilerParams(dimension_semantics=("arbitrary",)),
        name="dispatch",
    )(meta_s, dest3, x1_flat)


def _expert_kernel(be_ref, nused_ref, xs_ref, g2_ref, wg_ref, wu_ref, wd_ref, ys_ref, wgb, wub, wdb):
    i = pl.program_id(0)
    nused = nused_ref[0]
    changed = jnp.logical_or(i == 0, be_ref[i] != be_ref[jnp.maximum(i - 1, 0)])

    @pl.when(jnp.logical_and(i < nused, changed))
    def _():
        wgb[...] = wg_ref[0].astype(BF16)
        wub[...] = wu_ref[0].astype(BF16)
        wdb[...] = wd_ref[0].astype(BF16)

    @pl.when(i < nused)
    def _():
        xr = xs_ref[...]
        hb = (xr * lax.rsqrt(jnp.mean(xr * xr, axis=-1, keepdims=True) + EPS)
              * g2_ref[...]).astype(BF16)
        act = (_silu(_dot(hb, wgb[...])) * _dot(hb, wub[...])).astype(BF16)
        ys_ref[...] = _dot(act, wdb[...])

    @pl.when(i >= nused)
    def _():
        ys_ref[...] = jnp.zeros_like(ys_ref)


def _experts(xs, blk_expert, n_used, g2, w_gate, w_up, w_down, blk):
    n_slots, d = xs.shape
    nb = n_slots // blk
    de = w_gate.shape[-1]
    grid_spec = pltpu.PrefetchScalarGridSpec(
        num_scalar_prefetch=2,
        grid=(nb,),
        in_specs=[
            pl.BlockSpec((blk, d), lambda i, be, nu: (jnp.minimum(i, jnp.maximum(nu[0] - 1, 0)), 0)),
            pl.BlockSpec((1, d), lambda i, be, nu: (0, 0)),
            pl.BlockSpec((1, d, de), lambda i, be, nu: (be[i], 0, 0)),
            pl.BlockSpec((1, d, de), lambda i, be, nu: (be[i], 0, 0)),
            pl.BlockSpec((1, de, d), lambda i, be, nu: (be[i], 0, 0)),
        ],
        out_specs=pl.BlockSpec((blk, d), lambda i, be, nu: (i, 0)),
        scratch_shapes=[
            pltpu.VMEM((d, de), BF16),
            pltpu.VMEM((d, de), BF16),
            pltpu.VMEM((de, d), BF16),
        ],
    )
    return pl.pallas_call(
        _expert_kernel,
        grid_spec=grid_spec,
        out_shape=jax.ShapeDtypeStruct((n_slots, d), F32),
        compiler_params=pltpu.CompilerParams(
            dimension_semantics=("arbitrary",), vmem_limit_bytes=VMEM_LIMIT),
        name="experts",
    )(blk_expert, n_used, xs, g2, w_gate, w_up, w_down)


def _gather_rows(idx_ref, n_rows, src_hbm, dst_ref, sem):
    for r in range(n_rows):
        pltpu.make_async_copy(src_hbm.at[pl.ds(idx_ref[0, 0, r], 1)],
                              dst_ref.at[pl.ds(r, 1)], sem).start(priority=r % 2)


def _combine_kernel(posc_ref, posn_ref, x1_ref, w_ref, gf_ref, ys_hbm, y_ref, ybuf, sem, *, tm, nsteps):
    i = pl.program_id(0)
    slot = i % 2

    @pl.when(i == 0)
    def _():
        _gather_rows(posc_ref, 2 * tm, ys_hbm, ybuf.at[0], sem.at[0])

    @pl.when(i + 1 < nsteps)
    def _():
        _gather_rows(posn_ref, 2 * tm, ys_hbm, ybuf.at[1 - slot], sem.at[1 - slot])

    pltpu.make_async_copy(ys_hbm.at[pl.ds(0, 2 * tm)], ybuf.at[slot], sem.at[slot]).wait()
    w = w_ref[...]
    xo = (x1_ref[...] + w[:, 0:1] * ybuf[slot, 0:tm, :] + w[:, 1:2] * ybuf[slot, tm:2 * tm, :])
    y_ref[...] = xo * lax.rsqrt(jnp.mean(xo * xo, axis=-1, keepdims=True) + EPS) * gf_ref[...]


def _combine(x1_flat, pos, w_col, ys, gf, tm):
    t, d = x1_flat.shape
    nsteps = t // tm
    pos3 = pos.reshape(nsteps, 1, 2 * tm)
    return pl.pallas_call(
        functools.partial(_combine_kernel, tm=tm, nsteps=nsteps),
        grid=(nsteps,),
        in_specs=[
            pl.BlockSpec((1, 1, 2 * tm), lambda i: (i, 0, 0), memory_space=pltpu.SMEM),
            pl.BlockSpec((1, 1, 2 * tm), lambda i, _n=nsteps: (jnp.minimum(i + 1, _n - 1), 0, 0),
                         memory_space=pltpu.SMEM),
            pl.BlockSpec((tm, d), lambda i: (i, 0)),
            pl.BlockSpec((tm, 2), lambda i: (i, 0)),
            pl.BlockSpec((1, d), lambda i: (0, 0)),
            pl.BlockSpec(memory_space=pl.ANY),
        ],
        out_specs=pl.BlockSpec((tm, d), lambda i: (i, 0)),
        out_shape=jax.ShapeDtypeStruct((t, d), F32),
        scratch_shapes=[pltpu.VMEM((2, 2 * tm, d), F32), pltpu.SemaphoreType.DMA((2,))],
        compiler_params=pltpu.CompilerParams(
            dimension_semantics=("arbitrary",), vmem_limit_bytes=VMEM_LIMIT),
        name="combine",
    )(pos3, pos3, x1_flat, w_col, gf, ys)


def _moe_and_norm(x1_flat, logits_t, g2, w_gate, w_up, w_down, gf):
    t, d = x1_flat.shape
    tr = _pick(t, (512, 256, 128))
    tm = _pick(t, (512, 256, 128))
    tmd = _pick(t, (512, 256, 128))
    blk = MOE_BLOCK if 2 * t >= 4 * N_EXPERTS * MOE_BLOCK else MOE_BLOCK_SMALL
    dest8, w8, meta = _route(logits_t, tr, blk)
    dest = dest8[0:2]
    nb = (2 * t + blk - 1) // blk + N_EXPERTS
    pend = meta[:, 2]
    blk_expert = jnp.minimum(
        jnp.sum(jnp.arange(nb, dtype=jnp.int32)[:, None] * blk >= pend[None, :], axis=1),
        N_EXPERTS - 1).astype(jnp.int32)
    n_used = (pend[-1:] // blk).astype(jnp.int32)
    xs = _dispatch(x1_flat, dest, meta[:, 0:3].T.reshape(3 * N_EXPERTS), nb * blk, tmd, blk)
    ys = _experts(xs, blk_expert, n_used, g2, w_gate, w_up, w_down, blk)
    nsteps = t // tm
    pos = dest.reshape(2, nsteps, tm).transpose(1, 0, 2).reshape(nsteps * 2 * tm)
    return _combine(x1_flat, pos, w8[0:2].T, ys, gf, tm)


def _prep_layer_weights(norm1_g, w_in, conv_w, conv_b, dt_bias, a_log, d_skip, ssm_norm_g, ln_v_g,
                        ln_v_b, w_spatial, b_spatial, w_out, norm2_g, w_rg, b_rg, w_re, b_re):
    d = w_in.shape[0]
    conv_dim = SSD_WIDTH + 2 * SSD_GROUPS * SSD_STATE
    o_xbc = SSD_WIDTH
    o_dt = o_xbc + conv_dim
    o_gu = o_dt + SSD_HEADS
    o_gv = o_gu + GMLP_WIDTH
    w_dt = w_in[:, o_dt:o_gu]
    w_in_r = jnp.concatenate(
        [w_in[:, :o_xbc], w_in[:, o_xbc:o_dt], w_in[:, o_gu:o_gv], w_in[:, o_gv:],
         jnp.pad(w_dt, ((0, 0), (0, LANES - SSD_HEADS)))], axis=1).astype(BF16)
    rep = lambda v: jnp.repeat(v, SSD_HEAD_DIM)[None, :]
    col = lambda v: v[:, None]
    narrow = lambda v: jnp.pad(v, (0, LANES - SSD_HEADS))[None, :]
    sel = (jnp.arange(LANES)[:, None] == jnp.arange(SSD_WIDTH)[None, :] // SSD_HEAD_DIM).astype(BF16)
    w_r = jnp.concatenate([w_rg, jnp.zeros((d, SUBLANES - N_EXPERT_GROUPS), F32), w_re], axis=1).T
    b_r = jnp.concatenate([b_rg, jnp.zeros((SUBLANES - N_EXPERT_GROUPS,), F32), b_re])[:, None]
    w_r_hi = w_r.astype(BF16)
    w_r_lo = (w_r - w_r_hi.astype(F32)).astype(BF16)
    bsp = jnp.repeat(b_spatial.T, GMLP_GROUP_DIM, axis=1)
    return (norm1_g[None, :], w_in_r, w_dt.T.astype(BF16), conv_w, conv_b[None, :],
            narrow(dt_bias), narrow(a_log), sel, col(dt_bias), col(a_log), rep(d_skip),
            ssm_norm_g[None, :],
            ln_v_g[None, :], ln_v_b[None, :], w_spatial, bsp, w_out.astype(BF16), norm2_g[None, :],
            w_r_hi, w_r_lo, b_r)


def _state_to_pairs(s):
    b = s.shape[0]
    return s.reshape(b, N_PAIRS, 2, SSD_HEAD_DIM, SSD_STATE).transpose(0, 1, 4, 2, 3).reshape(
        b, N_PAIRS, SSD_STATE, 2 * SSD_HEAD_DIM)


def _pairs_to_state(s):
    b = s.shape[0]
    return s.reshape(b, N_PAIRS, SSD_STATE, 2, SSD_HEAD_DIM).transpose(0, 1, 3, 4, 2).reshape(
        b, SSD_HEADS, SSD_HEAD_DIM, SSD_STATE)


def _pick(n, prefs):
    for p in prefs:
        if n % p == 0:
            return p
    return n


def kernel(x_prompt, x_sample, cache_conv, state_ssm, norm1_g, w_in, conv_w, conv_b, dt_bias, a_log, d_skip, ssm_norm_g, ln_v_g, ln_v_b, w_spatial, b_spatial, w_out, norm2_g, w_router_group, b_router_group, w_router_expert, b_router_expert, w_gate, w_up, w_down, final_norm_g):
    depth = w_in.shape[0]
    assert depth == 1, "the combine kernel fuses the final norm, so only a single layer is supported"
    bp, lp, d = x_prompt.shape
    bs, ls, _ = x_sample.shape
    conv_dim = cache_conv.shape[-1]
    ls_pad = -(-ls // GMLP_CHUNK) * GMLP_CHUNK
    tl_p = _pick(lp, (256, 128))
    gf = final_norm_g[None, :]

    yp = x_prompt
    ys = jnp.pad(x_sample, ((0, 0), (0, ls_pad - ls), (0, 0)))
    conv_p, ssm_p, conv_s, ssm_s, v_s = [], [], [], [], []
    for i in range(depth):
        wts = _prep_layer_weights(
            norm1_g[i], w_in[i], conv_w[i], conv_b[i], dt_bias[i], a_log[i], d_skip[i], ssm_norm_g[i],
            ln_v_g[i], ln_v_b[i], w_spatial[i], b_spatial[i], w_out[i], norm2_g[i],
            w_router_group[i], b_router_group[i], w_router_expert[i], b_router_expert[i])
        g2 = norm2_g[i][None, :]
        gfi = gf

        conv0 = jnp.zeros((bp, SUBLANES, conv_dim), F32)
        ssm0 = jnp.zeros((bp, N_PAIRS, LANES, LANES), F32)
        x1, lt, cpo, hpo = _mixer(yp, conv0, ssm0, wts, tl=tl_p, l_valid=tl_p, emit_gv=False)
        tp = bp * lp
        outp = _moe_and_norm(x1.reshape(tp, d), lt, g2, w_gate[i], w_up[i], w_down[i], gfi)
        yp = outp.reshape(bp, lp, d)
        conv_p.append(cpo[:, SUBLANES - (CONV_WIDTH - 1):, :])
        ssm_p.append(_pairs_to_state(hpo))

        conv0 = jnp.pad(cache_conv[i], ((0, 0), (SUBLANES - (CONV_WIDTH - 1), 0), (0, 0)))
        x1, lt, cso, hso, gv = _mixer(ys, conv0, _state_to_pairs(state_ssm[i]), wts,
                                      tl=ls_pad, l_valid=ls, emit_gv=True)
        tsn = bs * ls
        x1v = x1[:, :ls].reshape(tsn, d)
        ltv = lt.reshape(ROUTER_ROWS, bs, ls_pad)[:, :, :ls].reshape(ROUTER_ROWS, tsn)
        outs = _moe_and_norm(x1v, ltv, g2, w_gate[i], w_up[i], w_down[i], gfi)
        ys = outs.reshape(bs, ls, d)
        conv_s.append(cso[:, SUBLANES - (CONV_WIDTH - 1):, :])
        ssm_s.append(_pairs_to_state(hso))
        v_s.append(gv[:, :ls])
    return (yp, ys, jnp.stack(conv_p), jnp.stack(ssm_p), jnp.stack(conv_s), jnp.stack(ssm_s),
            jnp.stack(v_s))
```

```python
import functools

import jax
import jax.numpy as jnp
from jax import lax
from jax.experimental import pallas as pl
from jax.experimental.pallas import tpu as pltpu

F32 = jnp.float32
BF16 = jnp.bfloat16
EPS = 1e-6

SSD_HEADS = 8
SSD_HEAD_DIM = 64
SSD_STATE = 128
SSD_GROUPS = 2
SSD_WIDTH = SSD_HEADS * SSD_HEAD_DIM
SSD_CHUNK = 64
CONV_WIDTH = 4
GMLP_GROUPS = 8
GMLP_GROUP_DIM = 64
GMLP_WIDTH = GMLP_GROUPS * GMLP_GROUP_DIM
GMLP_CHUNK = 128
N_EXPERT_GROUPS = 4
EXPERTS_PER_GROUP = 8
N_EXPERTS = N_EXPERT_GROUPS * EXPERTS_PER_GROUP
ROUTER_ROWS = 8 + N_EXPERTS
MOE_BLOCK = 512
MOE_BLOCK_SMALL = 32
DISPATCH_RING = 3
EXPERT_RING = 3

LANES = 128
SUBLANES = 8
N_PAIRS = SSD_HEADS // 2
VMEM_LIMIT = 56 * 1024 * 1024

C_Z = 0
C_XBC = C_Z + SSD_WIDTH
C_GU = C_XBC + SSD_WIDTH + 2 * SSD_GROUPS * SSD_STATE
C_GV = C_GU + GMLP_WIDTH
C_DT = C_GV + GMLP_WIDTH
C_END = C_DT + LANES
PROJ_SLAB = 256


def _silu(v):
    return v * (1.0 / (1.0 + jnp.exp(-v)))


def _gelu_tanh(v):
    c = 0.7978845608028654
    hv = 0.5 * v
    return hv + hv * jnp.tanh(v * (c + (c * 0.044715) * (v * v)))


def _softplus(v):
    return jnp.maximum(v, 0.0) + jnp.log1p(jnp.exp(-jnp.abs(v)))


def _dot(a, b):
    return jnp.dot(a, b, preferred_element_type=F32)


def _dot_nt(a, b):
    return lax.dot_general(a, b, (((1,), (1,)), ((), ())), preferred_element_type=F32)


def _dot_tn(a, b):
    return lax.dot_general(a, b, (((0,), (0,)), ((), ())), preferred_element_type=F32)


def _replicate_heads(v, sel_ref):
    hi = v.astype(BF16)
    r1 = v - hi.astype(F32)
    mid = r1.astype(BF16)
    lo = (r1 - mid.astype(F32)).astype(BF16)
    sel = sel_ref[...]
    return _dot(hi, sel) + _dot(mid, sel) + _dot(lo, sel)


def _mixer_kernel(xb_ref, xf_ref, conv0a_ref, conv0b_ref, ssm0a_ref, ssm0b_ref,
                  g1_ref, win_ref, wdt_ref, convw_ref, convb_ref,
                  dtb_n_ref, alog_n_ref, sel_ref, dtb_col_ref, alog_col_ref, dskip_ref, ssmg_ref,
                  lng_ref, lnb_ref, wsp_ref, bsp_ref, wout_ref, g2_ref, wr_hi_ref, wr_lo_ref,
                  br_ref,
                  x1_ref, logit_ref, convo_ref, ssmo_ref, *rest,
                  tl, nt, l_valid, emit_gv):
    if emit_gv:
        gvo_ref = rest[0]
        rest = rest[1:]
    else:
        gvo_ref = None
    (pa_ref, pb_ref, da_ref, db_ref, xpad_ref, state_ref, wspm_ref, zs_ref, gus_ref, gvs_ref,
     gvf_ref, dtc_ref, acs_ref, dtt_ref) = rest
    s = pl.program_id(0)

    def drain_pieces(pbuf, dbuf, v):
        def p_xbc():
            xpad_ref[v, SUBLANES:SUBLANES + tl, :] = pbuf[:, C_XBC:C_GU]

        def p_dt():
            dtn = _softplus(pbuf[:, C_DT:C_END] + dtb_n_ref[...])
            row = lax.broadcasted_iota(jnp.int32, (tl, LANES), 0)
            if l_valid < tl:
                dtn = jnp.where(row < l_valid, dtn, 0.0)
            acsn = dtn * (-jnp.exp(alog_n_ref[...]))
            row_in_chunk = jnp.bitwise_and(row, SSD_CHUNK - 1)
            sh = 1
            while sh < SSD_CHUNK:
                acsn = acsn + jnp.where(row_in_chunk >= sh, pltpu.roll(acsn, sh, axis=0), 0.0)
                sh *= 2
            dtc_ref[v] = _replicate_heads(dtn, sel_ref)
            acs_ref[v] = _replicate_heads(acsn, sel_ref)
            dtt_ref[v] = _softplus(dbuf[...] + dtb_col_ref[...])

        def p_z():
            zs_ref[v] = _silu(pbuf[:, C_Z:C_XBC])

        def p_gu():
            gus_ref[v] = _gelu_tanh(pbuf[:, C_GU:C_GV])

        def p_gv():
            gv = _gelu_tanh(pbuf[:, C_GV:C_DT])
            mu = jnp.mean(gv, axis=-1, keepdims=True)
            gvc = gv - mu
            var = jnp.mean(gvc * gvc, axis=-1, keepdims=True)
            gv = gvc * lax.rsqrt(var + EPS) * lng_ref[...] + lnb_ref[...]
            if gvo_ref is not None:
                gvf_ref[v] = gv
            gvs_ref[v] = gv.astype(BF16)
        return [p_xbc, p_dt, p_z, p_gu, p_gv]

    def front_pieces(x, pbuf, dbuf):
        hb = []

        def norm():
            h = x * lax.rsqrt(jnp.mean(x * x, axis=-1, keepdims=True) + EPS) * g1_ref[...]
            hb.append(h.astype(BF16))
            dbuf[...] = _dot_nt(wdt_ref[...], hb[0])

        def slab(c0):
            c1 = min(c0 + PROJ_SLAB, C_END)

            def run():
                pbuf[:, c0:c1] = _dot(hb[0], win_ref[:, c0:c1])
            return run
        return [norm] + [slab(c0) for c0 in range(0, C_END, PROJ_SLAB)]

    def front(x, pbuf, dbuf):
        for piece in front_pieces(x, pbuf, dbuf):
            piece()

    @pl.when(s == 0)
    def _():
        r = lax.broadcasted_iota(jnp.int32, (GMLP_CHUNK, GMLP_CHUNK), 0)
        c = lax.broadcasted_iota(jnp.int32, (GMLP_CHUNK, GMLP_CHUNK), 1)
        for gi in range(GMLP_GROUPS):
            wspm_ref[gi] = jnp.where(r >= c, wsp_ref[gi], 0.0).astype(BF16)
        for v in range(2):
            xpad_ref[v, 0:SUBLANES, :] = jnp.zeros((SUBLANES, xpad_ref.shape[2]), F32)
        state_ref[...] = jnp.zeros_like(state_ref)
        front(xb_ref[0], pa_ref, da_ref)
        front(xb_ref[1], pb_ref, db_ref)
        for piece in drain_pieces(pa_ref, da_ref, 0):
            piece()

    bufs = ((pa_ref, da_ref, conv0a_ref, ssm0a_ref), (pb_ref, db_ref, conv0b_ref, ssm0b_ref))
    for u, (pbuf, dbuf, conv0_ref, ssm0_ref) in enumerate(bufs):
        is_first = lax.rem(2 * s + u, nt) == 0
        xpad_ref[u, 0:SUBLANES, :] = jnp.where(is_first, conv0_ref[0], xpad_ref[u, 0:SUBLANES, :])
        for j in range(N_PAIRS):
            state_ref[j] = jnp.where(is_first, ssm0_ref[0, j], state_ref[j])
        if gvo_ref is not None:
            gvo_ref[u] = gvf_ref[u]

        nxt = bufs[1 - u]
        _mixer_back(
            xb_ref[u], front_pieces(xf_ref[u], pbuf, dbuf), drain_pieces(nxt[0], nxt[1], 1 - u),
            convw_ref=convw_ref,
            convb_ref=convb_ref, alog_col_ref=alog_col_ref, dskip_ref=dskip_ref,
            ssmg_ref=ssmg_ref, bsp_ref=bsp_ref, wout_ref=wout_ref, g2_ref=g2_ref,
            wr_hi_ref=wr_hi_ref, wr_lo_ref=wr_lo_ref, br_ref=br_ref, x1_ref=x1_ref,
            logit_ref=logit_ref, convo_ref=convo_ref, ssmo_ref=ssmo_ref, xpad_ref=xpad_ref,
            state_ref=state_ref, wspm_ref=wspm_ref, zs_ref=zs_ref, gus_ref=gus_ref,
            gvs_ref=gvs_ref, dtc_ref=dtc_ref, acs_ref=acs_ref, dtt_ref=dtt_ref,
            u=u, tl=tl, l_valid=l_valid)


def _mixer_back(x, fillers, vfillers, *, convw_ref, convb_ref, alog_col_ref, dskip_ref,
                ssmg_ref, bsp_ref, wout_ref, g2_ref, wr_hi_ref, wr_lo_ref, br_ref, x1_ref, logit_ref,
                convo_ref, ssmo_ref, xpad_ref, state_ref, wspm_ref, zs_ref, gus_ref, gvs_ref,
                dtc_ref, acs_ref, dtt_ref, u, tl, l_valid):
    fillers = list(fillers)
    vfillers = list(vfillers)
    dt = dtc_ref[u]
    acs = acs_ref[u]
    dtt = dtt_ref[u]

    def fill(n):
        for _ in range(min(n, len(fillers))):
            fillers.pop(0)()

    def vfill(n):
        for _ in range(min(n, len(vfillers))):
            vfillers.pop(0)()

    fill(1)
    conv = convb_ref[...]
    for k in range(CONV_WIDTH):
        off = SUBLANES - (CONV_WIDTH - 1) + k
        conv = conv + xpad_ref[u, off:off + tl, :] * convw_ref[k:k + 1, :]
        fill(1)
    xbc = _silu(conv)
    carry = xpad_ref[u, l_valid:l_valid + SUBLANES, :]
    xpad_ref[1 - u, 0:SUBLANES, :] = carry
    convo_ref[u] = carry

    xs = xbc[:, 0:SSD_WIDTH]
    bm = xbc[:, SSD_WIDTH:SSD_WIDTH + SSD_GROUPS * SSD_STATE].astype(BF16)
    cm = xbc[:, SSD_WIDTH + SSD_GROUPS * SSD_STATE:].astype(BF16)

    lane_t = lax.broadcasted_iota(jnp.int32, (SSD_HEADS, tl), 1)
    if l_valid < tl:
        dtt = jnp.where(lane_t < l_valid, dtt, 0.0)
    acst = dtt * (-jnp.exp(alog_col_ref[...]))
    lane_in_chunk = jnp.bitwise_and(lane_t, SSD_CHUNK - 1)
    sh = 1
    while sh < SSD_CHUNK:
        acst = acst + jnp.where(lane_in_chunk >= sh, pltpu.roll(acst, sh, axis=1), 0.0)
        sh *= 2

    lane = lax.broadcasted_iota(jnp.int32, (SSD_CHUNK, LANES), 1)
    rowc = lax.broadcasted_iota(jnp.int32, (SSD_CHUNK, LANES), 0)
    lo_half = lane < SSD_HEAD_DIM
    causal = rowc >= jnp.bitwise_and(lane, SSD_CHUNK - 1)
    lane1 = lax.broadcasted_iota(jnp.int32, (1, LANES), 1)
    lo_half1 = lane1 < SSD_CHUNK

    y_chunks = []
    for c in range(tl // SSD_CHUNK):
        r0 = c * SSD_CHUNK
        v = acst[:, (c // 2) * LANES:(c // 2 + 1) * LANES]
        vr = pltpu.roll(v, SSD_CHUNK, axis=1)
        v_lo, v_hi = (v, vr) if c % 2 == 0 else (vr, v)
        cb2 = []
        for g in range(SSD_GROUPS):
            cg = cm[r0:r0 + SSD_CHUNK, g * SSD_STATE:(g + 1) * SSD_STATE]
            bg = bm[r0:r0 + SSD_CHUNK, g * SSD_STATE:(g + 1) * SSD_STATE]
            cb2.append(_dot_nt(cg, jnp.concatenate([bg, bg], axis=0)))
        y_pairs = []
        for j in range(N_PAIRS):
            g = j // (N_PAIRS // SSD_GROUPS)
            cg = cm[r0:r0 + SSD_CHUNK, g * SSD_STATE:(g + 1) * SSD_STATE]
            bg = bm[r0:r0 + SSD_CHUNK, g * SSD_STATE:(g + 1) * SSD_STATE]
            sl = slice(j * LANES, (j + 1) * LANES)
            col_a = acs[r0:r0 + SSD_CHUNK, sl]
            row_a = jnp.where(lo_half1, v_lo[2 * j:2 * j + 1, :], v_hi[2 * j + 1:2 * j + 2, :])
            decay = jnp.where(causal, jnp.exp(col_a - row_a), 0.0)
            m = (cb2[g] * decay).astype(BF16)
            xdt = xs[r0:r0 + SSD_CHUNK, sl] * dt[r0:r0 + SSD_CHUNK, sl]
            zbd = jnp.concatenate([jnp.where(lo_half, xdt, 0.0), jnp.where(lo_half, 0.0, xdt)],
                                  axis=0).astype(BF16)
            y_diag = _dot(m, zbd)
            st = state_ref[j]
            y_off = _dot(cg, st.astype(BF16)) * jnp.exp(col_a)
            a_last = acs[r0 + SSD_CHUNK - 1:r0 + SSD_CHUNK, sl]
            zdte = (xdt * jnp.exp(a_last - col_a)).astype(BF16)
            state_ref[j] = st * jnp.exp(a_last) + _dot_tn(bg, zdte)
            y_pairs.append(y_diag + y_off)
        y_chunks.append(jnp.concatenate(y_pairs, axis=1))
        fill(1)
    y = jnp.concatenate(y_chunks, axis=0) if len(y_chunks) > 1 else y_chunks[0]
    ssmo_ref[u] = state_ref[...]

    y = y + xs * dskip_ref[...]
    gated = y * zs_ref[u]
    half = SSD_WIDTH // SSD_GROUPS
    outs = []
    for g in range(SSD_GROUPS):
        gg = gated[:, g * half:(g + 1) * half]
        outs.append(gg * lax.rsqrt(jnp.mean(gg * gg, axis=-1, keepdims=True) + EPS))
    ssd_out = jnp.concatenate(outs, axis=1) * ssmg_ref[...]
    fill(1)

    lane_g = lax.broadcasted_iota(jnp.int32, (GMLP_CHUNK, LANES), 1)
    lo_g = lane_g < GMLP_GROUP_DIM
    mixed_chunks = []
    for q in range(tl // GMLP_CHUNK):
        mixed_pairs = []
        for j in range(GMLP_GROUPS // 2):
            vp = gvs_ref[u, q * GMLP_CHUNK:(q + 1) * GMLP_CHUNK, j * LANES:(j + 1) * LANES]
            r_even = _dot(wspm_ref[2 * j], vp)
            r_odd = _dot(wspm_ref[2 * j + 1], vp)
            mixed_pairs.append(jnp.where(lo_g, r_even, r_odd))
        mixed_chunks.append(jnp.concatenate(mixed_pairs, axis=1) + bsp_ref[...])
    mixed = jnp.concatenate(mixed_chunks, axis=0) if len(mixed_chunks) > 1 else mixed_chunks[0]
    gmlp_out = gus_ref[u] * mixed

    merged = jnp.concatenate([ssd_out, gmlp_out], axis=1).astype(BF16)
    x1 = x + _dot(merged, wout_ref[...])
    x1_ref[u] = x1
    fill(len(fillers))

    h2 = x1 * lax.rsqrt(jnp.mean(x1 * x1, axis=-1, keepdims=True) + EPS) * g2_ref[...]
    h2_hi = h2.astype(BF16)
    h2_lo = (h2 - h2_hi.astype(F32)).astype(BF16)
    logit_ref[:, u * tl:(u + 1) * tl] = (
        _dot_nt(wr_hi_ref[...], h2_hi) + _dot_nt(wr_lo_ref[...], h2_hi)
        + _dot_nt(wr_hi_ref[...], h2_lo) + br_ref[...])
    vfill(len(vfillers))


def _full_spec(shape):
    nd = len(shape)
    return pl.BlockSpec(shape, lambda s, _nd=nd: (0,) * _nd)


def _mixer(x, conv0, ssm0, wts, *, tl, l_valid, emit_gv):
    bsz, l, d = x.shape
    nt = l // tl
    g = bsz * nt
    assert l % tl == 0 and tl % GMLP_CHUNK == 0 and l_valid % SUBLANES == 0
    assert (nt == 1 or l_valid == tl) and g % 2 == 0
    steps = g // 2
    tile = lambda f: (lambda s: (f(s), 0, 0))
    seq = lambda f: (lambda s: (f(s) // nt,) + (0,) * 2)
    seq4 = lambda f: (lambda s: (f(s) // nt,) + (0,) * 3)
    in_specs = [
        pl.BlockSpec((2, tl, d), tile(lambda s: s)),
        pl.BlockSpec((2, tl, d), tile(lambda s: jnp.minimum(s + 1, steps - 1))),
        pl.BlockSpec((1, SUBLANES, d), seq(lambda s: 2 * s)),
        pl.BlockSpec((1, SUBLANES, d), seq(lambda s: 2 * s + 1)),
        pl.BlockSpec((1, N_PAIRS, LANES, LANES), seq4(lambda s: 2 * s)),
        pl.BlockSpec((1, N_PAIRS, LANES, LANES), seq4(lambda s: 2 * s + 1)),
    ] + [_full_spec(w.shape) for w in wts]
    out_shape = [
        jax.ShapeDtypeStruct((g, tl, d), F32),
        jax.ShapeDtypeStruct((ROUTER_ROWS, g * tl), F32),
        jax.ShapeDtypeStruct((g, SUBLANES, d), F32),
        jax.ShapeDtypeStruct((g, N_PAIRS, LANES, LANES), F32),
    ]
    out_specs = [
        pl.BlockSpec((2, tl, d), lambda s: (s, 0, 0)),
        pl.BlockSpec((ROUTER_ROWS, 2 * tl), lambda s: (0, s)),
        pl.BlockSpec((2, SUBLANES, d), lambda s: (s, 0, 0)),
        pl.BlockSpec((2, N_PAIRS, LANES, LANES), lambda s: (s, 0, 0, 0)),
    ]
    if emit_gv:
        out_shape.append(jax.ShapeDtypeStruct((g, tl, GMLP_WIDTH), F32))
        out_specs.append(pl.BlockSpec((2, tl, GMLP_WIDTH), lambda s: (s, 0, 0)))
    xt = x.reshape(g, tl, d)
    outs = pl.pallas_call(
        functools.partial(_mixer_kernel, tl=tl, nt=nt, l_valid=l_valid, emit_gv=emit_gv),
        grid=(steps,),
        in_specs=in_specs,
        out_specs=out_specs,
        out_shape=out_shape,
        scratch_shapes=[
            pltpu.VMEM((tl, C_END), F32),
            pltpu.VMEM((tl, C_END), F32),
            pltpu.VMEM((SSD_HEADS, tl), F32),
            pltpu.VMEM((SSD_HEADS, tl), F32),
            pltpu.VMEM((2, tl + SUBLANES, d), F32),
            pltpu.VMEM((N_PAIRS, LANES, LANES), F32),
            pltpu.VMEM((GMLP_GROUPS, GMLP_CHUNK, GMLP_CHUNK), BF16),
            pltpu.VMEM((2, tl, SSD_WIDTH), F32),
            pltpu.VMEM((2, tl, GMLP_WIDTH), F32),
            pltpu.VMEM((2, tl, GMLP_WIDTH), BF16),
            pltpu.VMEM((2, tl, GMLP_WIDTH), F32),
            pltpu.VMEM((2, tl, SSD_WIDTH), F32),
            pltpu.VMEM((2, tl, SSD_WIDTH), F32),
            pltpu.VMEM((2, SSD_HEADS, tl), F32),
        ],
        compiler_params=pltpu.CompilerParams(
            dimension_semantics=("arbitrary",), vmem_limit_bytes=VMEM_LIMIT),
        name="mixer",
    )(xt, xt, conv0, conv0, ssm0, ssm0, *wts)
    x1, lt, cvo, sso = outs[:4]
    last = slice(nt - 1, None, nt)
    res = (x1.reshape(bsz, l, d), lt, cvo[last], sso[last])
    if emit_gv:
        res += (outs[4].reshape(bsz, l, GMLP_WIDTH),)
    return res


def _route_kernel(logit_ref, tri_ref, dest_ref, w_ref, meta_ref, base_ref, keep_ref, *, tr, nt, blk):
    i = pl.program_id(0)
    sub8 = lax.broadcasted_iota(jnp.int32, (SUBLANES, tr), 0).astype(F32)
    sube = lax.broadcasted_iota(jnp.int32, (N_EXPERTS, tr), 0).astype(F32)

    @pl.when(i == 0)
    def _():
        base_ref[...] = jnp.zeros_like(base_ref)

    @pl.when(i < nt)
    def _():
        _route_pass0(logit_ref, tri_ref, base_ref, keep_ref, pl.multiple_of(i * tr, tr), sub8, sube, tr)

    @pl.when(i == nt)
    def _():
        counts = base_ref[...]
        padded = jnp.floor((counts + float(blk - 1)) * (1.0 / blk)) * float(blk)
        sub_e = lax.broadcasted_iota(jnp.int32, (N_EXPERTS, LANES), 0)
        pend = padded
        sh = 1
        while sh < N_EXPERTS:
            pend = pend + jnp.where(sub_e >= sh, pltpu.roll(pend, sh, axis=0), 0.0)
            sh *= 2
        pstart = pend - padded
        lane_e = lax.broadcasted_iota(jnp.int32, (N_EXPERTS, LANES), 1)
        meta = jnp.where(lane_e == 0, counts, jnp.where(lane_e == 1, pstart,
                         jnp.where(lane_e == 2, pend, 0.0)))
        meta_ref[...] = meta.astype(jnp.int32)

        def chunk(c, carry):
            sl = pl.ds(pl.multiple_of(c * tr, tr), tr)
            kept = keep_ref[:, sl]
            ps1 = jnp.sum(jnp.where(sube == kept[0:1], pstart[:, 0:1], 0.0), axis=0, keepdims=True)
            ps2 = jnp.sum(jnp.where(sube == kept[1:2], pstart[:, 0:1], 0.0), axis=0, keepdims=True)
            dest = jnp.where(sub8 == 0, ps1 + kept[2:3],
                             jnp.where(sub8 == 1, ps2 + kept[3:4], 0.0))
            dest_ref[:, sl] = dest.astype(jnp.int32)
            w_ref[:, sl] = jnp.where(sub8 == 0, kept[4:5], jnp.where(sub8 == 1, kept[5:6], 0.0))
            return carry
        lax.fori_loop(0, nt, chunk, 0)


def _route_pass0(logit_ref, tri_ref, base_ref, keep_ref, off, sub8, sube, tr):
    lg = logit_ref[...]
    big = float(SUBLANES)
    gl = jnp.where(sub8 < N_EXPERT_GROUPS, lg[0:SUBLANES], -jnp.inf)
    gmax = jnp.max(gl, axis=0, keepdims=True)
    g_sel = jnp.min(jnp.where(gl == gmax, sub8, big), axis=0, keepdims=True)
    p_group = 1.0 / jnp.sum(jnp.exp(gl - gmax), axis=0, keepdims=True)
    el = lg[SUBLANES:2 * SUBLANES]
    for g in range(1, N_EXPERT_GROUPS):
        el = jnp.where(g_sel == g, lg[(g + 1) * SUBLANES:(g + 2) * SUBLANES], el)
    top1 = jnp.max(el, axis=0, keepdims=True)
    i1 = jnp.min(jnp.where(el == top1, sub8, big), axis=0, keepdims=True)
    el2 = jnp.where(sub8 == i1, -jnp.inf, el)
    top2 = jnp.max(el2, axis=0, keepdims=True)
    i2 = jnp.min(jnp.where(el2 == top2, sub8, big), axis=0, keepdims=True)
    ex = jnp.exp(top2 - top1)
    w1 = p_group * (1.0 / (1.0 + ex))
    w2 = p_group * (ex / (1.0 + ex))
    e1 = g_sel * EXPERTS_PER_GROUP + i1
    e2 = g_sel * EXPERTS_PER_GROUP + i2

    oh1 = (sube == e1)
    oh2 = (sube == e2)
    tri = tri_ref[...]
    cum1 = _dot(jnp.where(oh1, 1.0, 0.0).astype(BF16), tri)
    cum2 = _dot(jnp.where(oh2, 1.0, 0.0).astype(BF16), tri)
    tot1 = jnp.sum(jnp.where(oh1, 1.0, 0.0), axis=1, keepdims=True)
    tot2 = jnp.sum(jnp.where(oh2, 1.0, 0.0), axis=1, keepdims=True)
    base = base_ref[:, 0:1]
    r1 = jnp.sum(jnp.where(oh1, cum1 + base, 0.0), axis=0, keepdims=True)
    r2 = jnp.sum(jnp.where(oh2, cum2 + base + tot1, 0.0), axis=0, keepdims=True)
    base_ref[...] = jnp.broadcast_to(base + tot1 + tot2, base_ref.shape)
    keep_ref[:, pl.ds(off, tr)] = jnp.where(
        sub8 == 0, e1, jnp.where(sub8 == 1, e2, jnp.where(sub8 == 2, r1, jnp.where(
            sub8 == 3, r2, jnp.where(sub8 == 4, w1, jnp.where(sub8 == 5, w2, 0.0))))))


def _route(logits_t, tr, blk):
    rows, t = logits_t.shape
    assert t % tr == 0
    nt = t // tr
    tri = jnp.triu(jnp.ones((tr, tr), BF16), k=1)
    return pl.pallas_call(
        functools.partial(_route_kernel, tr=tr, nt=nt, blk=blk),
        grid=(nt + 1,),
        in_specs=[pl.BlockSpec((rows, tr), lambda i, _n=nt: (0, jnp.minimum(i, _n - 1))),
                  pl.BlockSpec((tr, tr), lambda i: (0, 0))],
        out_specs=[pl.BlockSpec((SUBLANES, t), lambda i: (0, 0)),
                   pl.BlockSpec((SUBLANES, t), lambda i: (0, 0)),
                   pl.BlockSpec((N_EXPERTS, LANES), lambda i: (0, 0))],
        out_shape=[jax.ShapeDtypeStruct((SUBLANES, t), jnp.int32),
                   jax.ShapeDtypeStruct((SUBLANES, t), F32),
                   jax.ShapeDtypeStruct((N_EXPERTS, LANES), jnp.int32)],
        scratch_shapes=[pltpu.VMEM((N_EXPERTS, LANES), F32), pltpu.VMEM((SUBLANES, t), F32)],
        compiler_params=pltpu.CompilerParams(
            dimension_semantics=("arbitrary",), vmem_limit_bytes=VMEM_LIMIT),
        name="route",
    )(logits_t, tri)


def _dispatch_kernel(meta_ref, dest_ref, x1_hbm, xs_hbm, xin, zbuf, lsem, sem, zsem, *,
                     tmd, nsteps, blk, nb):
    i = pl.program_id(0)

    def load_tile(j, slot):
        return pltpu.make_async_copy(x1_hbm.at[pl.ds(pl.multiple_of(j * tmd, tmd), tmd)],
                                     xin.at[slot], lsem.at[slot])

    def wait_step(slot):
        for _ in range(2):
            pltpu.make_async_copy(xin.at[slot], xin.at[slot], sem.at[slot]).wait()

    def pad_rows(e, row_fn, oct_fn):
        start = meta_ref[N_EXPERTS + e] + meta_ref[e]
        start8 = lax.shift_right_logical(start + (SUBLANES - 1), 3)
        lax.fori_loop(start, start8 * SUBLANES, row_fn, 0)
        lax.fori_loop(start8, lax.shift_right_logical(meta_ref[2 * N_EXPERTS + e], 3), oct_fn, 0)

    def zero_row(row):
        return pltpu.make_async_copy(zbuf.at[pl.ds(0, 1)], xs_hbm.at[pl.ds(row, 1)], zsem)

    def zero_oct(o):
        return pltpu.make_async_copy(
            zbuf.at[pl.ds(0, SUBLANES)],
            xs_hbm.at[pl.ds(pl.multiple_of(o * SUBLANES, SUBLANES), SUBLANES)], zsem)

    def zero_block(b):
        return pltpu.make_async_copy(zbuf, xs_hbm.at[pl.ds(pl.multiple_of(b * blk, blk), blk)], zsem)

    @pl.when(i == 0)
    def _():
        zbuf[...] = jnp.zeros_like(zbuf)
        n_used = lax.div(meta_ref[3 * N_EXPERTS - 1], blk)

        def start_row(row, c):
            zero_row(row).start()
            return c

        def wait_row(row, c):
            zero_row(row).wait()
            return c

        def start_oct(o, c):
            zero_oct(o).start()
            return c

        def wait_oct(o, c):
            zero_oct(o).wait()
            return c

        def start_blk(b, c):
            zero_block(b).start()
            return c

        def wait_blk(b, c):
            zero_block(b).wait()
            return c

        for e in range(N_EXPERTS):
            pad_rows(e, start_row, start_oct)
        lax.fori_loop(n_used, nb, start_blk, 0)
        for e in range(N_EXPERTS):
            pad_rows(e, wait_row, wait_oct)
        lax.fori_loop(n_used, nb, wait_blk, 0)

        load_tile(0, 0).start()
        if nsteps > 1:
            load_tile(1, 1).start()

    slot = lax.rem(i, DISPATCH_RING)
    load_tile(i, slot).wait()
    for r in range(tmd):
        for k in range(2):
            pltpu.make_async_copy(xin.at[slot, pl.ds(r, 1)],
                                  xs_hbm.at[pl.ds(dest_ref[0, 0, k * tmd + r], 1)],
                                  sem.at[slot]).start(priority=k)

    @pl.when(i > 0)
    def _():
        wait_step(lax.rem(i + DISPATCH_RING - 1, DISPATCH_RING))

    @pl.when(i + 2 < nsteps)
    def _():
        load_tile(i + 2, lax.rem(i + 2, DISPATCH_RING)).start()

    @pl.when(i == nsteps - 1)
    def _():
        wait_step(slot)


def _dispatch(x1_flat, dest, meta_s, n_slots, tmd, blk):
    t, d = x1_flat.shape
    nsteps = t // tmd
    dest3 = dest.reshape(2, nsteps, tmd).transpose(1, 0, 2).reshape(nsteps, 1, 2 * tmd)
    grid_spec = pltpu.PrefetchScalarGridSpec(
        num_scalar_prefetch=1,
        grid=(nsteps,),
        in_specs=[pl.BlockSpec((1, 1, 2 * tmd), lambda i, m: (i, 0, 0), memory_space=pltpu.SMEM),
                  pl.BlockSpec(memory_space=pl.ANY)],
        out_specs=pl.BlockSpec(memory_space=pl.ANY),
        scratch_shapes=[pltpu.VMEM((DISPATCH_RING, tmd, d), F32), pltpu.VMEM((blk, d), F32),
                        pltpu.SemaphoreType.DMA((DISPATCH_RING,)),
                        pltpu.SemaphoreType.DMA((DISPATCH_RING,)), pltpu.SemaphoreType.DMA(())],
    )
    return pl.pallas_call(
        functools.partial(_dispatch_kernel, tmd=tmd, nsteps=nsteps, blk=blk, nb=n_slots // blk),
        grid_spec=grid_spec,
        out_shape=jax.ShapeDtypeStruct((n_slots, d), F32),
        compiler_params=pltpu.CompilerParams(dimension_semantics=("arbitrary",)),
        name="dispatch",
    )(meta_s, dest3, x1_flat)


def _expert_kernel(be_ref, nused_ref, xs_hbm, g2_ref, wg_ref, wu_ref, wd_ref, ys_ref, wgb, wub, wdb,
                   xring, lsem):
    i = pl.program_id(0)
    nused = nused_ref[0]
    changed = jnp.logical_or(i == 0, be_ref[i] != be_ref[jnp.maximum(i - 1, 0)])
    blk = ys_ref.shape[0]
    slot = lax.rem(i, EXPERT_RING)

    def load_block(j):
        s = lax.rem(j, EXPERT_RING)
        return pltpu.make_async_copy(xs_hbm.at[pl.ds(pl.multiple_of(j * blk, blk), blk)],
                                     xring.at[s], lsem.at[s])

    @pl.when(i == 0)
    def _():
        for j in range(EXPERT_RING - 1):
            @pl.when(j < nused)
            def _():
                load_block(j).start()

    @pl.when(i + EXPERT_RING - 1 < nused)
    def _():
        load_block(i + EXPERT_RING - 1).start()

    @pl.when(jnp.logical_and(i < nused, changed))
    def _():
        wgb[...] = wg_ref[0].astype(BF16)
        wub[...] = wu_ref[0].astype(BF16)
        wdb[...] = wd_ref[0].astype(BF16)

    @pl.when(i < nused)
    def _():
        load_block(i).wait()
        cuts = (0, blk // 2, blk) if blk >= 2 * LANES else (0, blk)

        def norm(a, b):
            xr = xring[slot, a:b, :]
            return (xr * lax.rsqrt(jnp.mean(xr * xr, axis=-1, keepdims=True) + EPS)
                    * g2_ref[...]).astype(BF16)

        def act(gu):
            return (_silu(gu[0]) * gu[1]).astype(BF16)

        hbs = [norm(a, b) for a, b in zip(cuts[:-1], cuts[1:])]
        gus = [(_dot(hb, wgb[...]), _dot(hb, wub[...])) for hb in hbs]
        for (a, b), gu in zip(zip(cuts[:-1], cuts[1:]), gus):
            ys_ref[a:b, :] = _dot(act(gu), wdb[...])

    @pl.when(i >= nused)
    def _():
        ys_ref[...] = jnp.zeros_like(ys_ref)


def _experts(xs, blk_expert, n_used, g2, w_gate, w_up, w_down, blk):
    n_slots, d = xs.shape
    nb = n_slots // blk
    de = w_gate.shape[-1]
    grid_spec = pltpu.PrefetchScalarGridSpec(
        num_scalar_prefetch=2,
        grid=(nb,),
        in_specs=[
            pl.BlockSpec(memory_space=pl.ANY),
            pl.BlockSpec((1, d), lambda i, be, nu: (0, 0)),
            pl.BlockSpec((1, d, de), lambda i, be, nu: (be[i], 0, 0)),
            pl.BlockSpec((1, d, de), lambda i, be, nu: (be[i], 0, 0)),
            pl.BlockSpec((1, de, d), lambda i, be, nu: (be[i], 0, 0)),
        ],
        out_specs=pl.BlockSpec((blk, d), lambda i, be, nu: (i, 0)),
        scratch_shapes=[
            pltpu.VMEM((d, de), BF16),
            pltpu.VMEM((d, de), BF16),
            pltpu.VMEM((de, d), BF16),
            pltpu.VMEM((EXPERT_RING, blk, d), F32),
            pltpu.SemaphoreType.DMA((EXPERT_RING,)),
        ],
    )
    return pl.pallas_call(
        _expert_kernel,
        grid_spec=grid_spec,
        out_shape=jax.ShapeDtypeStruct((n_slots, d), F32),
        compiler_params=pltpu.CompilerParams(
            dimension_semantics=("arbitrary",), vmem_limit_bytes=VMEM_LIMIT),
        name="experts",
    )(blk_expert, n_used, xs, g2, w_gate, w_up, w_down)


def _gather_rows(idx_ref, n_rows, src_hbm, dst_ref, sem):
    for r in range(n_rows):
        pltpu.make_async_copy(src_hbm.at[pl.ds(idx_ref[0, 0, r], 1)],
                              dst_ref.at[pl.ds(r, 1)], sem).start(priority=r % 2)


def _combine_kernel(posc_ref, posn_ref, x1_ref, w_ref, gf_ref, ys_hbm, y_ref, ybuf, sem, *, tm, nsteps):
    i = pl.program_id(0)
    slot = i % 2

    @pl.when(i == 0)
    def _():
        _gather_rows(posc_ref, 2 * tm, ys_hbm, ybuf.at[0], sem.at[0])

    @pl.when(i + 1 < nsteps)
    def _():
        _gather_rows(posn_ref, 2 * tm, ys_hbm, ybuf.at[1 - slot], sem.at[1 - slot])

    pltpu.make_async_copy(ybuf.at[1 - slot], ybuf.at[slot], sem.at[slot]).wait()
    w = w_ref[...]
    xo = (x1_ref[...] + w[:, 0:1] * ybuf[slot, 0:tm, :] + w[:, 1:2] * ybuf[slot, tm:2 * tm, :])
    y_ref[...] = xo * lax.rsqrt(jnp.mean(xo * xo, axis=-1, keepdims=True) + EPS) * gf_ref[...]


def _combine(x1_flat, pos, w_col, ys, gf, tm):
    t, d = x1_flat.shape
    nsteps = t // tm
    pos3 = pos.reshape(nsteps, 1, 2 * tm)
    return pl.pallas_call(
        functools.partial(_combine_kernel, tm=tm, nsteps=nsteps),
        grid=(nsteps,),
        in_specs=[
            pl.BlockSpec((1, 1, 2 * tm), lambda i: (i, 0, 0), memory_space=pltpu.SMEM),
            pl.BlockSpec((1, 1, 2 * tm), lambda i, _n=nsteps: (jnp.minimum(i + 1, _n - 1), 0, 0),
                         memory_space=pltpu.SMEM),
            pl.BlockSpec((tm, d), lambda i: (i, 0)),
            pl.BlockSpec((tm, 2), lambda i: (i, 0)),
            pl.BlockSpec((1, d), lambda i: (0, 0)),
            pl.BlockSpec(memory_space=pl.ANY),
        ],
        out_specs=pl.BlockSpec((tm, d), lambda i: (i, 0)),
        out_shape=jax.ShapeDtypeStruct((t, d), F32),
        scratch_shapes=[pltpu.VMEM((2, 2 * tm, d), F32), pltpu.SemaphoreType.DMA((2,))],
        compiler_params=pltpu.CompilerParams(
            dimension_semantics=("arbitrary",), vmem_limit_bytes=VMEM_LIMIT),
        name="combine",
    )(pos3, pos3, x1_flat, w_col, gf, ys)


def _moe_and_norm(x1_flat, logits_t, g2, w_gate, w_up, w_down, gf):
    t, d = x1_flat.shape
    tr = _pick(t, (512, 256, 128))
    tm = _pick(t, (512, 256, 128))
    tmd = _pick(t, (512, 256, 128))
    blk = MOE_BLOCK if 2 * t >= 4 * N_EXPERTS * MOE_BLOCK else MOE_BLOCK_SMALL
    dest8, w8, meta = _route(logits_t, tr, blk)
    dest = dest8[0:2]
    nb = (2 * t + blk - 1) // blk + N_EXPERTS
    pend = meta[:, 2]
    blk_expert = jnp.minimum(
        jnp.sum(jnp.arange(nb, dtype=jnp.int32)[:, None] * blk >= pend[None, :], axis=1),
        N_EXPERTS - 1).astype(jnp.int32)
    n_used = (pend[-1:] // blk).astype(jnp.int32)
    xs = _dispatch(x1_flat, dest, meta[:, 0:3].T.reshape(3 * N_EXPERTS), nb * blk, tmd, blk)
    ys = _experts(xs, blk_expert, n_used, g2, w_gate, w_up, w_down, blk)
    nsteps = t // tm
    pos = dest.reshape(2, nsteps, tm).transpose(1, 0, 2).reshape(nsteps * 2 * tm)
    return _combine(x1_flat, pos, w8[0:2].T, ys, gf, tm)


def _prep_layer_weights(norm1_g, w_in, conv_w, conv_b, dt_bias, a_log, d_skip, ssm_norm_g, ln_v_g,
                        ln_v_b, w_spatial, b_spatial, w_out, norm2_g, w_rg, b_rg, w_re, b_re):
    d = w_in.shape[0]
    conv_dim = SSD_WIDTH + 2 * SSD_GROUPS * SSD_STATE
    o_xbc = SSD_WIDTH
    o_dt = o_xbc + conv_dim
    o_gu = o_dt + SSD_HEADS
    o_gv = o_gu + GMLP_WIDTH
    w_dt = w_in[:, o_dt:o_gu]
    w_in_r = jnp.concatenate(
        [w_in[:, :o_xbc], w_in[:, o_xbc:o_dt], w_in[:, o_gu:o_gv], w_in[:, o_gv:],
         jnp.pad(w_dt, ((0, 0), (0, LANES - SSD_HEADS)))], axis=1).astype(BF16)
    rep = lambda v: jnp.repeat(v, SSD_HEAD_DIM)[None, :]
    col = lambda v: v[:, None]
    narrow = lambda v: jnp.pad(v, (0, LANES - SSD_HEADS))[None, :]
    sel = (jnp.arange(LANES)[:, None] == jnp.arange(SSD_WIDTH)[None, :] // SSD_HEAD_DIM).astype(BF16)
    w_r = jnp.concatenate([w_rg, jnp.zeros((d, SUBLANES - N_EXPERT_GROUPS), F32), w_re], axis=1).T
    b_r = jnp.concatenate([b_rg, jnp.zeros((SUBLANES - N_EXPERT_GROUPS,), F32), b_re])[:, None]
    w_r_hi = w_r.astype(BF16)
    w_r_lo = (w_r - w_r_hi.astype(F32)).astype(BF16)
    bsp = jnp.repeat(b_spatial.T, GMLP_GROUP_DIM, axis=1)
    return (norm1_g[None, :], w_in_r, w_dt.T.astype(BF16), conv_w, conv_b[None, :],
            narrow(dt_bias), narrow(a_log), sel, col(dt_bias), col(a_log), rep(d_skip),
            ssm_norm_g[None, :],
            ln_v_g[None, :], ln_v_b[None, :], w_spatial, bsp, w_out.astype(BF16), norm2_g[None, :],
            w_r_hi, w_r_lo, b_r)


def _state_to_pairs(s):
    b = s.shape[0]
    return s.reshape(b, N_PAIRS, 2, SSD_HEAD_DIM, SSD_STATE).transpose(0, 1, 4, 2, 3).reshape(
        b, N_PAIRS, SSD_STATE, 2 * SSD_HEAD_DIM)


def _pairs_to_state(s):
    b = s.shape[0]
    return s.reshape(b, N_PAIRS, SSD_STATE, 2, SSD_HEAD_DIM).transpose(0, 1, 3, 4, 2).reshape(
        b, SSD_HEADS, SSD_HEAD_DIM, SSD_STATE)


def _pick(n, prefs):
    for p in prefs:
        if n % p == 0:
            return p
    return n


def kernel(x_prompt, x_sample, cache_conv, state_ssm, norm1_g, w_in, conv_w, conv_b, dt_bias, a_log, d_skip, ssm_norm_g, ln_v_g, ln_v_b, w_spatial, b_spatial, w_out, norm2_g, w_router_group, b_router_group, w_router_expert, b_router_expert, w_gate, w_up, w_down, final_norm_g):
    depth = w_in.shape[0]
    assert depth == 1, "the combine kernel fuses the final norm, so only a single layer is supported"
    bp, lp, d = x_prompt.shape
    bs, ls, _ = x_sample.shape
    conv_dim = cache_conv.shape[-1]
    ls_pad = -(-ls // GMLP_CHUNK) * GMLP_CHUNK
    tl_p = _pick(lp, (256, 128))
    gf = final_norm_g[None, :]

    yp = x_prompt
    ys = jnp.pad(x_sample, ((0, 0), (0, ls_pad - ls), (0, 0)))
    conv_p, ssm_p, conv_s, ssm_s, v_s = [], [], [], [], []
    for i in range(depth):
        wts = _prep_layer_weights(
            norm1_g[i], w_in[i], conv_w[i], conv_b[i], dt_bias[i], a_log[i], d_skip[i], ssm_norm_g[i],
            ln_v_g[i], ln_v_b[i], w_spatial[i], b_spatial[i], w_out[i], norm2_g[i],
            w_router_group[i], b_router_group[i], w_router_expert[i], b_router_expert[i])
        g2 = norm2_g[i][None, :]
        gfi = gf

        conv0 = jnp.zeros((bp, SUBLANES, conv_dim), F32)
        ssm0 = jnp.zeros((bp, N_PAIRS, LANES, LANES), F32)
        x1, lt, cpo, hpo = _mixer(yp, conv0, ssm0, wts, tl=tl_p, l_valid=tl_p, emit_gv=False)
        tp = bp * lp
        outp = _moe_and_norm(x1.reshape(tp, d), lt, g2, w_gate[i], w_up[i], w_down[i], gfi)
        yp = outp.reshape(bp, lp, d)
        conv_p.append(cpo[:, SUBLANES - (CONV_WIDTH - 1):, :])
        ssm_p.append(_pairs_to_state(hpo))

        conv0 = jnp.pad(cache_conv[i], ((0, 0), (SUBLANES - (CONV_WIDTH - 1), 0), (0, 0)))
        x1, lt, cso, hso, gv = _mixer(ys, conv0, _state_to_pairs(state_ssm[i]), wts,
                                      tl=ls_pad, l_valid=ls, emit_gv=True)
        tsn = bs * ls
        x1v = x1[:, :ls].reshape(tsn, d)
        ltv = lt.reshape(ROUTER_ROWS, bs, ls_pad)[:, :, :ls].reshape(ROUTER_ROWS, tsn)
        outs = _moe_and_norm(x1v, ltv, g2, w_gate[i], w_up[i], w_down[i], gfi)
        ys = outs.reshape(bs, ls, d)
        conv_s.append(cso[:, SUBLANES - (CONV_WIDTH - 1):, :])
        ssm_s.append(_pairs_to_state(hso))
        v_s.append(gv[:, :ls])
    return (yp, ys, jnp.stack(conv_p), jnp.stack(ssm_p), jnp.stack(conv_s), jnp.stack(ssm_s),
            jnp.stack(v_s))
```

```python
import functools

import jax
import jax.numpy as jnp
from jax import lax
from jax.experimental import pallas as pl
from jax.experimental.pallas import tpu as pltpu

F32 = jnp.float32
BF16 = jnp.bfloat16
EPS = 1e-6

SSD_HEADS = 8
SSD_HEAD_DIM = 64
SSD_STATE = 128
SSD_GROUPS = 2
SSD_WIDTH = SSD_HEADS * SSD_HEAD_DIM
SSD_CHUNK = 64
CONV_WIDTH = 4
GMLP_GROUPS = 8
GMLP_GROUP_DIM = 64
GMLP_WIDTH = GMLP_GROUPS * GMLP_GROUP_DIM
GMLP_CHUNK = 128
N_EXPERT_GROUPS = 4
EXPERTS_PER_GROUP = 8
N_EXPERTS = N_EXPERT_GROUPS * EXPERTS_PER_GROUP
ROUTER_ROWS = 8 + N_EXPERTS
MOE_BLOCK = 512
MOE_BLOCK_SMALL = 32
DISPATCH_RING = 3
EXPERT_RING = 3

LANES = 128
SUBLANES = 8
N_PAIRS = SSD_HEADS // 2
VMEM_LIMIT = 56 * 1024 * 1024

C_Z = 0
C_XBC = C_Z + SSD_WIDTH
C_GU = C_XBC + SSD_WIDTH + 2 * SSD_GROUPS * SSD_STATE
C_GV = C_GU + GMLP_WIDTH
C_DT = C_GV + GMLP_WIDTH
C_END = C_DT + LANES
PROJ_SLAB = 256


def _silu(v):
    return v * (1.0 / (1.0 + jnp.exp(-v)))


def _gelu_tanh(v):
    c = 0.7978845608028654
    hv = 0.5 * v
    return hv + hv * jnp.tanh(v * (c + (c * 0.044715) * (v * v)))


def _softplus(v):
    return jnp.maximum(v, 0.0) + jnp.log1p(jnp.exp(-jnp.abs(v)))


def _dot(a, b):
    return jnp.dot(a, b, preferred_element_type=F32)


def _dot_nt(a, b):
    return lax.dot_general(a, b, (((1,), (1,)), ((), ())), preferred_element_type=F32)


def _dot_tn(a, b):
    return lax.dot_general(a, b, (((0,), (0,)), ((), ())), preferred_element_type=F32)


def _replicate_heads(v, sel_ref):
    hi = v.astype(BF16)
    r1 = v - hi.astype(F32)
    mid = r1.astype(BF16)
    lo = (r1 - mid.astype(F32)).astype(BF16)
    sel = sel_ref[...]
    return _dot(hi, sel) + _dot(mid, sel) + _dot(lo, sel)


def _mixer_kernel(xb_ref, xf_ref, conv0a_ref, conv0b_ref, ssm0a_ref, ssm0b_ref,
                  g1_ref, win_ref, wdt_ref, convw_ref, convb_ref,
                  dtb_n_ref, alog_n_ref, sel_ref, dtb_col_ref, alog_col_ref, dskip_ref, ssmg_ref,
                  lng_ref, lnb_ref, wsp_ref, bsp_ref, wout_ref, g2_ref, wr_hi_ref, wr_lo_ref,
                  br_ref,
                  x1_ref, h2p_ref, logit_ref, convo_ref, ssmo_ref, *rest,
                  tl, nt, l_valid, emit_gv):
    if emit_gv:
        gvo_ref = rest[0]
        rest = rest[1:]
    else:
        gvo_ref = None
    (pa_ref, pb_ref, da_ref, db_ref, xpad_ref, state_ref, wspm_ref, zs_ref, gus_ref, gvs_ref,
     gvf_ref, dtc_ref, acs_ref, dtt_ref) = rest
    s = pl.program_id(0)

    def drain_pieces(pbuf, dbuf, v):
        def p_xbc():
            xpad_ref[v, SUBLANES:SUBLANES + tl, :] = pbuf[:, C_XBC:C_GU]

        def p_dt():
            dtn = _softplus(pbuf[:, C_DT:C_END] + dtb_n_ref[...])
            row = lax.broadcasted_iota(jnp.int32, (tl, LANES), 0)
            if l_valid < tl:
                dtn = jnp.where(row < l_valid, dtn, 0.0)
            acsn = dtn * (-jnp.exp(alog_n_ref[...]))
            row_in_chunk = jnp.bitwise_and(row, SSD_CHUNK - 1)
            sh = 1
            while sh < SSD_CHUNK:
                acsn = acsn + jnp.where(row_in_chunk >= sh, pltpu.roll(acsn, sh, axis=0), 0.0)
                sh *= 2
            dtc_ref[v] = _replicate_heads(dtn, sel_ref)
            acs_ref[v] = _replicate_heads(acsn, sel_ref)
            dtt_ref[v] = _softplus(dbuf[...] + dtb_col_ref[...])

        def p_z():
            zs_ref[v] = _silu(pbuf[:, C_Z:C_XBC])

        def p_gu():
            gus_ref[v] = _gelu_tanh(pbuf[:, C_GU:C_GV])

        def p_gv():
            gv = _gelu_tanh(pbuf[:, C_GV:C_DT])
            mu = jnp.mean(gv, axis=-1, keepdims=True)
            gvc = gv - mu
            var = jnp.mean(gvc * gvc, axis=-1, keepdims=True)
            gv = gvc * lax.rsqrt(var + EPS) * lng_ref[...] + lnb_ref[...]
            if gvo_ref is not None:
                gvf_ref[v] = gv
            gvs_ref[v] = gv.astype(BF16)
        return [p_xbc, p_dt, p_z, p_gu, p_gv]

    def front_pieces(x, pbuf, dbuf):
        hb = []

        def norm():
            h = x * lax.rsqrt(jnp.mean(x * x, axis=-1, keepdims=True) + EPS) * g1_ref[...]
            hb.append(h.astype(BF16))
            dbuf[...] = _dot_nt(wdt_ref[...], hb[0])

        def slab(c0):
            c1 = min(c0 + PROJ_SLAB, C_END)

            def run():
                pbuf[:, c0:c1] = _dot(hb[0], win_ref[:, c0:c1])
            return run
        return [norm] + [slab(c0) for c0 in range(0, C_END, PROJ_SLAB)]

    def front(x, pbuf, dbuf):
        for piece in front_pieces(x, pbuf, dbuf):
            piece()

    @pl.when(s == 0)
    def _():
        r = lax.broadcasted_iota(jnp.int32, (GMLP_CHUNK, GMLP_CHUNK), 0)
        c = lax.broadcasted_iota(jnp.int32, (GMLP_CHUNK, GMLP_CHUNK), 1)
        for gi in range(GMLP_GROUPS):
            wspm_ref[gi] = jnp.where(r >= c, wsp_ref[gi], 0.0).astype(BF16)
        for v in range(2):
            xpad_ref[v, 0:SUBLANES, :] = jnp.zeros((SUBLANES, xpad_ref.shape[2]), F32)
        state_ref[...] = jnp.zeros_like(state_ref)
        front(xb_ref[0], pa_ref, da_ref)
        front(xb_ref[1], pb_ref, db_ref)
        for piece in drain_pieces(pa_ref, da_ref, 0):
            piece()

    bufs = ((pa_ref, da_ref, conv0a_ref, ssm0a_ref), (pb_ref, db_ref, conv0b_ref, ssm0b_ref))
    for u, (pbuf, dbuf, conv0_ref, ssm0_ref) in enumerate(bufs):
        is_first = lax.rem(2 * s + u, nt) == 0
        xpad_ref[u, 0:SUBLANES, :] = jnp.where(is_first, conv0_ref[0], xpad_ref[u, 0:SUBLANES, :])
        for j in range(N_PAIRS):
            state_ref[j] = jnp.where(is_first, ssm0_ref[0, j], state_ref[j])
        if gvo_ref is not None:
            gvo_ref[u] = gvf_ref[u]

        nxt = bufs[1 - u]
        _mixer_back(
            xb_ref[u], front_pieces(xf_ref[u], pbuf, dbuf), drain_pieces(nxt[0], nxt[1], 1 - u),
            convw_ref=convw_ref,
            convb_ref=convb_ref, alog_col_ref=alog_col_ref, dskip_ref=dskip_ref,
            ssmg_ref=ssmg_ref, bsp_ref=bsp_ref, wout_ref=wout_ref, g2_ref=g2_ref,
            wr_hi_ref=wr_hi_ref, wr_lo_ref=wr_lo_ref, br_ref=br_ref, x1_ref=x1_ref,
            h2p_ref=h2p_ref, logit_ref=logit_ref, convo_ref=convo_ref, ssmo_ref=ssmo_ref, xpad_ref=xpad_ref,
            state_ref=state_ref, wspm_ref=wspm_ref, zs_ref=zs_ref, gus_ref=gus_ref,
            gvs_ref=gvs_ref, dtc_ref=dtc_ref, acs_ref=acs_ref, dtt_ref=dtt_ref,
            u=u, tl=tl, l_valid=l_valid)


def _mixer_back(x, fillers, vfillers, *, convw_ref, convb_ref, alog_col_ref, dskip_ref,
                ssmg_ref, bsp_ref, wout_ref, g2_ref, wr_hi_ref, wr_lo_ref, br_ref, x1_ref, h2p_ref,
                logit_ref, convo_ref, ssmo_ref, xpad_ref, state_ref, wspm_ref, zs_ref, gus_ref,
                gvs_ref, dtc_ref, acs_ref, dtt_ref, u, tl, l_valid):
    fillers = list(fillers)
    vfillers = list(vfillers)
    dt = dtc_ref[u]
    acs = acs_ref[u]
    dtt = dtt_ref[u]

    def fill(n):
        for _ in range(min(n, len(fillers))):
            fillers.pop(0)()

    def vfill(n):
        for _ in range(min(n, len(vfillers))):
            vfillers.pop(0)()

    fill(1)
    conv = convb_ref[...]
    for k in range(CONV_WIDTH):
        off = SUBLANES - (CONV_WIDTH - 1) + k
        conv = conv + xpad_ref[u, off:off + tl, :] * convw_ref[k:k + 1, :]
        fill(1)
    xbc = _silu(conv)
    carry = xpad_ref[u, l_valid:l_valid + SUBLANES, :]
    xpad_ref[1 - u, 0:SUBLANES, :] = carry
    convo_ref[u] = carry

    xs = xbc[:, 0:SSD_WIDTH]
    bm = xbc[:, SSD_WIDTH:SSD_WIDTH + SSD_GROUPS * SSD_STATE].astype(BF16)
    cm = xbc[:, SSD_WIDTH + SSD_GROUPS * SSD_STATE:].astype(BF16)

    lane_t = lax.broadcasted_iota(jnp.int32, (SSD_HEADS, tl), 1)
    if l_valid < tl:
        dtt = jnp.where(lane_t < l_valid, dtt, 0.0)
    acst = dtt * (-jnp.exp(alog_col_ref[...]))
    lane_in_chunk = jnp.bitwise_and(lane_t, SSD_CHUNK - 1)
    sh = 1
    while sh < SSD_CHUNK:
        acst = acst + jnp.where(lane_in_chunk >= sh, pltpu.roll(acst, sh, axis=1), 0.0)
        sh *= 2

    lane = lax.broadcasted_iota(jnp.int32, (SSD_CHUNK, LANES), 1)
    rowc = lax.broadcasted_iota(jnp.int32, (SSD_CHUNK, LANES), 0)
    lo_half = lane < SSD_HEAD_DIM
    causal = rowc >= jnp.bitwise_and(lane, SSD_CHUNK - 1)
    lane1 = lax.broadcasted_iota(jnp.int32, (1, LANES), 1)
    lo_half1 = lane1 < SSD_CHUNK

    y_chunks = []
    for c in range(tl // SSD_CHUNK):
        r0 = c * SSD_CHUNK
        v = acst[:, (c // 2) * LANES:(c // 2 + 1) * LANES]
        vr = pltpu.roll(v, SSD_CHUNK, axis=1)
        v_lo, v_hi = (v, vr) if c % 2 == 0 else (vr, v)
        cb2 = []
        for g in range(SSD_GROUPS):
            cg = cm[r0:r0 + SSD_CHUNK, g * SSD_STATE:(g + 1) * SSD_STATE]
            bg = bm[r0:r0 + SSD_CHUNK, g * SSD_STATE:(g + 1) * SSD_STATE]
            cb2.append(_dot_nt(cg, jnp.concatenate([bg, bg], axis=0)))
        y_pairs = []
        for j in range(N_PAIRS):
            g = j // (N_PAIRS // SSD_GROUPS)
            cg = cm[r0:r0 + SSD_CHUNK, g * SSD_STATE:(g + 1) * SSD_STATE]
            bg = bm[r0:r0 + SSD_CHUNK, g * SSD_STATE:(g + 1) * SSD_STATE]
            sl = slice(j * LANES, (j + 1) * LANES)
            col_a = acs[r0:r0 + SSD_CHUNK, sl]
            row_a = jnp.where(lo_half1, v_lo[2 * j:2 * j + 1, :], v_hi[2 * j + 1:2 * j + 2, :])
            decay = jnp.where(causal, jnp.exp(col_a - row_a), 0.0)
            m = (cb2[g] * decay).astype(BF16)
            xdt = xs[r0:r0 + SSD_CHUNK, sl] * dt[r0:r0 + SSD_CHUNK, sl]
            zbd = jnp.concatenate([jnp.where(lo_half, xdt, 0.0), jnp.where(lo_half, 0.0, xdt)],
                                  axis=0).astype(BF16)
            y_diag = _dot(m, zbd)
            st = state_ref[j]
            y_off = _dot(cg, st.astype(BF16)) * jnp.exp(col_a)
            a_last = acs[r0 + SSD_CHUNK - 1:r0 + SSD_CHUNK, sl]
            zdte = (xdt * jnp.exp(a_last - col_a)).astype(BF16)
            state_ref[j] = st * jnp.exp(a_last) + _dot_tn(bg, zdte)
            y_pairs.append(y_diag + y_off)
        y_chunks.append(jnp.concatenate(y_pairs, axis=1))
        fill(1)
    y = jnp.concatenate(y_chunks, axis=0) if len(y_chunks) > 1 else y_chunks[0]
    ssmo_ref[u] = state_ref[...]

    y = y + xs * dskip_ref[...]
    gated = y * zs_ref[u]
    half = SSD_WIDTH // SSD_GROUPS
    outs = []
    for g in range(SSD_GROUPS):
        gg = gated[:, g * half:(g + 1) * half]
        outs.append(gg * lax.rsqrt(jnp.mean(gg * gg, axis=-1, keepdims=True) + EPS))
    ssd_out = jnp.concatenate(outs, axis=1) * ssmg_ref[...]
    fill(1)

    lane_g = lax.broadcasted_iota(jnp.int32, (GMLP_CHUNK, LANES), 1)
    lo_g = lane_g < GMLP_GROUP_DIM
    mixed_chunks = []
    for q in range(tl // GMLP_CHUNK):
        mixed_pairs = []
        for j in range(GMLP_GROUPS // 2):
            vp = gvs_ref[u, q * GMLP_CHUNK:(q + 1) * GMLP_CHUNK, j * LANES:(j + 1) * LANES]
            r_even = _dot(wspm_ref[2 * j], vp)
            r_odd = _dot(wspm_ref[2 * j + 1], vp)
            mixed_pairs.append(jnp.where(lo_g, r_even, r_odd))
        mixed_chunks.append(jnp.concatenate(mixed_pairs, axis=1) + bsp_ref[...])
    mixed = jnp.concatenate(mixed_chunks, axis=0) if len(mixed_chunks) > 1 else mixed_chunks[0]
    gmlp_out = gus_ref[u] * mixed

    merged = jnp.concatenate([ssd_out, gmlp_out], axis=1).astype(BF16)
    x1 = x + _dot(merged, wout_ref[...])
    x1_ref[u] = x1
    fill(len(fillers))

    h2 = x1 * lax.rsqrt(jnp.mean(x1 * x1, axis=-1, keepdims=True) + EPS) * g2_ref[...]
    h2_hi = h2.astype(BF16)
    h2_hif = h2_hi.astype(F32)
    h2_lo = (h2 - h2_hif).astype(BF16)
    bits = lax.bitcast_convert_type(h2_hif, jnp.uint32)
    hd = bits.shape[1] // 2
    h2p_ref[u] = jnp.bitwise_or(lax.shift_right_logical(bits[:, :hd], jnp.uint32(16)),
                                jnp.bitwise_and(bits[:, hd:], jnp.uint32(0xFFFF0000)))
    logit_ref[:, u * tl:(u + 1) * tl] = (
        _dot_nt(wr_hi_ref[...], h2_hi) + _dot_nt(wr_lo_ref[...], h2_hi)
        + _dot_nt(wr_hi_ref[...], h2_lo) + br_ref[...])
    vfill(len(vfillers))


def _full_spec(shape):
    nd = len(shape)
    return pl.BlockSpec(shape, lambda s, _nd=nd: (0,) * _nd)


def _mixer(x, conv0, ssm0, wts, *, tl, l_valid, emit_gv):
    bsz, l, d = x.shape
    nt = l // tl
    g = bsz * nt
    assert l % tl == 0 and tl % GMLP_CHUNK == 0 and l_valid % SUBLANES == 0
    assert (nt == 1 or l_valid == tl) and g % 2 == 0
    steps = g // 2
    tile = lambda f: (lambda s: (f(s), 0, 0))
    seq = lambda f: (lambda s: (f(s) // nt,) + (0,) * 2)
    seq4 = lambda f: (lambda s: (f(s) // nt,) + (0,) * 3)
    in_specs = [
        pl.BlockSpec((2, tl, d), tile(lambda s: s)),
        pl.BlockSpec((2, tl, d), tile(lambda s: jnp.minimum(s + 1, steps - 1))),
        pl.BlockSpec((1, SUBLANES, d), seq(lambda s: 2 * s)),
        pl.BlockSpec((1, SUBLANES, d), seq(lambda s: 2 * s + 1)),
        pl.BlockSpec((1, N_PAIRS, LANES, LANES), seq4(lambda s: 2 * s)),
        pl.BlockSpec((1, N_PAIRS, LANES, LANES), seq4(lambda s: 2 * s + 1)),
    ] + [_full_spec(w.shape) for w in wts]
    out_shape = [
        jax.ShapeDtypeStruct((g, tl, d), F32),
        jax.ShapeDtypeStruct((g, tl, d // 2), jnp.uint32),
        jax.ShapeDtypeStruct((ROUTER_ROWS, g * tl), F32),
        jax.ShapeDtypeStruct((g, SUBLANES, d), F32),
        jax.ShapeDtypeStruct((g, N_PAIRS, LANES, LANES), F32),
    ]
    out_specs = [
        pl.BlockSpec((2, tl, d), lambda s: (s, 0, 0)),
        pl.BlockSpec((2, tl, d // 2), lambda s: (s, 0, 0)),
        pl.BlockSpec((ROUTER_ROWS, 2 * tl), lambda s: (0, s)),
        pl.BlockSpec((2, SUBLANES, d), lambda s: (s, 0, 0)),
        pl.BlockSpec((2, N_PAIRS, LANES, LANES), lambda s: (s, 0, 0, 0)),
    ]
    if emit_gv:
        out_shape.append(jax.ShapeDtypeStruct((g, tl, GMLP_WIDTH), F32))
        out_specs.append(pl.BlockSpec((2, tl, GMLP_WIDTH), lambda s: (s, 0, 0)))
    xt = x.reshape(g, tl, d)
    outs = pl.pallas_call(
        functools.partial(_mixer_kernel, tl=tl, nt=nt, l_valid=l_valid, emit_gv=emit_gv),
        grid=(steps,),
        in_specs=in_specs,
        out_specs=out_specs,
        out_shape=out_shape,
        scratch_shapes=[
            pltpu.VMEM((tl, C_END), F32),
            pltpu.VMEM((tl, C_END), F32),
            pltpu.VMEM((SSD_HEADS, tl), F32),
            pltpu.VMEM((SSD_HEADS, tl), F32),
            pltpu.VMEM((2, tl + SUBLANES, d), F32),
            pltpu.VMEM((N_PAIRS, LANES, LANES), F32),
            pltpu.VMEM((GMLP_GROUPS, GMLP_CHUNK, GMLP_CHUNK), BF16),
            pltpu.VMEM((2, tl, SSD_WIDTH), F32),
            pltpu.VMEM((2, tl, GMLP_WIDTH), F32),
            pltpu.VMEM((2, tl, GMLP_WIDTH), BF16),
            pltpu.VMEM((2, tl, GMLP_WIDTH), F32),
            pltpu.VMEM((2, tl, SSD_WIDTH), F32),
            pltpu.VMEM((2, tl, SSD_WIDTH), F32),
            pltpu.VMEM((2, SSD_HEADS, tl), F32),
        ],
        compiler_params=pltpu.CompilerParams(
            dimension_semantics=("arbitrary",), vmem_limit_bytes=VMEM_LIMIT),
        name="mixer",
    )(xt, xt, conv0, conv0, ssm0, ssm0, *wts)
    x1, h2p, lt, cvo, sso = outs[:5]
    last = slice(nt - 1, None, nt)
    res = (x1.reshape(bsz, l, d), h2p.reshape(bsz, l, d // 2), lt, cvo[last], sso[last])
    if emit_gv:
        res += (outs[5].reshape(bsz, l, GMLP_WIDTH),)
    return res


def _route_kernel(logit_ref, tri_ref, dest_ref, w_ref, meta_ref, base_ref, keep_ref, *, tr, nt, blk):
    i = pl.program_id(0)
    sub8 = lax.broadcasted_iota(jnp.int32, (SUBLANES, tr), 0).astype(F32)
    sube = lax.broadcasted_iota(jnp.int32, (N_EXPERTS, tr), 0).astype(F32)

    @pl.when(i == 0)
    def _():
        base_ref[...] = jnp.zeros_like(base_ref)

    @pl.when(i < nt)
    def _():
        _route_pass0(logit_ref, tri_ref, base_ref, keep_ref, pl.multiple_of(i * tr, tr), sub8, sube, tr)

    @pl.when(i == nt)
    def _():
        counts = base_ref[...]
        padded = jnp.floor((counts + float(blk - 1)) * (1.0 / blk)) * float(blk)
        sub_e = lax.broadcasted_iota(jnp.int32, (N_EXPERTS, LANES), 0)
        pend = padded
        sh = 1
        while sh < N_EXPERTS:
            pend = pend + jnp.where(sub_e >= sh, pltpu.roll(pend, sh, axis=0), 0.0)
            sh *= 2
        pstart = pend - padded
        lane_e = lax.broadcasted_iota(jnp.int32, (N_EXPERTS, LANES), 1)
        meta = jnp.where(lane_e == 0, counts, jnp.where(lane_e == 1, pstart,
                         jnp.where(lane_e == 2, pend, 0.0)))
        meta_ref[...] = meta.astype(jnp.int32)

        def chunk(c, carry):
            sl = pl.ds(pl.multiple_of(c * tr, tr), tr)
            kept = keep_ref[:, sl]
            ps1 = jnp.sum(jnp.where(sube == kept[0:1], pstart[:, 0:1], 0.0), axis=0, keepdims=True)
            ps2 = jnp.sum(jnp.where(sube == kept[1:2], pstart[:, 0:1], 0.0), axis=0, keepdims=True)
            dest = jnp.where(sub8 == 0, ps1 + kept[2:3],
                             jnp.where(sub8 == 1, ps2 + kept[3:4], 0.0))
            dest_ref[:, sl] = dest.astype(jnp.int32)
            w_ref[:, sl] = jnp.where(sub8 == 0, kept[4:5], jnp.where(sub8 == 1, kept[5:6], 0.0))
            return carry
        lax.fori_loop(0, nt, chunk, 0)


def _route_pass0(logit_ref, tri_ref, base_ref, keep_ref, off, sub8, sube, tr):
    lg = logit_ref[...]
    big = float(SUBLANES)
    gl = jnp.where(sub8 < N_EXPERT_GROUPS, lg[0:SUBLANES], -jnp.inf)
    gmax = jnp.max(gl, axis=0, keepdims=True)
    g_sel = jnp.min(jnp.where(gl == gmax, sub8, big), axis=0, keepdims=True)
    p_group = 1.0 / jnp.sum(jnp.exp(gl - gmax), axis=0, keepdims=True)
    el = lg[SUBLANES:2 * SUBLANES]
    for g in range(1, N_EXPERT_GROUPS):
        el = jnp.where(g_sel == g, lg[(g + 1) * SUBLANES:(g + 2) * SUBLANES], el)
    top1 = jnp.max(el, axis=0, keepdims=True)
    i1 = jnp.min(jnp.where(el == top1, sub8, big), axis=0, keepdims=True)
    el2 = jnp.where(sub8 == i1, -jnp.inf, el)
    top2 = jnp.max(el2, axis=0, keepdims=True)
    i2 = jnp.min(jnp.where(el2 == top2, sub8, big), axis=0, keepdims=True)
    ex = jnp.exp(top2 - top1)
    w1 = p_group * (1.0 / (1.0 + ex))
    w2 = p_group * (ex / (1.0 + ex))
    e1 = g_sel * EXPERTS_PER_GROUP + i1
    e2 = g_sel * EXPERTS_PER_GROUP + i2

    oh1 = (sube == e1)
    oh2 = (sube == e2)
    tri = tri_ref[...]
    cum1 = _dot(jnp.where(oh1, 1.0, 0.0).astype(BF16), tri)
    cum2 = _dot(jnp.where(oh2, 1.0, 0.0).astype(BF16), tri)
    tot1 = jnp.sum(jnp.where(oh1, 1.0, 0.0), axis=1, keepdims=True)
    tot2 = jnp.sum(jnp.where(oh2, 1.0, 0.0), axis=1, keepdims=True)
    base = base_ref[:, 0:1]
    r1 = jnp.sum(jnp.where(oh1, cum1 + base, 0.0), axis=0, keepdims=True)
    r2 = jnp.sum(jnp.where(oh2, cum2 + base + tot1, 0.0), axis=0, keepdims=True)
    base_ref[...] = jnp.broadcast_to(base + tot1 + tot2, base_ref.shape)
    keep_ref[:, pl.ds(off, tr)] = jnp.where(
        sub8 == 0, e1, jnp.where(sub8 == 1, e2, jnp.where(sub8 == 2, r1, jnp.where(
            sub8 == 3, r2, jnp.where(sub8 == 4, w1, jnp.where(sub8 == 5, w2, 0.0))))))


def _route(logits_t, tr, blk):
    rows, t = logits_t.shape
    assert t % tr == 0
    nt = t // tr
    tri = jnp.triu(jnp.ones((tr, tr), BF16), k=1)
    return pl.pallas_call(
        functools.partial(_route_kernel, tr=tr, nt=nt, blk=blk),
        grid=(nt + 1,),
        in_specs=[pl.BlockSpec((rows, tr), lambda i, _n=nt: (0, jnp.minimum(i, _n - 1))),
                  pl.BlockSpec((tr, tr), lambda i: (0, 0))],
        out_specs=[pl.BlockSpec((SUBLANES, t), lambda i: (0, 0)),
                   pl.BlockSpec((SUBLANES, t), lambda i: (0, 0)),
                   pl.BlockSpec((N_EXPERTS, LANES), lambda i: (0, 0))],
        out_shape=[jax.ShapeDtypeStruct((SUBLANES, t), jnp.int32),
                   jax.ShapeDtypeStruct((SUBLANES, t), F32),
                   jax.ShapeDtypeStruct((N_EXPERTS, LANES), jnp.int32)],
        scratch_shapes=[pltpu.VMEM((N_EXPERTS, LANES), F32), pltpu.VMEM((SUBLANES, t), F32)],
        compiler_params=pltpu.CompilerParams(
            dimension_semantics=("arbitrary",), vmem_limit_bytes=VMEM_LIMIT),
        name="route",
    )(logits_t, tri)


def _dispatch_kernel(meta_ref, dest_ref, x1_hbm, xs_hbm, xin, zbuf, lsem, sem, zsem, *,
                     tmd, nsteps, blk, nb):
    i = pl.program_id(0)

    def load_tile(j, slot):
        return pltpu.make_async_copy(x1_hbm.at[pl.ds(pl.multiple_of(j * tmd, tmd), tmd)],
                                     xin.at[slot], lsem.at[slot])

    def wait_step(slot):
        for _ in range(2):
            pltpu.make_async_copy(xin.at[slot], xin.at[slot], sem.at[slot]).wait()

    def pad_rows(e, row_fn, oct_fn):
        start = meta_ref[N_EXPERTS + e] + meta_ref[e]
        start8 = lax.shift_right_logical(start + (SUBLANES - 1), 3)
        lax.fori_loop(start, start8 * SUBLANES, row_fn, 0)
        lax.fori_loop(start8, lax.shift_right_logical(meta_ref[2 * N_EXPERTS + e], 3), oct_fn, 0)

    def zero_row(row):
        return pltpu.make_async_copy(zbuf.at[pl.ds(0, 1)], xs_hbm.at[pl.ds(row, 1)], zsem)

    def zero_oct(o):
        return pltpu.make_async_copy(
            zbuf.at[pl.ds(0, SUBLANES)],
            xs_hbm.at[pl.ds(pl.multiple_of(o * SUBLANES, SUBLANES), SUBLANES)], zsem)

    def zero_block(b):
        return pltpu.make_async_copy(zbuf, xs_hbm.at[pl.ds(pl.multiple_of(b * blk, blk), blk)], zsem)

    @pl.when(i == 0)
    def _():
        zbuf[...] = jnp.zeros_like(zbuf)
        n_used = lax.div(meta_ref[3 * N_EXPERTS - 1], blk)

        def start_row(row, c):
            zero_row(row).start()
            return c

        def wait_row(row, c):
            zero_row(row).wait()
            return c

        def start_oct(o, c):
            zero_oct(o).start()
            return c

        def wait_oct(o, c):
            zero_oct(o).wait()
            return c

        def start_blk(b, c):
            zero_block(b).start()
            return c

        def wait_blk(b, c):
            zero_block(b).wait()
            return c

        for e in range(N_EXPERTS):
            pad_rows(e, start_row, start_oct)
        lax.fori_loop(n_used, nb, start_blk, 0)
        for e in range(N_EXPERTS):
            pad_rows(e, wait_row, wait_oct)
        lax.fori_loop(n_used, nb, wait_blk, 0)

        load_tile(0, 0).start()
        if nsteps > 1:
            load_tile(1, 1).start()

    slot = lax.rem(i, DISPATCH_RING)
    load_tile(i, slot).wait()
    for r in range(tmd):
        for k in range(2):
            pltpu.make_async_copy(xin.at[slot, pl.ds(r, 1)],
                                  xs_hbm.at[pl.ds(dest_ref[0, 0, k * tmd + r], 1)],
                                  sem.at[slot]).start(priority=k)

    @pl.when(i > 0)
    def _():
        wait_step(lax.rem(i + DISPATCH_RING - 1, DISPATCH_RING))

    @pl.when(i + 2 < nsteps)
    def _():
        load_tile(i + 2, lax.rem(i + 2, DISPATCH_RING)).start()

    @pl.when(i == nsteps - 1)
    def _():
        wait_step(slot)


def _dispatch(rows, dest, meta_s, n_slots, tmd, blk):
    t, d = rows.shape
    dt = rows.dtype
    nsteps = t // tmd
    dest3 = dest.reshape(2, nsteps, tmd).transpose(1, 0, 2).reshape(nsteps, 1, 2 * tmd)
    grid_spec = pltpu.PrefetchScalarGridSpec(
        num_scalar_prefetch=1,
        grid=(nsteps,),
        in_specs=[pl.BlockSpec((1, 1, 2 * tmd), lambda i, m: (i, 0, 0), memory_space=pltpu.SMEM),
                  pl.BlockSpec(memory_space=pl.ANY)],
        out_specs=pl.BlockSpec(memory_space=pl.ANY),
        scratch_shapes=[pltpu.VMEM((DISPATCH_RING, tmd, d), dt), pltpu.VMEM((blk, d), dt),
                        pltpu.SemaphoreType.DMA((DISPATCH_RING,)),
                        pltpu.SemaphoreType.DMA((DISPATCH_RING,)), pltpu.SemaphoreType.DMA(())],
    )
    return pl.pallas_call(
        functools.partial(_dispatch_kernel, tmd=tmd, nsteps=nsteps, blk=blk, nb=n_slots // blk),
        grid_spec=grid_spec,
        out_shape=jax.ShapeDtypeStruct((n_slots, d), dt),
        compiler_params=pltpu.CompilerParams(dimension_semantics=("arbitrary",)),
        name="dispatch",
    )(meta_s, dest3, rows)


def _expert_kernel(be_ref, nused_ref, xs_hbm, wg_ref, wu_ref, wd_ref, ys_ref, wgb, wub, wdb,
                   xring, lsem):
    i = pl.program_id(0)
    nused = nused_ref[0]
    changed = jnp.logical_or(i == 0, be_ref[i] != be_ref[jnp.maximum(i - 1, 0)])
    blk = ys_ref.shape[0]
    slot = lax.rem(i, EXPERT_RING)

    def load_block(j):
        s = lax.rem(j, EXPERT_RING)
        return pltpu.make_async_copy(xs_hbm.at[pl.ds(pl.multiple_of(j * blk, blk), blk)],
                                     xring.at[s], lsem.at[s])

    @pl.when(i == 0)
    def _():
        for j in range(EXPERT_RING - 1):
            @pl.when(j < nused)
            def _():
                load_block(j).start()

    @pl.when(i + EXPERT_RING - 1 < nused)
    def _():
        load_block(i + EXPERT_RING - 1).start()

    @pl.when(jnp.logical_and(i < nused, changed))
    def _():
        wgb[...] = wg_ref[0].astype(BF16)
        wub[...] = wu_ref[0].astype(BF16)
        wdb[...] = wd_ref[0].astype(BF16)

    @pl.when(i < nused)
    def _():
        load_block(i).wait()
        cuts = (0, blk // 2, blk) if blk >= 2 * LANES else (0, blk)
        hd = xring.shape[2]

        def unpack(a, b):
            w = xring[slot, a:b, :]
            lo = lax.bitcast_convert_type(lax.shift_left(w, jnp.uint32(16)), F32)
            hi = lax.bitcast_convert_type(jnp.bitwise_and(w, jnp.uint32(0xFFFF0000)), F32)
            return lo.astype(BF16), hi.astype(BF16)

        def proj(h, w_ref):
            return _dot(h[0], w_ref[0:hd, :]) + _dot(h[1], w_ref[hd:2 * hd, :])

        def act(gu):
            return (_silu(gu[0]) * gu[1]).astype(BF16)

        hbs = [unpack(a, b) for a, b in zip(cuts[:-1], cuts[1:])]
        gus = [(proj(hb, wgb), proj(hb, wub)) for hb in hbs]
        for (a, b), gu in zip(zip(cuts[:-1], cuts[1:]), gus):
            ys_ref[a:b, :] = _dot(act(gu), wdb[...])

    @pl.when(i >= nused)
    def _():
        ys_ref[...] = jnp.zeros_like(ys_ref)


def _experts(xs, blk_expert, n_used, w_gate, w_up, w_down, blk):
    n_slots, hd = xs.shape
    d = 2 * hd
    nb = n_slots // blk
    de = w_gate.shape[-1]
    grid_spec = pltpu.PrefetchScalarGridSpec(
        num_scalar_prefetch=2,
        grid=(nb,),
        in_specs=[
            pl.BlockSpec(memory_space=pl.ANY),
            pl.BlockSpec((1, d, de), lambda i, be, nu: (be[i], 0, 0)),
            pl.BlockSpec((1, d, de), lambda i, be, nu: (be[i], 0, 0)),
            pl.BlockSpec((1, de, d), lambda i, be, nu: (be[i], 0, 0)),
        ],
        out_specs=pl.BlockSpec((blk, d), lambda i, be, nu: (i, 0)),
        scratch_shapes=[
            pltpu.VMEM((d, de), BF16),
            pltpu.VMEM((d, de), BF16),
            pltpu.VMEM((de, d), BF16),
            pltpu.VMEM((EXPERT_RING, blk, hd), xs.dtype),
            pltpu.SemaphoreType.DMA((EXPERT_RING,)),
        ],
    )
    return pl.pallas_call(
        _expert_kernel,
        grid_spec=grid_spec,
        out_shape=jax.ShapeDtypeStruct((n_slots, d), F32),
        compiler_params=pltpu.CompilerParams(
            dimension_semantics=("arbitrary",), vmem_limit_bytes=VMEM_LIMIT),
        name="experts",
    )(blk_expert, n_used, xs, w_gate, w_up, w_down)


def _gather_rows(idx_ref, n_rows, src_hbm, dst_ref, sem):
    for r in range(n_rows):
        pltpu.make_async_copy(src_hbm.at[pl.ds(idx_ref[0, 0, r], 1)],
                              dst_ref.at[pl.ds(r, 1)], sem).start(priority=r % 2)


def _combine_kernel(posc_ref, posn_ref, x1_ref, w_ref, gf_ref, ys_hbm, y_ref, ybuf, sem, *, tm, nsteps):
    i = pl.program_id(0)
    slot = i % 2

    @pl.when(i == 0)
    def _():
        _gather_rows(posc_ref, 2 * tm, ys_hbm, ybuf.at[0], sem.at[0])

    @pl.when(i + 1 < nsteps)
    def _():
        _gather_rows(posn_ref, 2 * tm, ys_hbm, ybuf.at[1 - slot], sem.at[1 - slot])

    pltpu.make_async_copy(ybuf.at[1 - slot], ybuf.at[slot], sem.at[slot]).wait()
    w = w_ref[...]
    xo = (x1_ref[...] + w[:, 0:1] * ybuf[slot, 0:tm, :] + w[:, 1:2] * ybuf[slot, tm:2 * tm, :])
    y_ref[...] = xo * lax.rsqrt(jnp.mean(xo * xo, axis=-1, keepdims=True) + EPS) * gf_ref[...]


def _combine(x1_flat, pos, w_col, ys, gf, tm):
    t, d = x1_flat.shape
    nsteps = t // tm
    pos3 = pos.reshape(nsteps, 1, 2 * tm)
    return pl.pallas_call(
        functools.partial(_combine_kernel, tm=tm, nsteps=nsteps),
        grid=(nsteps,),
        in_specs=[
            pl.BlockSpec((1, 1, 2 * tm), lambda i: (i, 0, 0), memory_space=pltpu.SMEM),
            pl.BlockSpec((1, 1, 2 * tm), lambda i, _n=nsteps: (jnp.minimum(i + 1, _n - 1), 0, 0),
                         memory_space=pltpu.SMEM),
            pl.BlockSpec((tm, d), lambda i: (i, 0)),
            pl.BlockSpec((tm, 2), lambda i: (i, 0)),
            pl.BlockSpec((1, d), lambda i: (0, 0)),
            pl.BlockSpec(memory_space=pl.ANY),
        ],
        out_specs=pl.BlockSpec((tm, d), lambda i: (i, 0)),
        out_shape=jax.ShapeDtypeStruct((t, d), F32),
        scratch_shapes=[pltpu.VMEM((2, 2 * tm, d), F32), pltpu.SemaphoreType.DMA((2,))],
        compiler_params=pltpu.CompilerParams(
            dimension_semantics=("arbitrary",), vmem_limit_bytes=VMEM_LIMIT),
        name="combine",
    )(pos3, pos3, x1_flat, w_col, gf, ys)


def _moe_and_norm(x1_flat, h2p_flat, logits_t, w_gate, w_up, w_down, gf):
    t, d = x1_flat.shape
    tr = _pick(t, (512, 256, 128))
    tm = _pick(t, (512, 256, 128))
    tmd = _pick(t, (512, 256, 128))
    blk = MOE_BLOCK if 2 * t >= 4 * N_EXPERTS * MOE_BLOCK else MOE_BLOCK_SMALL
    dest8, w8, meta = _route(logits_t, tr, blk)
    dest = dest8[0:2]
    nb = (2 * t + blk - 1) // blk + N_EXPERTS
    pend = meta[:, 2]
    blk_expert = jnp.minimum(
        jnp.sum(jnp.arange(nb, dtype=jnp.int32)[:, None] * blk >= pend[None, :], axis=1),
        N_EXPERTS - 1).astype(jnp.int32)
    n_used = (pend[-1:] // blk).astype(jnp.int32)
    xs = _dispatch(h2p_flat, dest, meta[:, 0:3].T.reshape(3 * N_EXPERTS), nb * blk, tmd, blk)
    ys = _experts(xs, blk_expert, n_used, w_gate, w_up, w_down, blk)
    nsteps = t // tm
    pos = dest.reshape(2, nsteps, tm).transpose(1, 0, 2).reshape(nsteps * 2 * tm)
    return _combine(x1_flat, pos, w8[0:2].T, ys, gf, tm)


def _prep_layer_weights(norm1_g, w_in, conv_w, conv_b, dt_bias, a_log, d_skip, ssm_norm_g, ln_v_g,
                        ln_v_b, w_spatial, b_spatial, w_out, norm2_g, w_rg, b_rg, w_re, b_re):
    d = w_in.shape[0]
    conv_dim = SSD_WIDTH + 2 * SSD_GROUPS * SSD_STATE
    o_xbc = SSD_WIDTH
    o_dt = o_xbc + conv_dim
    o_gu = o_dt + SSD_HEADS
    o_gv = o_gu + GMLP_WIDTH
    w_dt = w_in[:, o_dt:o_gu]
    w_in_r = jnp.concatenate(
        [w_in[:, :o_xbc], w_in[:, o_xbc:o_dt], w_in[:, o_gu:o_gv], w_in[:, o_gv:],
         jnp.pad(w_dt, ((0, 0), (0, LANES - SSD_HEADS)))], axis=1).astype(BF16)
    rep = lambda v: jnp.repeat(v, SSD_HEAD_DIM)[None, :]
    col = lambda v: v[:, None]
    narrow = lambda v: jnp.pad(v, (0, LANES - SSD_HEADS))[None, :]
    sel = (jnp.arange(LANES)[:, None] == jnp.arange(SSD_WIDTH)[None, :] // SSD_HEAD_DIM).astype(BF16)
    w_r = jnp.concatenate([w_rg, jnp.zeros((d, SUBLANES - N_EXPERT_GROUPS), F32), w_re], axis=1).T
    b_r = jnp.concatenate([b_rg, jnp.zeros((SUBLANES - N_EXPERT_GROUPS,), F32), b_re])[:, None]
    w_r_hi = w_r.astype(BF16)
    w_r_lo = (w_r - w_r_hi.astype(F32)).astype(BF16)
    bsp = jnp.repeat(b_spatial.T, GMLP_GROUP_DIM, axis=1)
    return (norm1_g[None, :], w_in_r, w_dt.T.astype(BF16), conv_w, conv_b[None, :],
            narrow(dt_bias), narrow(a_log), sel, col(dt_bias), col(a_log), rep(d_skip),
            ssm_norm_g[None, :],
            ln_v_g[None, :], ln_v_b[None, :], w_spatial, bsp, w_out.astype(BF16), norm2_g[None, :],
            w_r_hi, w_r_lo, b_r)


def _state_to_pairs(s):
    b = s.shape[0]
    return s.reshape(b, N_PAIRS, 2, SSD_HEAD_DIM, SSD_STATE).transpose(0, 1, 4, 2, 3).reshape(
        b, N_PAIRS, SSD_STATE, 2 * SSD_HEAD_DIM)


def _pairs_to_state(s):
    b = s.shape[0]
    return s.reshape(b, N_PAIRS, SSD_STATE, 2, SSD_HEAD_DIM).transpose(0, 1, 3, 4, 2).reshape(
        b, SSD_HEADS, SSD_HEAD_DIM, SSD_STATE)


def _pick(n, prefs):
    for p in prefs:
        if n % p == 0:
            return p
    return n


def kernel(x_prompt, x_sample, cache_conv, state_ssm, norm1_g, w_in, conv_w, conv_b, dt_bias, a_log, d_skip, ssm_norm_g, ln_v_g, ln_v_b, w_spatial, b_spatial, w_out, norm2_g, w_router_group, b_router_group, w_router_expert, b_router_expert, w_gate, w_up, w_down, final_norm_g):
    depth = w_in.shape[0]
    assert depth == 1, "the combine kernel fuses the final norm, so only a single layer is supported"
    bp, lp, d = x_prompt.shape
    bs, ls, _ = x_sample.shape
    conv_dim = cache_conv.shape[-1]
    ls_pad = -(-ls // GMLP_CHUNK) * GMLP_CHUNK
    tl_p = _pick(lp, (256, 128))
    gf = final_norm_g[None, :]

    yp = x_prompt
    ys = jnp.pad(x_sample, ((0, 0), (0, ls_pad - ls), (0, 0)))
    conv_p, ssm_p, conv_s, ssm_s, v_s = [], [], [], [], []
    for i in range(depth):
        wts = _prep_layer_weights(
            norm1_g[i], w_in[i], conv_w[i], conv_b[i], dt_bias[i], a_log[i], d_skip[i], ssm_norm_g[i],
            ln_v_g[i], ln_v_b[i], w_spatial[i], b_spatial[i], w_out[i], norm2_g[i],
            w_router_group[i], b_router_group[i], w_router_expert[i], b_router_expert[i])
        g2 = norm2_g[i][None, :]
        gfi = gf

        conv0 = jnp.zeros((bp, SUBLANES, conv_dim), F32)
        ssm0 = jnp.zeros((bp, N_PAIRS, LANES, LANES), F32)
        x1, h2p, lt, cpo, hpo = _mixer(yp, conv0, ssm0, wts, tl=tl_p, l_valid=tl_p, emit_gv=False)
        tp = bp * lp
        outp = _moe_and_norm(x1.reshape(tp, d), h2p.reshape(tp, d // 2), lt, w_gate[i], w_up[i],
                             w_down[i], gfi)
        yp = outp.reshape(bp, lp, d)
        conv_p.append(cpo[:, SUBLANES - (CONV_WIDTH - 1):, :])
        ssm_p.append(_pairs_to_state(hpo))

        conv0 = jnp.pad(cache_conv[i], ((0, 0), (SUBLANES - (CONV_WIDTH - 1), 0), (0, 0)))
        x1, h2p, lt, cso, hso, gv = _mixer(ys, conv0, _state_to_pairs(state_ssm[i]), wts,
                                           tl=ls_pad, l_valid=ls, emit_gv=True)
        tsn = bs * ls
        x1v = x1[:, :ls].reshape(tsn, d)
        h2pv = h2p[:, :ls].reshape(tsn, d // 2)
        ltv = lt.reshape(ROUTER_ROWS, bs, ls_pad)[:, :, :ls].reshape(ROUTER_ROWS, tsn)
        outs = _moe_and_norm(x1v, h2pv, ltv, w_gate[i], w_up[i], w_down[i], gfi)
        ys = outs.reshape(bs, ls, d)
        conv_s.append(cso[:, SUBLANES - (CONV_WIDTH - 1):, :])
        ssm_s.append(_pairs_to_state(hso))
        v_s.append(gv[:, :ls])
    return (yp, ys, jnp.stack(conv_p), jnp.stack(ssm_p), jnp.stack(conv_s), jnp.stack(ssm_s),
            jnp.stack(v_s))
```

```python
import functools

import jax
import jax.numpy as jnp
from jax import lax
from jax.experimental import pallas as pl
from jax.experimental.pallas import tpu as pltpu

F32 = jnp.float32
BF16 = jnp.bfloat16
EPS = 1e-6

SSD_HEADS = 8
SSD_HEAD_DIM = 64
SSD_STATE = 128
SSD_GROUPS = 2
SSD_WIDTH = SSD_HEADS * SSD_HEAD_DIM
SSD_CHUNK = 64
CONV_WIDTH = 4
GMLP_GROUPS = 8
GMLP_GROUP_DIM = 64
GMLP_WIDTH = GMLP_GROUPS * GMLP_GROUP_DIM
GMLP_CHUNK = 128
N_EXPERT_GROUPS = 4
EXPERTS_PER_GROUP = 8
N_EXPERTS = N_EXPERT_GROUPS * EXPERTS_PER_GROUP
ROUTER_ROWS = 8 + N_EXPERTS
MOE_BLOCK = 512
MOE_BLOCK_SMALL = 32
DISPATCH_RING = 3
EXPERT_RING = 3
COMBINE_PARTS = 4

LANES = 128
SUBLANES = 8
N_PAIRS = SSD_HEADS // 2
VMEM_LIMIT = 56 * 1024 * 1024

C_Z = 0
C_XBC = C_Z + SSD_WIDTH
C_GU = C_XBC + SSD_WIDTH + 2 * SSD_GROUPS * SSD_STATE
C_GV = C_GU + GMLP_WIDTH
C_DT = C_GV + GMLP_WIDTH
C_END = C_DT + LANES
PROJ_SLAB = 256


def _silu(v):
    return v * (1.0 / (1.0 + jnp.exp(-v)))


def _gelu_tanh(v):
    c = 0.7978845608028654
    hv = 0.5 * v
    return hv + hv * jnp.tanh(v * (c + (c * 0.044715) * (v * v)))


def _softplus(v):
    return jnp.maximum(v, 0.0) + jnp.log1p(jnp.exp(-jnp.abs(v)))


def _dot(a, b):
    return jnp.dot(a, b, preferred_element_type=F32)


def _dot_nt(a, b):
    return lax.dot_general(a, b, (((1,), (1,)), ((), ())), preferred_element_type=F32)


def _dot_tn(a, b):
    return lax.dot_general(a, b, (((0,), (0,)), ((), ())), preferred_element_type=F32)


def _replicate_heads(v, sel_ref):
    hi = v.astype(BF16)
    r1 = v - hi.astype(F32)
    mid = r1.astype(BF16)
    lo = (r1 - mid.astype(F32)).astype(BF16)
    sel = sel_ref[...]
    return _dot(hi, sel) + _dot(mid, sel) + _dot(lo, sel)


def _mixer_kernel(xb_ref, xf_ref, conv0a_ref, conv0b_ref, ssm0a_ref, ssm0b_ref,
                  g1_ref, win_ref, wdt_ref, convw_ref, convb_ref,
                  dtb_n_ref, alog_n_ref, sel_ref, dtb_col_ref, alog_col_ref, dskip_ref, ssmg_ref,
                  lng_ref, lnb_ref, wsp_ref, bsp_ref, wout_ref, g2_ref, wr_hi_ref, wr_lo_ref,
                  br_ref,
                  x1_ref, h2p_ref, logit_ref, convo_ref, ssmo_ref, *rest,
                  tl, nt, l_valid, emit_gv):
    if emit_gv:
        gvo_ref = rest[0]
        rest = rest[1:]
    else:
        gvo_ref = None
    (pa_ref, pb_ref, da_ref, db_ref, xpad_ref, state_ref, wspm_ref, zs_ref, gus_ref, gvs_ref,
     gvf_ref, dtc_ref, acs_ref, dtt_ref) = rest
    s = pl.program_id(0)

    def drain_pieces(pbuf, dbuf, v):
        def p_xbc():
            xpad_ref[v, SUBLANES:SUBLANES + tl, :] = pbuf[:, C_XBC:C_GU]

        def p_dt():
            dtn = _softplus(pbuf[:, C_DT:C_END] + dtb_n_ref[...])
            row = lax.broadcasted_iota(jnp.int32, (tl, LANES), 0)
            if l_valid < tl:
                dtn = jnp.where(row < l_valid, dtn, 0.0)
            acsn = dtn * (-jnp.exp(alog_n_ref[...]))
            row_in_chunk = jnp.bitwise_and(row, SSD_CHUNK - 1)
            sh = 1
            while sh < SSD_CHUNK:
                acsn = acsn + jnp.where(row_in_chunk >= sh, pltpu.roll(acsn, sh, axis=0), 0.0)
                sh *= 2
            dtc_ref[v] = _replicate_heads(dtn, sel_ref)
            acs_ref[v] = _replicate_heads(acsn, sel_ref)
            dtt_ref[v] = _softplus(dbuf[...] + dtb_col_ref[...])

        def p_z():
            zs_ref[v] = _silu(pbuf[:, C_Z:C_XBC])

        def p_gu():
            gus_ref[v] = _gelu_tanh(pbuf[:, C_GU:C_GV])

        def p_gv():
            gv = _gelu_tanh(pbuf[:, C_GV:C_DT])
            mu = jnp.mean(gv, axis=-1, keepdims=True)
            gvc = gv - mu
            var = jnp.mean(gvc * gvc, axis=-1, keepdims=True)
            gv = gvc * lax.rsqrt(var + EPS) * lng_ref[...] + lnb_ref[...]
            if gvo_ref is not None:
                gvf_ref[v] = gv
            gvs_ref[v] = gv.astype(BF16)
        return [p_xbc, p_dt, p_z, p_gu, p_gv]

    def front_pieces(x, pbuf, dbuf):
        hb = []

        def norm():
            h = x * lax.rsqrt(jnp.mean(x * x, axis=-1, keepdims=True) + EPS) * g1_ref[...]
            hb.append(h.astype(BF16))
            dbuf[...] = _dot_nt(wdt_ref[...], hb[0])

        def slab(c0):
            c1 = min(c0 + PROJ_SLAB, C_END)

            def run():
                pbuf[:, c0:c1] = _dot(hb[0], win_ref[:, c0:c1])
            return run
        return [norm] + [slab(c0) for c0 in range(0, C_END, PROJ_SLAB)]

    def front(x, pbuf, dbuf):
        for piece in front_pieces(x, pbuf, dbuf):
            piece()

    @pl.when(s == 0)
    def _():
        r = lax.broadcasted_iota(jnp.int32, (GMLP_CHUNK, GMLP_CHUNK), 0)
        c = lax.broadcasted_iota(jnp.int32, (GMLP_CHUNK, GMLP_CHUNK), 1)
        for gi in range(GMLP_GROUPS):
            wspm_ref[gi] = jnp.where(r >= c, wsp_ref[gi], 0.0).astype(BF16)
        for v in range(2):
            xpad_ref[v, 0:SUBLANES, :] = jnp.zeros((SUBLANES, xpad_ref.shape[2]), F32)
        state_ref[...] = jnp.zeros_like(state_ref)
        front(xb_ref[0], pa_ref, da_ref)
        front(xb_ref[1], pb_ref, db_ref)
        for piece in drain_pieces(pa_ref, da_ref, 0):
            piece()

    bufs = ((pa_ref, da_ref, conv0a_ref, ssm0a_ref), (pb_ref, db_ref, conv0b_ref, ssm0b_ref))
    for u, (pbuf, dbuf, conv0_ref, ssm0_ref) in enumerate(bufs):
        is_first = lax.rem(2 * s + u, nt) == 0
        xpad_ref[u, 0:SUBLANES, :] = jnp.where(is_first, conv0_ref[0], xpad_ref[u, 0:SUBLANES, :])
        for j in range(N_PAIRS):
            state_ref[j] = jnp.where(is_first, ssm0_ref[0, j], state_ref[j])
        if gvo_ref is not None:
            gvo_ref[u] = gvf_ref[u]

        nxt = bufs[1 - u]
        _mixer_back(
            xb_ref[u], front_pieces(xf_ref[u], pbuf, dbuf), drain_pieces(nxt[0], nxt[1], 1 - u),
            convw_ref=convw_ref,
            convb_ref=convb_ref, alog_col_ref=alog_col_ref, dskip_ref=dskip_ref,
            ssmg_ref=ssmg_ref, bsp_ref=bsp_ref, wout_ref=wout_ref, g2_ref=g2_ref,
            wr_hi_ref=wr_hi_ref, wr_lo_ref=wr_lo_ref, br_ref=br_ref, x1_ref=x1_ref,
            h2p_ref=h2p_ref, logit_ref=logit_ref, convo_ref=convo_ref, ssmo_ref=ssmo_ref, xpad_ref=xpad_ref,
            state_ref=state_ref, wspm_ref=wspm_ref, zs_ref=zs_ref, gus_ref=gus_ref,
            gvs_ref=gvs_ref, dtc_ref=dtc_ref, acs_ref=acs_ref, dtt_ref=dtt_ref,
            u=u, tl=tl, l_valid=l_valid)


def _mixer_back(x, fillers, vfillers, *, convw_ref, convb_ref, alog_col_ref, dskip_ref,
                ssmg_ref, bsp_ref, wout_ref, g2_ref, wr_hi_ref, wr_lo_ref, br_ref, x1_ref, h2p_ref,
                logit_ref, convo_ref, ssmo_ref, xpad_ref, state_ref, wspm_ref, zs_ref, gus_ref,
                gvs_ref, dtc_ref, acs_ref, dtt_ref, u, tl, l_valid):
    fillers = list(fillers)
    vfillers = list(vfillers)
    dt = dtc_ref[u]
    acs = acs_ref[u]
    dtt = dtt_ref[u]

    def fill(n):
        for _ in range(min(n, len(fillers))):
            fillers.pop(0)()

    def vfill(n):
        for _ in range(min(n, len(vfillers))):
            vfillers.pop(0)()

    fill(1)
    conv = convb_ref[...]
    for k in range(CONV_WIDTH):
        off = SUBLANES - (CONV_WIDTH - 1) + k
        conv = conv + xpad_ref[u, off:off + tl, :] * convw_ref[k:k + 1, :]
        fill(1)
    xbc = _silu(conv)
    carry = xpad_ref[u, l_valid:l_valid + SUBLANES, :]
    xpad_ref[1 - u, 0:SUBLANES, :] = carry
    convo_ref[u] = carry

    xs = xbc[:, 0:SSD_WIDTH]
    bm = xbc[:, SSD_WIDTH:SSD_WIDTH + SSD_GROUPS * SSD_STATE].astype(BF16)
    cm = xbc[:, SSD_WIDTH + SSD_GROUPS * SSD_STATE:].astype(BF16)

    lane_t = lax.broadcasted_iota(jnp.int32, (SSD_HEADS, tl), 1)
    if l_valid < tl:
        dtt = jnp.where(lane_t < l_valid, dtt, 0.0)
    acst = dtt * (-jnp.exp(alog_col_ref[...]))
    lane_in_chunk = jnp.bitwise_and(lane_t, SSD_CHUNK - 1)
    sh = 1
    while sh < SSD_CHUNK:
        acst = acst + jnp.where(lane_in_chunk >= sh, pltpu.roll(acst, sh, axis=1), 0.0)
        sh *= 2

    lane = lax.broadcasted_iota(jnp.int32, (SSD_CHUNK, LANES), 1)
    rowc = lax.broadcasted_iota(jnp.int32, (SSD_CHUNK, LANES), 0)
    lo_half = lane < SSD_HEAD_DIM
    causal = rowc >= jnp.bitwise_and(lane, SSD_CHUNK - 1)
    lane1 = lax.broadcasted_iota(jnp.int32, (1, LANES), 1)
    lo_half1 = lane1 < SSD_CHUNK

    y_chunks = []
    for c in range(tl // SSD_CHUNK):
        r0 = c * SSD_CHUNK
        v = acst[:, (c // 2) * LANES:(c // 2 + 1) * LANES]
        vr = pltpu.roll(v, SSD_CHUNK, axis=1)
        v_lo, v_hi = (v, vr) if c % 2 == 0 else (vr, v)
        cb2 = []
        for g in range(SSD_GROUPS):
            cg = cm[r0:r0 + SSD_CHUNK, g * SSD_STATE:(g + 1) * SSD_STATE]
            bg = bm[r0:r0 + SSD_CHUNK, g * SSD_STATE:(g + 1) * SSD_STATE]
            cb2.append(_dot_nt(cg, jnp.concatenate([bg, bg], axis=0)))
        y_pairs = []
        for j in range(N_PAIRS):
            g = j // (N_PAIRS // SSD_GROUPS)
            cg = cm[r0:r0 + SSD_CHUNK, g * SSD_STATE:(g + 1) * SSD_STATE]
            bg = bm[r0:r0 + SSD_CHUNK, g * SSD_STATE:(g + 1) * SSD_STATE]
            sl = slice(j * LANES, (j + 1) * LANES)
            col_a = acs[r0:r0 + SSD_CHUNK, sl]
            row_a = jnp.where(lo_half1, v_lo[2 * j:2 * j + 1, :], v_hi[2 * j + 1:2 * j + 2, :])
            decay = jnp.where(causal, jnp.exp(col_a - row_a), 0.0)
            m = (cb2[g] * decay).astype(BF16)
            xdt = xs[r0:r0 + SSD_CHUNK, sl] * dt[r0:r0 + SSD_CHUNK, sl]
            zbd = jnp.concatenate([jnp.where(lo_half, xdt, 0.0), jnp.where(lo_half, 0.0, xdt)],
                                  axis=0).astype(BF16)
            y_diag = _dot(m, zbd)
            st = state_ref[j]
            y_off = _dot(cg, st.astype(BF16)) * jnp.exp(col_a)
            a_last = acs[r0 + SSD_CHUNK - 1:r0 + SSD_CHUNK, sl]
            zdte = (xdt * jnp.exp(a_last - col_a)).astype(BF16)
            state_ref[j] = st * jnp.exp(a_last) + _dot_tn(bg, zdte)
            y_pairs.append(y_diag + y_off)
        y_chunks.append(jnp.concatenate(y_pairs, axis=1))
        fill(1)
    y = jnp.concatenate(y_chunks, axis=0) if len(y_chunks) > 1 else y_chunks[0]
    ssmo_ref[u] = state_ref[...]

    y = y + xs * dskip_ref[...]
    gated = y * zs_ref[u]
    half = SSD_WIDTH // SSD_GROUPS
    outs = []
    for g in range(SSD_GROUPS):
        gg = gated[:, g * half:(g + 1) * half]
        outs.append(gg * lax.rsqrt(jnp.mean(gg * gg, axis=-1, keepdims=True) + EPS))
    ssd_out = jnp.concatenate(outs, axis=1) * ssmg_ref[...]
    fill(1)

    lane_g = lax.broadcasted_iota(jnp.int32, (GMLP_CHUNK, LANES), 1)
    lo_g = lane_g < GMLP_GROUP_DIM
    mixed_chunks = []
    for q in range(tl // GMLP_CHUNK):
        mixed_pairs = []
        for j in range(GMLP_GROUPS // 2):
            vp = gvs_ref[u, q * GMLP_CHUNK:(q + 1) * GMLP_CHUNK, j * LANES:(j + 1) * LANES]
            r_even = _dot(wspm_ref[2 * j], vp)
            r_odd = _dot(wspm_ref[2 * j + 1], vp)
            mixed_pairs.append(jnp.where(lo_g, r_even, r_odd))
        mixed_chunks.append(jnp.concatenate(mixed_pairs, axis=1) + bsp_ref[...])
    mixed = jnp.concatenate(mixed_chunks, axis=0) if len(mixed_chunks) > 1 else mixed_chunks[0]
    gmlp_out = gus_ref[u] * mixed

    merged = jnp.concatenate([ssd_out, gmlp_out], axis=1).astype(BF16)
    x1 = x + _dot(merged, wout_ref[...])
    x1_ref[u] = x1
    fill(len(fillers))

    h2 = x1 * lax.rsqrt(jnp.mean(x1 * x1, axis=-1, keepdims=True) + EPS) * g2_ref[...]
    h2_hi = h2.astype(BF16)
    h2_hif = h2_hi.astype(F32)
    h2_lo = (h2 - h2_hif).astype(BF16)
    bits = lax.bitcast_convert_type(h2_hif, jnp.uint32)
    hd = bits.shape[1] // 2
    h2p_ref[u] = jnp.bitwise_or(lax.shift_right_logical(bits[:, :hd], jnp.uint32(16)),
                                jnp.bitwise_and(bits[:, hd:], jnp.uint32(0xFFFF0000)))
    logit_ref[:, u * tl:(u + 1) * tl] = (
        _dot_nt(wr_hi_ref[...], h2_hi) + _dot_nt(wr_lo_ref[...], h2_hi)
        + _dot_nt(wr_hi_ref[...], h2_lo) + br_ref[...])
    vfill(len(vfillers))


def _full_spec(shape):
    nd = len(shape)
    return pl.BlockSpec(shape, lambda s, _nd=nd: (0,) * _nd)


def _mixer(x, conv0, ssm0, wts, *, tl, l_valid, emit_gv):
    bsz, l, d = x.shape
    nt = l // tl
    g = bsz * nt
    assert l % tl == 0 and tl % GMLP_CHUNK == 0 and l_valid % SUBLANES == 0
    assert (nt == 1 or l_valid == tl) and g % 2 == 0
    steps = g // 2
    tile = lambda f: (lambda s: (f(s), 0, 0))
    seq = lambda f: (lambda s: (f(s) // nt,) + (0,) * 2)
    seq4 = lambda f: (lambda s: (f(s) // nt,) + (0,) * 3)
    in_specs = [
        pl.BlockSpec((2, tl, d), tile(lambda s: s)),
        pl.BlockSpec((2, tl, d), tile(lambda s: jnp.minimum(s + 1, steps - 1))),
        pl.BlockSpec((1, SUBLANES, d), seq(lambda s: 2 * s)),
        pl.BlockSpec((1, SUBLANES, d), seq(lambda s: 2 * s + 1)),
        pl.BlockSpec((1, N_PAIRS, LANES, LANES), seq4(lambda s: 2 * s)),
        pl.BlockSpec((1, N_PAIRS, LANES, LANES), seq4(lambda s: 2 * s + 1)),
    ] + [_full_spec(w.shape) for w in wts]
    out_shape = [
        jax.ShapeDtypeStruct((g, tl, d), F32),
        jax.ShapeDtypeStruct((g, tl, d // 2), jnp.uint32),
        jax.ShapeDtypeStruct((ROUTER_ROWS, g * tl), F32),
        jax.ShapeDtypeStruct((g, SUBLANES, d), F32),
        jax.ShapeDtypeStruct((g, N_PAIRS, LANES, LANES), F32),
    ]
    out_specs = [
        pl.BlockSpec((2, tl, d), lambda s: (s, 0, 0)),
        pl.BlockSpec((2, tl, d // 2), lambda s: (s, 0, 0)),
        pl.BlockSpec((ROUTER_ROWS, 2 * tl), lambda s: (0, s)),
        pl.BlockSpec((2, SUBLANES, d), lambda s: (s, 0, 0)),
        pl.BlockSpec((2, N_PAIRS, LANES, LANES), lambda s: (s, 0, 0, 0)),
    ]
    if emit_gv:
        out_shape.append(jax.ShapeDtypeStruct((g, tl, GMLP_WIDTH), F32))
        out_specs.append(pl.BlockSpec((2, tl, GMLP_WIDTH), lambda s: (s, 0, 0)))
    xt = x.reshape(g, tl, d)
    outs = pl.pallas_call(
        functools.partial(_mixer_kernel, tl=tl, nt=nt, l_valid=l_valid, emit_gv=emit_gv),
        grid=(steps,),
        in_specs=in_specs,
        out_specs=out_specs,
        out_shape=out_shape,
        scratch_shapes=[
            pltpu.VMEM((tl, C_END), F32),
            pltpu.VMEM((tl, C_END), F32),
            pltpu.VMEM((SSD_HEADS, tl), F32),
            pltpu.VMEM((SSD_HEADS, tl), F32),
            pltpu.VMEM((2, tl + SUBLANES, d), F32),
            pltpu.VMEM((N_PAIRS, LANES, LANES), F32),
            pltpu.VMEM((GMLP_GROUPS, GMLP_CHUNK, GMLP_CHUNK), BF16),
            pltpu.VMEM((2, tl, SSD_WIDTH), F32),
            pltpu.VMEM((2, tl, GMLP_WIDTH), F32),
            pltpu.VMEM((2, tl, GMLP_WIDTH), BF16),
            pltpu.VMEM((2, tl, GMLP_WIDTH), F32),
            pltpu.VMEM((2, tl, SSD_WIDTH), F32),
            pltpu.VMEM((2, tl, SSD_WIDTH), F32),
            pltpu.VMEM((2, SSD_HEADS, tl), F32),
        ],
        compiler_params=pltpu.CompilerParams(
            dimension_semantics=("arbitrary",), vmem_limit_bytes=VMEM_LIMIT),
        name="mixer",
    )(xt, xt, conv0, conv0, ssm0, ssm0, *wts)
    x1, h2p, lt, cvo, sso = outs[:5]
    last = slice(nt - 1, None, nt)
    res = (x1.reshape(bsz, l, d), h2p.reshape(bsz, l, d // 2), lt, cvo[last], sso[last])
    if emit_gv:
        res += (outs[5].reshape(bsz, l, GMLP_WIDTH),)
    return res


def _route_kernel(logit_ref, tri_ref, dest_ref, w_ref, meta_ref, base_ref, keep_ref, *, tr, nt, blk):
    i = pl.program_id(0)
    sub8 = lax.broadcasted_iota(jnp.int32, (SUBLANES, tr), 0).astype(F32)
    sube = lax.broadcasted_iota(jnp.int32, (N_EXPERTS, tr), 0).astype(F32)

    @pl.when(i == 0)
    def _():
        base_ref[...] = jnp.zeros_like(base_ref)

    @pl.when(i < nt)
    def _():
        _route_pass0(logit_ref, tri_ref, base_ref, keep_ref, pl.multiple_of(i * tr, tr), sub8, sube, tr)

    @pl.when(i == nt)
    def _():
        counts = base_ref[...]
        padded = jnp.floor((counts + float(blk - 1)) * (1.0 / blk)) * float(blk)
        sub_e = lax.broadcasted_iota(jnp.int32, (N_EXPERTS, LANES), 0)
        pend = padded
        sh = 1
        while sh < N_EXPERTS:
            pend = pend + jnp.where(sub_e >= sh, pltpu.roll(pend, sh, axis=0), 0.0)
            sh *= 2
        pstart = pend - padded
        lane_e = lax.broadcasted_iota(jnp.int32, (N_EXPERTS, LANES), 1)
        meta = jnp.where(lane_e == 0, counts, jnp.where(lane_e == 1, pstart,
                         jnp.where(lane_e == 2, pend, 0.0)))
        meta_ref[...] = meta.astype(jnp.int32)

        def chunk(c, carry):
            sl = pl.ds(pl.multiple_of(c * tr, tr), tr)
            kept = keep_ref[:, sl]
            ps1 = jnp.sum(jnp.where(sube == kept[0:1], pstart[:, 0:1], 0.0), axis=0, keepdims=True)
            ps2 = jnp.sum(jnp.where(sube == kept[1:2], pstart[:, 0:1], 0.0), axis=0, keepdims=True)
            dest = jnp.where(sub8 == 0, ps1 + kept[2:3],
                             jnp.where(sub8 == 1, ps2 + kept[3:4], 0.0))
            dest_ref[:, sl] = dest.astype(jnp.int32)
            w_ref[:, sl] = jnp.where(sub8 == 0, kept[4:5], jnp.where(sub8 == 1, kept[5:6], 0.0))
            return carry
        lax.fori_loop(0, nt, chunk, 0)


def _route_pass0(logit_ref, tri_ref, base_ref, keep_ref, off, sub8, sube, tr):
    lg = logit_ref[...]
    big = float(SUBLANES)
    gl = jnp.where(sub8 < N_EXPERT_GROUPS, lg[0:SUBLANES], -jnp.inf)
    gmax = jnp.max(gl, axis=0, keepdims=True)
    g_sel = jnp.min(jnp.where(gl == gmax, sub8, big), axis=0, keepdims=True)
    p_group = 1.0 / jnp.sum(jnp.exp(gl - gmax), axis=0, keepdims=True)
    el = lg[SUBLANES:2 * SUBLANES]
    for g in range(1, N_EXPERT_GROUPS):
        el = jnp.where(g_sel == g, lg[(g + 1) * SUBLANES:(g + 2) * SUBLANES], el)
    top1 = jnp.max(el, axis=0, keepdims=True)
    i1 = jnp.min(jnp.where(el == top1, sub8, big), axis=0, keepdims=True)
    el2 = jnp.where(sub8 == i1, -jnp.inf, el)
    top2 = jnp.max(el2, axis=0, keepdims=True)
    i2 = jnp.min(jnp.where(el2 == top2, sub8, big), axis=0, keepdims=True)
    ex = jnp.exp(top2 - top1)
    w1 = p_group * (1.0 / (1.0 + ex))
    w2 = p_group * (ex / (1.0 + ex))
    e1 = g_sel * EXPERTS_PER_GROUP + i1
    e2 = g_sel * EXPERTS_PER_GROUP + i2

    oh1 = (sube == e1)
    oh2 = (sube == e2)
    tri = tri_ref[...]
    cum1 = _dot(jnp.where(oh1, 1.0, 0.0).astype(BF16), tri)
    cum2 = _dot(jnp.where(oh2, 1.0, 0.0).astype(BF16), tri)
    tot1 = jnp.sum(jnp.where(oh1, 1.0, 0.0), axis=1, keepdims=True)
    tot2 = jnp.sum(jnp.where(oh2, 1.0, 0.0), axis=1, keepdims=True)
    base = base_ref[:, 0:1]
    r1 = jnp.sum(jnp.where(oh1, cum1 + base, 0.0), axis=0, keepdims=True)
    r2 = jnp.sum(jnp.where(oh2, cum2 + base + tot1, 0.0), axis=0, keepdims=True)
    base_ref[...] = jnp.broadcast_to(base + tot1 + tot2, base_ref.shape)
    keep_ref[:, pl.ds(off, tr)] = jnp.where(
        sub8 == 0, e1, jnp.where(sub8 == 1, e2, jnp.where(sub8 == 2, r1, jnp.where(
            sub8 == 3, r2, jnp.where(sub8 == 4, w1, jnp.where(sub8 == 5, w2, 0.0))))))


def _route(logits_t, tr, blk):
    rows, t = logits_t.shape
    assert t % tr == 0
    nt = t // tr
    tri = jnp.triu(jnp.ones((tr, tr), BF16), k=1)
    return pl.pallas_call(
        functools.partial(_route_kernel, tr=tr, nt=nt, blk=blk),
        grid=(nt + 1,),
        in_specs=[pl.BlockSpec((rows, tr), lambda i, _n=nt: (0, jnp.minimum(i, _n - 1))),
                  pl.BlockSpec((tr, tr), lambda i: (0, 0))],
        out_specs=[pl.BlockSpec((SUBLANES, t), lambda i: (0, 0)),
                   pl.BlockSpec((SUBLANES, t), lambda i: (0, 0)),
                   pl.BlockSpec((N_EXPERTS, LANES), lambda i: (0, 0))],
        out_shape=[jax.ShapeDtypeStruct((SUBLANES, t), jnp.int32),
                   jax.ShapeDtypeStruct((SUBLANES, t), F32),
                   jax.ShapeDtypeStruct((N_EXPERTS, LANES), jnp.int32)],
        scratch_shapes=[pltpu.VMEM((N_EXPERTS, LANES), F32), pltpu.VMEM((SUBLANES, t), F32)],
        compiler_params=pltpu.CompilerParams(
            dimension_semantics=("arbitrary",), vmem_limit_bytes=VMEM_LIMIT),
        name="route",
    )(logits_t, tri)


def _dispatch_kernel(meta_ref, dest_ref, x1_hbm, xs_hbm, xin, zbuf, lsem, sem, zsem, *,
                     tmd, nsteps, blk, nb):
    i = pl.program_id(0)

    def load_tile(j, slot):
        return pltpu.make_async_copy(x1_hbm.at[pl.ds(pl.multiple_of(j * tmd, tmd), tmd)],
                                     xin.at[slot], lsem.at[slot])

    def wait_step(slot):
        for _ in range(2):
            pltpu.make_async_copy(xin.at[slot], xin.at[slot], sem.at[slot]).wait()

    def pad_rows(e, row_fn, oct_fn):
        start = meta_ref[N_EXPERTS + e] + meta_ref[e]
        start8 = lax.shift_right_logical(start + (SUBLANES - 1), 3)
        lax.fori_loop(start, start8 * SUBLANES, row_fn, 0)
        lax.fori_loop(start8, lax.shift_right_logical(meta_ref[2 * N_EXPERTS + e], 3), oct_fn, 0)

    def zero_row(row):
        return pltpu.make_async_copy(zbuf.at[pl.ds(0, 1)], xs_hbm.at[pl.ds(row, 1)], zsem)

    def zero_oct(o):
        return pltpu.make_async_copy(
            zbuf.at[pl.ds(0, SUBLANES)],
            xs_hbm.at[pl.ds(pl.multiple_of(o * SUBLANES, SUBLANES), SUBLANES)], zsem)

    def zero_block(b):
        return pltpu.make_async_copy(zbuf, xs_hbm.at[pl.ds(pl.multiple_of(b * blk, blk), blk)], zsem)

    @pl.when(i == 0)
    def _():
        zbuf[...] = jnp.zeros_like(zbuf)
        n_used = lax.div(meta_ref[3 * N_EXPERTS - 1], blk)

        def start_row(row, c):
            zero_row(row).start()
            return c

        def wait_row(row, c):
            zero_row(row).wait()
            return c

        def start_oct(o, c):
            zero_oct(o).start()
            return c

        def wait_oct(o, c):
            zero_oct(o).wait()
            return c

        def start_blk(b, c):
            zero_block(b).start()
            return c

        def wait_blk(b, c):
            zero_block(b).wait()
            return c

        for e in range(N_EXPERTS):
            pad_rows(e, start_row, start_oct)
        lax.fori_loop(n_used, nb, start_blk, 0)
        for e in range(N_EXPERTS):
            pad_rows(e, wait_row, wait_oct)
        lax.fori_loop(n_used, nb, wait_blk, 0)

        load_tile(0, 0).start()
        if nsteps > 1:
            load_tile(1, 1).start()

    slot = lax.rem(i, DISPATCH_RING)
    load_tile(i, slot).wait()
    for r in range(tmd):
        for k in range(2):
            pltpu.make_async_copy(xin.at[slot, pl.ds(r, 1)],
                                  xs_hbm.at[pl.ds(dest_ref[0, 0, k * tmd + r], 1)],
                                  sem.at[slot]).start(priority=k)

    @pl.when(i > 0)
    def _():
        wait_step(lax.rem(i + DISPATCH_RING - 1, DISPATCH_RING))

    @pl.when(i + 2 < nsteps)
    def _():
        load_tile(i + 2, lax.rem(i + 2, DISPATCH_RING)).start()

    @pl.when(i == nsteps - 1)
    def _():
        wait_step(slot)


def _dispatch(rows, dest, meta_s, n_slots, tmd, blk):
    t, d = rows.shape
    dt = rows.dtype
    nsteps = t // tmd
    dest3 = dest.reshape(2, nsteps, tmd).transpose(1, 0, 2).reshape(nsteps, 1, 2 * tmd)
    grid_spec = pltpu.PrefetchScalarGridSpec(
        num_scalar_prefetch=1,
        grid=(nsteps,),
        in_specs=[pl.BlockSpec((1, 1, 2 * tmd), lambda i, m: (i, 0, 0), memory_space=pltpu.SMEM),
                  pl.BlockSpec(memory_space=pl.ANY)],
        out_specs=pl.BlockSpec(memory_space=pl.ANY),
        scratch_shapes=[pltpu.VMEM((DISPATCH_RING, tmd, d), dt), pltpu.VMEM((blk, d), dt),
                        pltpu.SemaphoreType.DMA((DISPATCH_RING,)),
                        pltpu.SemaphoreType.DMA((DISPATCH_RING,)), pltpu.SemaphoreType.DMA(())],
    )
    return pl.pallas_call(
        functools.partial(_dispatch_kernel, tmd=tmd, nsteps=nsteps, blk=blk, nb=n_slots // blk),
        grid_spec=grid_spec,
        out_shape=jax.ShapeDtypeStruct((n_slots, d), dt),
        compiler_params=pltpu.CompilerParams(dimension_semantics=("arbitrary",)),
        name="dispatch",
    )(meta_s, dest3, rows)


def _expert_kernel(be_ref, nused_ref, xs_hbm, wg_ref, wu_ref, wd_ref, ys_ref, wgb, wub, wdb,
                   xring, lsem):
    i = pl.program_id(0)
    nused = nused_ref[0]
    changed = jnp.logical_or(i == 0, be_ref[i] != be_ref[jnp.maximum(i - 1, 0)])
    blk = ys_ref.shape[0]
    slot = lax.rem(i, EXPERT_RING)

    def load_block(j):
        s = lax.rem(j, EXPERT_RING)
        return pltpu.make_async_copy(xs_hbm.at[pl.ds(pl.multiple_of(j * blk, blk), blk)],
                                     xring.at[s], lsem.at[s])

    @pl.when(i == 0)
    def _():
        for j in range(EXPERT_RING - 1):
            @pl.when(j < nused)
            def _():
                load_block(j).start()

    @pl.when(i + EXPERT_RING - 1 < nused)
    def _():
        load_block(i + EXPERT_RING - 1).start()

    @pl.when(jnp.logical_and(i < nused, changed))
    def _():
        wgb[...] = wg_ref[0].astype(BF16)
        wub[...] = wu_ref[0].astype(BF16)
        wdb[...] = wd_ref[0].astype(BF16)

    @pl.when(i < nused)
    def _():
        load_block(i).wait()
        cuts = (0, blk // 2, blk) if blk >= 2 * LANES else (0, blk)
        hd = xring.shape[2]

        def unpack(a, b):
            w = xring[slot, a:b, :]
            lo = lax.bitcast_convert_type(lax.shift_left(w, jnp.uint32(16)), F32)
            hi = lax.bitcast_convert_type(jnp.bitwise_and(w, jnp.uint32(0xFFFF0000)), F32)
            return lo.astype(BF16), hi.astype(BF16)

        def proj(h, w_ref):
            return _dot(h[0], w_ref[0:hd, :]) + _dot(h[1], w_ref[hd:2 * hd, :])

        def act(gu):
            return (_silu(gu[0]) * gu[1]).astype(BF16)

        hbs = [unpack(a, b) for a, b in zip(cuts[:-1], cuts[1:])]
        gus = [(proj(hb, wgb), proj(hb, wub)) for hb in hbs]
        for (a, b), gu in zip(zip(cuts[:-1], cuts[1:]), gus):
            ys_ref[a:b, :] = _dot(act(gu), wdb[...])

    @pl.when(i >= nused)
    def _():
        ys_ref[...] = jnp.zeros_like(ys_ref)


def _experts(xs, blk_expert, n_used, w_gate, w_up, w_down, blk):
    n_slots, hd = xs.shape
    d = 2 * hd
    nb = n_slots // blk
    de = w_gate.shape[-1]
    grid_spec = pltpu.PrefetchScalarGridSpec(
        num_scalar_prefetch=2,
        grid=(nb,),
        in_specs=[
            pl.BlockSpec(memory_space=pl.ANY),
            pl.BlockSpec((1, d, de), lambda i, be, nu: (be[i], 0, 0)),
            pl.BlockSpec((1, d, de), lambda i, be, nu: (be[i], 0, 0)),
            pl.BlockSpec((1, de, d), lambda i, be, nu: (be[i], 0, 0)),
        ],
        out_specs=pl.BlockSpec((blk, d), lambda i, be, nu: (i, 0)),
        scratch_shapes=[
            pltpu.VMEM((d, de), BF16),
            pltpu.VMEM((d, de), BF16),
            pltpu.VMEM((de, d), BF16),
            pltpu.VMEM((EXPERT_RING, blk, hd), xs.dtype),
            pltpu.SemaphoreType.DMA((EXPERT_RING,)),
        ],
    )
    return pl.pallas_call(
        _expert_kernel,
        grid_spec=grid_spec,
        out_shape=jax.ShapeDtypeStruct((n_slots, d), F32),
        compiler_params=pltpu.CompilerParams(
            dimension_semantics=("arbitrary",), vmem_limit_bytes=VMEM_LIMIT),
        name="experts",
    )(blk_expert, n_used, xs, w_gate, w_up, w_down)


def _combine_kernel(posc_ref, posn_ref, x1_ref, w_ref, gf_ref, ys_hbm, y_ref, *rest, tq, nsteps):
    bufs, sem = rest[:COMBINE_PARTS], rest[COMBINE_PARTS]
    i = pl.program_id(0)

    def gather(idx_ref, q):
        for r in range(2 * tq):
            pltpu.make_async_copy(ys_hbm.at[pl.ds(idx_ref[0, 0, q * 2 * tq + r], 1)],
                                  bufs[q].at[pl.ds(r, 1)], sem.at[q]).start(priority=r % 2)

    def wait(q):
        pltpu.make_async_copy(bufs[q], bufs[q], sem.at[q]).wait()

    def finish(q):
        rows = slice(q * tq, (q + 1) * tq)
        w = w_ref[rows, :]
        xo = x1_ref[rows, :] + w[:, 0:1] * bufs[q][0:tq, :] + w[:, 1:2] * bufs[q][tq:2 * tq, :]
        y_ref[rows, :] = (xo * lax.rsqrt(jnp.mean(xo * xo, axis=-1, keepdims=True) + EPS)
                          * gf_ref[...])

    @pl.when(i == 0)
    def _():
        for q in range(COMBINE_PARTS - 1):
            gather(posc_ref, q)

    for q in range(COMBINE_PARTS):
        wait(q)
        ahead = q + COMBINE_PARTS - 1
        if ahead < COMBINE_PARTS:
            gather(posc_ref, ahead)
        else:
            gather(posn_ref, ahead - COMBINE_PARTS)
        finish(q)

    @pl.when(i == nsteps - 1)
    def _():
        for q in range(COMBINE_PARTS - 1):
            wait(q)


def _combine(x1_flat, dest, w_col, ys, gf, tm):
    t, d = x1_flat.shape
    nsteps = t // tm
    tq = tm // COMBINE_PARTS
    pos3 = dest.reshape(2, nsteps, COMBINE_PARTS, tq).transpose(1, 2, 0, 3).reshape(nsteps, 1, 2 * tm)
    return pl.pallas_call(
        functools.partial(_combine_kernel, tq=tq, nsteps=nsteps),
        grid=(nsteps,),
        in_specs=[
            pl.BlockSpec((1, 1, 2 * tm), lambda i: (i, 0, 0), memory_space=pltpu.SMEM),
            pl.BlockSpec((1, 1, 2 * tm), lambda i, _n=nsteps: (jnp.minimum(i + 1, _n - 1), 0, 0),
                         memory_space=pltpu.SMEM),
            pl.BlockSpec((tm, d), lambda i: (i, 0)),
            pl.BlockSpec((tm, 2), lambda i: (i, 0)),
            pl.BlockSpec((1, d), lambda i: (0, 0)),
            pl.BlockSpec(memory_space=pl.ANY),
        ],
        out_specs=pl.BlockSpec((tm, d), lambda i: (i, 0)),
        out_shape=jax.ShapeDtypeStruct((t, d), F32),
        scratch_shapes=([pltpu.VMEM((2 * tq, d), F32) for _ in range(COMBINE_PARTS)]
                        + [pltpu.SemaphoreType.DMA((COMBINE_PARTS,))]),
        compiler_params=pltpu.CompilerParams(
            dimension_semantics=("arbitrary",), vmem_limit_bytes=VMEM_LIMIT),
        name="combine",
    )(pos3, pos3, x1_flat, w_col, gf, ys)


def _moe_and_norm(x1_flat, h2p_flat, logits_t, w_gate, w_up, w_down, gf):
    t, d = x1_flat.shape
    tr = _pick(t, (512, 256, 128))
    tm = _pick(t, (512, 256, 128))
    tmd = _pick(t, (512, 256, 128))
    blk = MOE_BLOCK if 2 * t >= 4 * N_EXPERTS * MOE_BLOCK else MOE_BLOCK_SMALL
    dest8, w8, meta = _route(logits_t, tr, blk)
    dest = dest8[0:2]
    nb = (2 * t + blk - 1) // blk + N_EXPERTS
    pend = meta[:, 2]
    blk_expert = jnp.minimum(
        jnp.sum(jnp.arange(nb, dtype=jnp.int32)[:, None] * blk >= pend[None, :], axis=1),
        N_EXPERTS - 1).astype(jnp.int32)
    n_used = (pend[-1:] // blk).astype(jnp.int32)
    xs = _dispatch(h2p_flat, dest, meta[:, 0:3].T.reshape(3 * N_EXPERTS), nb * blk, tmd, blk)
    ys = _experts(xs, blk_expert, n_used, w_gate, w_up, w_down, blk)
    return _combine(x1_flat, dest, w8[0:2].T, ys, gf, tm)


def _prep_layer_weights(norm1_g, w_in, conv_w, conv_b, dt_bias, a_log, d_skip, ssm_norm_g, ln_v_g,
                        ln_v_b, w_spatial, b_spatial, w_out, norm2_g, w_rg, b_rg, w_re, b_re):
    d = w_in.shape[0]
    conv_dim = SSD_WIDTH + 2 * SSD_GROUPS * SSD_STATE
    o_xbc = SSD_WIDTH
    o_dt = o_xbc + conv_dim
    o_gu = o_dt + SSD_HEADS
    o_gv = o_gu + GMLP_WIDTH
    w_dt = w_in[:, o_dt:o_gu]
    w_in_r = jnp.concatenate(
        [w_in[:, :o_xbc], w_in[:, o_xbc:o_dt], w_in[:, o_gu:o_gv], w_in[:, o_gv:],
         jnp.pad(w_dt, ((0, 0), (0, LANES - SSD_HEADS)))], axis=1).astype(BF16)
    rep = lambda v: jnp.repeat(v, SSD_HEAD_DIM)[None, :]
    col = lambda v: v[:, None]
    narrow = lambda v: jnp.pad(v, (0, LANES - SSD_HEADS))[None, :]
    sel = (jnp.arange(LANES)[:, None] == jnp.arange(SSD_WIDTH)[None, :] // SSD_HEAD_DIM).astype(BF16)
    w_r = jnp.concatenate([w_rg, jnp.zeros((d, SUBLANES - N_EXPERT_GROUPS), F32), w_re], axis=1).T
    b_r = jnp.concatenate([b_rg, jnp.zeros((SUBLANES - N_EXPERT_GROUPS,), F32), b_re])[:, None]
    w_r_hi = w_r.astype(BF16)
    w_r_lo = (w_r - w_r_hi.astype(F32)).astype(BF16)
    bsp = jnp.repeat(b_spatial.T, GMLP_GROUP_DIM, axis=1)
    return (norm1_g[None, :], w_in_r, w_dt.T.astype(BF16), conv_w, conv_b[None, :],
            narrow(dt_bias), narrow(a_log), sel, col(dt_bias), col(a_log), rep(d_skip),
            ssm_norm_g[None, :],
            ln_v_g[None, :], ln_v_b[None, :], w_spatial, bsp, w_out.astype(BF16), norm2_g[None, :],
            w_r_hi, w_r_lo, b_r)


def _state_to_pairs(s):
    b = s.shape[0]
    return s.reshape(b, N_PAIRS, 2, SSD_HEAD_DIM, SSD_STATE).transpose(0, 1, 4, 2, 3).reshape(
        b, N_PAIRS, SSD_STATE, 2 * SSD_HEAD_DIM)


def _pairs_to_state(s):
    b = s.shape[0]
    return s.reshape(b, N_PAIRS, SSD_STATE, 2, SSD_HEAD_DIM).transpose(0, 1, 3, 4, 2).reshape(
        b, SSD_HEADS, SSD_HEAD_DIM, SSD_STATE)


def _pick(n, prefs):
    for p in prefs:
        if n % p == 0:
            return p
    return n


def kernel(x_prompt, x_sample, cache_conv, state_ssm, norm1_g, w_in, conv_w, conv_b, dt_bias, a_log, d_skip, ssm_norm_g, ln_v_g, ln_v_b, w_spatial, b_spatial, w_out, norm2_g, w_router_group, b_router_group, w_router_expert, b_router_expert, w_gate, w_up, w_down, final_norm_g):
    depth = w_in.shape[0]
    assert depth == 1, "the combine kernel fuses the final norm, so only a single layer is supported"
    bp, lp, d = x_prompt.shape
    bs, ls, _ = x_sample.shape
    conv_dim = cache_conv.shape[-1]
    ls_pad = -(-ls // GMLP_CHUNK) * GMLP_CHUNK
    tl_p = _pick(lp, (256, 128))
    gf = final_norm_g[None, :]

    yp = x_prompt
    ys = jnp.pad(x_sample, ((0, 0), (0, ls_pad - ls), (0, 0)))
    conv_p, ssm_p, conv_s, ssm_s, v_s = [], [], [], [], []
    for i in range(depth):
        wts = _prep_layer_weights(
            norm1_g[i], w_in[i], conv_w[i], conv_b[i], dt_bias[i], a_log[i], d_skip[i], ssm_norm_g[i],
            ln_v_g[i], ln_v_b[i], w_spatial[i], b_spatial[i], w_out[i], norm2_g[i],
            w_router_group[i], b_router_group[i], w_router_expert[i], b_router_expert[i])
        gfi = gf

        conv0 = jnp.zeros((bp, SUBLANES, conv_dim), F32)
        ssm0 = jnp.zeros((bp, N_PAIRS, LANES, LANES), F32)
        x1, h2p, lt, cpo, hpo = _mixer(yp, conv0, ssm0, wts, tl=tl_p, l_valid=tl_p, emit_gv=False)
        tp = bp * lp
        outp = _moe_and_norm(x1.reshape(tp, d), h2p.reshape(tp, d // 2), lt, w_gate[i], w_up[i],
                             w_down[i], gfi)
        yp = outp.reshape(bp, lp, d)
        conv_p.append(cpo[:, SUBLANES - (CONV_WIDTH - 1):, :])
        ssm_p.append(_pairs_to_state(hpo))

        conv0 = jnp.pad(cache_conv[i], ((0, 0), (SUBLANES - (CONV_WIDTH - 1), 0), (0, 0)))
        x1, h2p, lt, cso, hso, gv = _mixer(ys, conv0, _state_to_pairs(state_ssm[i]), wts,
                                           tl=ls_pad, l_valid=ls, emit_gv=True)
        tsn = bs * ls
        x1v = x1[:, :ls].reshape(tsn, d)
        h2pv = h2p[:, :ls].reshape(tsn, d // 2)
        ltv = lt.reshape(ROUTER_ROWS, bs, ls_pad)[:, :, :ls].reshape(ROUTER_ROWS, tsn)
        outs = _moe_and_norm(x1v, h2pv, ltv, w_gate[i], w_up[i], w_down[i], gfi)
        ys = outs.reshape(bs, ls, d)
        conv_s.append(cso[:, SUBLANES - (CONV_WIDTH - 1):, :])
        ssm_s.append(_pairs_to_state(hso))
        v_s.append(gv[:, :ls])
    return (yp, ys, jnp.stack(conv_p), jnp.stack(ssm_p), jnp.stack(conv_s), jnp.stack(ssm_s),
            jnp.stack(v_s))
```

```python
import functools

import jax
import jax.numpy as jnp
from jax import lax
from jax.experimental import pallas as pl
from jax.experimental.pallas import tpu as pltpu

F32 = jnp.float32
BF16 = jnp.bfloat16
EPS = 1e-6

SSD_HEADS = 8
SSD_HEAD_DIM = 64
SSD_STATE = 128
SSD_GROUPS = 2
SSD_WIDTH = SSD_HEADS * SSD_HEAD_DIM
SSD_CHUNK = 64
CONV_WIDTH = 4
GMLP_GROUPS = 8
GMLP_GROUP_DIM = 64
GMLP_WIDTH = GMLP_GROUPS * GMLP_GROUP_DIM
GMLP_CHUNK = 128
N_EXPERT_GROUPS = 4
EXPERTS_PER_GROUP = 8
N_EXPERTS = N_EXPERT_GROUPS * EXPERTS_PER_GROUP
ROUTER_ROWS = 8 + N_EXPERTS
MOE_BLOCK = 512
MOE_BLOCK_SMALL = 32
DISPATCH_RING = 3
EXPERT_RING = 3
COMBINE_PARTS = 4

LANES = 128
SUBLANES = 8
N_PAIRS = SSD_HEADS // 2
VMEM_LIMIT = 56 * 1024 * 1024

C_Z = 0
C_XBC = C_Z + SSD_WIDTH
C_GU = C_XBC + SSD_WIDTH + 2 * SSD_GROUPS * SSD_STATE
C_GV = C_GU + GMLP_WIDTH
C_DT = C_GV + GMLP_WIDTH
C_END = C_DT + LANES
PROJ_SLAB = 256


def _silu(v):
    return v * (1.0 / (1.0 + jnp.exp(-v)))


def _gelu_tanh(v):
    c = 0.7978845608028654
    hv = 0.5 * v
    return hv + hv * jnp.tanh(v * (c + (c * 0.044715) * (v * v)))


def _softplus(v):
    return jnp.maximum(v, 0.0) + jnp.log1p(jnp.exp(-jnp.abs(v)))


def _dot(a, b):
    return jnp.dot(a, b, preferred_element_type=F32)


def _dot_nt(a, b):
    return lax.dot_general(a, b, (((1,), (1,)), ((), ())), preferred_element_type=F32)


def _dot_tn(a, b):
    return lax.dot_general(a, b, (((0,), (0,)), ((), ())), preferred_element_type=F32)


def _pack_bf16_pairs(v):
    bits = lax.bitcast_convert_type(v, jnp.uint32)
    hw = bits.shape[1] // 2
    return jnp.bitwise_or(lax.shift_right_logical(bits[:, :hw], jnp.uint32(16)),
                          jnp.bitwise_and(bits[:, hw:], jnp.uint32(0xFFFF0000)))


def _unpack_bf16_pairs(w):
    lo = lax.bitcast_convert_type(lax.shift_left(w, jnp.uint32(16)), F32)
    hi = lax.bitcast_convert_type(jnp.bitwise_and(w, jnp.uint32(0xFFFF0000)), F32)
    return lo, hi


def _replicate_heads(v, sel_ref):
    hi = v.astype(BF16)
    r1 = v - hi.astype(F32)
    mid = r1.astype(BF16)
    lo = (r1 - mid.astype(F32)).astype(BF16)
    sel = sel_ref[...]
    return _dot(hi, sel) + _dot(mid, sel) + _dot(lo, sel)


def _mixer_kernel(xb_ref, xf_ref, conv0a_ref, conv0b_ref, ssm0a_ref, ssm0b_ref,
                  g1_ref, win_ref, wdt_ref, convw_ref, convb_ref,
                  dtb_n_ref, alog_n_ref, sel_ref, dtb_col_ref, alog_col_ref, dskip_ref, ssmg_ref,
                  lng_ref, lnb_ref, wsp_ref, bsp_ref, wout_ref, g2_ref, wr_hi_ref, wr_lo_ref,
                  br_ref,
                  x1_ref, h2p_ref, logit_ref, convo_ref, ssmo_ref, *rest,
                  tl, nt, l_valid, emit_gv):
    if emit_gv:
        gvo_ref = rest[0]
        rest = rest[1:]
    else:
        gvo_ref = None
    (pa_ref, pb_ref, da_ref, db_ref, xpad_ref, state_ref, wspm_ref, zs_ref, gus_ref, gvs_ref,
     gvf_ref, dtc_ref, acs_ref, dtt_ref) = rest
    s = pl.program_id(0)

    def drain_pieces(pbuf, dbuf, v):
        def p_xbc():
            xpad_ref[v, SUBLANES:SUBLANES + tl, :] = pbuf[:, C_XBC:C_GU]

        def p_dt():
            dtn = _softplus(pbuf[:, C_DT:C_END] + dtb_n_ref[...])
            row = lax.broadcasted_iota(jnp.int32, (tl, LANES), 0)
            if l_valid < tl:
                dtn = jnp.where(row < l_valid, dtn, 0.0)
            acsn = dtn * (-jnp.exp(alog_n_ref[...]))
            row_in_chunk = jnp.bitwise_and(row, SSD_CHUNK - 1)
            sh = 1
            while sh < SSD_CHUNK:
                acsn = acsn + jnp.where(row_in_chunk >= sh, pltpu.roll(acsn, sh, axis=0), 0.0)
                sh *= 2
            dtc_ref[v] = _replicate_heads(dtn, sel_ref)
            acs_ref[v] = _replicate_heads(acsn, sel_ref)
            dtt_ref[v] = _softplus(dbuf[...] + dtb_col_ref[...])

        def p_z():
            zs_ref[v] = _silu(pbuf[:, C_Z:C_XBC])

        def p_gu():
            gus_ref[v] = _gelu_tanh(pbuf[:, C_GU:C_GV])

        def p_gv():
            gv = _gelu_tanh(pbuf[:, C_GV:C_DT])
            mu = jnp.mean(gv, axis=-1, keepdims=True)
            gvc = gv - mu
            var = jnp.mean(gvc * gvc, axis=-1, keepdims=True)
            gv = gvc * lax.rsqrt(var + EPS) * lng_ref[...] + lnb_ref[...]
            if gvo_ref is not None:
                gvf_ref[v] = gv
            gvs_ref[v] = gv.astype(BF16)
        return [p_xbc, p_dt, p_z, p_gu, p_gv]

    def front_pieces(x, pbuf, dbuf):
        hb = []

        def norm():
            h = x * lax.rsqrt(jnp.mean(x * x, axis=-1, keepdims=True) + EPS) * g1_ref[...]
            hb.append(h.astype(BF16))
            dbuf[...] = _dot_nt(wdt_ref[...], hb[0])

        def slab(c0):
            c1 = min(c0 + PROJ_SLAB, C_END)

            def run():
                pbuf[:, c0:c1] = _dot(hb[0], win_ref[:, c0:c1])
            return run
        return [norm] + [slab(c0) for c0 in range(0, C_END, PROJ_SLAB)]

    def front(x, pbuf, dbuf):
        for piece in front_pieces(x, pbuf, dbuf):
            piece()

    @pl.when(s == 0)
    def _():
        r = lax.broadcasted_iota(jnp.int32, (GMLP_CHUNK, GMLP_CHUNK), 0)
        c = lax.broadcasted_iota(jnp.int32, (GMLP_CHUNK, GMLP_CHUNK), 1)
        for gi in range(GMLP_GROUPS):
            wspm_ref[gi] = jnp.where(r >= c, wsp_ref[gi], 0.0).astype(BF16)
        for v in range(2):
            xpad_ref[v, 0:SUBLANES, :] = jnp.zeros((SUBLANES, xpad_ref.shape[2]), F32)
        state_ref[...] = jnp.zeros_like(state_ref)
        front(xb_ref[0], pa_ref, da_ref)
        front(xb_ref[1], pb_ref, db_ref)
        for piece in drain_pieces(pa_ref, da_ref, 0):
            piece()

    bufs = ((pa_ref, da_ref, conv0a_ref, ssm0a_ref), (pb_ref, db_ref, conv0b_ref, ssm0b_ref))
    for u, (pbuf, dbuf, conv0_ref, ssm0_ref) in enumerate(bufs):
        is_first = lax.rem(2 * s + u, nt) == 0
        xpad_ref[u, 0:SUBLANES, :] = jnp.where(is_first, conv0_ref[0], xpad_ref[u, 0:SUBLANES, :])
        for j in range(N_PAIRS):
            state_ref[j] = jnp.where(is_first, ssm0_ref[0, j], state_ref[j])
        if gvo_ref is not None:
            gvo_ref[u] = gvf_ref[u]

        nxt = bufs[1 - u]
        _mixer_back(
            xb_ref[u], front_pieces(xf_ref[u], pbuf, dbuf), drain_pieces(nxt[0], nxt[1], 1 - u),
            convw_ref=convw_ref,
            convb_ref=convb_ref, alog_col_ref=alog_col_ref, dskip_ref=dskip_ref,
            ssmg_ref=ssmg_ref, bsp_ref=bsp_ref, wout_ref=wout_ref, g2_ref=g2_ref,
            wr_hi_ref=wr_hi_ref, wr_lo_ref=wr_lo_ref, br_ref=br_ref, x1_ref=x1_ref,
            h2p_ref=h2p_ref, logit_ref=logit_ref, convo_ref=convo_ref, ssmo_ref=ssmo_ref, xpad_ref=xpad_ref,
            state_ref=state_ref, wspm_ref=wspm_ref, zs_ref=zs_ref, gus_ref=gus_ref,
            gvs_ref=gvs_ref, dtc_ref=dtc_ref, acs_ref=acs_ref, dtt_ref=dtt_ref,
            u=u, tl=tl, l_valid=l_valid)


def _mixer_back(x, fillers, vfillers, *, convw_ref, convb_ref, alog_col_ref, dskip_ref,
                ssmg_ref, bsp_ref, wout_ref, g2_ref, wr_hi_ref, wr_lo_ref, br_ref, x1_ref, h2p_ref,
                logit_ref, convo_ref, ssmo_ref, xpad_ref, state_ref, wspm_ref, zs_ref, gus_ref,
                gvs_ref, dtc_ref, acs_ref, dtt_ref, u, tl, l_valid):
    fillers = list(fillers)
    vfillers = list(vfillers)
    dt = dtc_ref[u]
    acs = acs_ref[u]
    dtt = dtt_ref[u]

    def fill(n):
        for _ in range(min(n, len(fillers))):
            fillers.pop(0)()

    def vfill(n):
        for _ in range(min(n, len(vfillers))):
            vfillers.pop(0)()

    fill(1)
    conv = convb_ref[...]
    for k in range(CONV_WIDTH):
        off = SUBLANES - (CONV_WIDTH - 1) + k
        conv = conv + xpad_ref[u, off:off + tl, :] * convw_ref[k:k + 1, :]
        fill(1)
    xbc = _silu(conv)
    carry = xpad_ref[u, l_valid:l_valid + SUBLANES, :]
    xpad_ref[1 - u, 0:SUBLANES, :] = carry
    convo_ref[u] = carry

    xs = xbc[:, 0:SSD_WIDTH]
    bm = xbc[:, SSD_WIDTH:SSD_WIDTH + SSD_GROUPS * SSD_STATE].astype(BF16)
    cm = xbc[:, SSD_WIDTH + SSD_GROUPS * SSD_STATE:].astype(BF16)

    lane_t = lax.broadcasted_iota(jnp.int32, (SSD_HEADS, tl), 1)
    if l_valid < tl:
        dtt = jnp.where(lane_t < l_valid, dtt, 0.0)
    acst = dtt * (-jnp.exp(alog_col_ref[...]))
    lane_in_chunk = jnp.bitwise_and(lane_t, SSD_CHUNK - 1)
    sh = 1
    while sh < SSD_CHUNK:
        acst = acst + jnp.where(lane_in_chunk >= sh, pltpu.roll(acst, sh, axis=1), 0.0)
        sh *= 2

    lane = lax.broadcasted_iota(jnp.int32, (SSD_CHUNK, LANES), 1)
    rowc = lax.broadcasted_iota(jnp.int32, (SSD_CHUNK, LANES), 0)
    lo_half = lane < SSD_HEAD_DIM
    causal = rowc >= jnp.bitwise_and(lane, SSD_CHUNK - 1)
    lane1 = lax.broadcasted_iota(jnp.int32, (1, LANES), 1)
    lo_half1 = lane1 < SSD_CHUNK

    y_chunks = []
    for c in range(tl // SSD_CHUNK):
        r0 = c * SSD_CHUNK
        v = acst[:, (c // 2) * LANES:(c // 2 + 1) * LANES]
        vr = pltpu.roll(v, SSD_CHUNK, axis=1)
        v_lo, v_hi = (v, vr) if c % 2 == 0 else (vr, v)
        cb2 = []
        for g in range(SSD_GROUPS):
            cg = cm[r0:r0 + SSD_CHUNK, g * SSD_STATE:(g + 1) * SSD_STATE]
            bg = bm[r0:r0 + SSD_CHUNK, g * SSD_STATE:(g + 1) * SSD_STATE]
            cb2.append(_dot_nt(cg, jnp.concatenate([bg, bg], axis=0)))
        y_pairs = []
        for j in range(N_PAIRS):
            g = j // (N_PAIRS // SSD_GROUPS)
            cg = cm[r0:r0 + SSD_CHUNK, g * SSD_STATE:(g + 1) * SSD_STATE]
            bg = bm[r0:r0 + SSD_CHUNK, g * SSD_STATE:(g + 1) * SSD_STATE]
            sl = slice(j * LANES, (j + 1) * LANES)
            col_a = acs[r0:r0 + SSD_CHUNK, sl]
            row_a = jnp.where(lo_half1, v_lo[2 * j:2 * j + 1, :], v_hi[2 * j + 1:2 * j + 2, :])
            decay = jnp.where(causal, jnp.exp(col_a - row_a), 0.0)
            m = (cb2[g] * decay).astype(BF16)
            xdt = xs[r0:r0 + SSD_CHUNK, sl] * dt[r0:r0 + SSD_CHUNK, sl]
            zbd = jnp.concatenate([jnp.where(lo_half, xdt, 0.0), jnp.where(lo_half, 0.0, xdt)],
                                  axis=0).astype(BF16)
            y_diag = _dot(m, zbd)
            st = state_ref[j]
            y_off = _dot(cg, st.astype(BF16)) * jnp.exp(col_a)
            a_last = acs[r0 + SSD_CHUNK - 1:r0 + SSD_CHUNK, sl]
            zdte = (xdt * jnp.exp(a_last - col_a)).astype(BF16)
            state_ref[j] = st * jnp.exp(a_last) + _dot_tn(bg, zdte)
            y_pairs.append(y_diag + y_off)
        y_chunks.append(jnp.concatenate(y_pairs, axis=1))
        fill(1)
    y = jnp.concatenate(y_chunks, axis=0) if len(y_chunks) > 1 else y_chunks[0]
    ssmo_ref[u] = state_ref[...]

    y = y + xs * dskip_ref[...]
    gated = y * zs_ref[u]
    half = SSD_WIDTH // SSD_GROUPS
    outs = []
    for g in range(SSD_GROUPS):
        gg = gated[:, g * half:(g + 1) * half]
        outs.append(gg * lax.rsqrt(jnp.mean(gg * gg, axis=-1, keepdims=True) + EPS))
    ssd_out = jnp.concatenate(outs, axis=1) * ssmg_ref[...]
    fill(1)

    lane_g = lax.broadcasted_iota(jnp.int32, (GMLP_CHUNK, LANES), 1)
    lo_g = lane_g < GMLP_GROUP_DIM
    mixed_chunks = []
    for q in range(tl // GMLP_CHUNK):
        mixed_pairs = []
        for j in range(GMLP_GROUPS // 2):
            vp = gvs_ref[u, q * GMLP_CHUNK:(q + 1) * GMLP_CHUNK, j * LANES:(j + 1) * LANES]
            r_even = _dot(wspm_ref[2 * j], vp)
            r_odd = _dot(wspm_ref[2 * j + 1], vp)
            mixed_pairs.append(jnp.where(lo_g, r_even, r_odd))
        mixed_chunks.append(jnp.concatenate(mixed_pairs, axis=1) + bsp_ref[...])
    mixed = jnp.concatenate(mixed_chunks, axis=0) if len(mixed_chunks) > 1 else mixed_chunks[0]
    gmlp_out = gus_ref[u] * mixed

    merged = jnp.concatenate([ssd_out, gmlp_out], axis=1).astype(BF16)
    x1 = x + _dot(merged, wout_ref[...])
    x1_ref[u] = x1
    fill(len(fillers))

    h2 = x1 * lax.rsqrt(jnp.mean(x1 * x1, axis=-1, keepdims=True) + EPS) * g2_ref[...]
    h2_hi = h2.astype(BF16)
    h2_hif = h2_hi.astype(F32)
    h2_lo = (h2 - h2_hif).astype(BF16)
    h2p_ref[u] = _pack_bf16_pairs(h2_hif)
    logit_ref[:, u * tl:(u + 1) * tl] = (
        _dot_nt(wr_hi_ref[...], h2_hi) + _dot_nt(wr_lo_ref[...], h2_hi)
        + _dot_nt(wr_hi_ref[...], h2_lo) + br_ref[...])
    vfill(len(vfillers))


def _full_spec(shape):
    nd = len(shape)
    return pl.BlockSpec(shape, lambda s, _nd=nd: (0,) * _nd)


def _mixer(x, conv0, ssm0, wts, *, tl, l_valid, emit_gv):
    bsz, l, d = x.shape
    nt = l // tl
    g = bsz * nt
    assert l % tl == 0 and tl % GMLP_CHUNK == 0 and l_valid % SUBLANES == 0
    assert (nt == 1 or l_valid == tl) and g % 2 == 0
    steps = g // 2
    tile = lambda f: (lambda s: (f(s), 0, 0))
    seq = lambda f: (lambda s: (f(s) // nt,) + (0,) * 2)
    seq4 = lambda f: (lambda s: (f(s) // nt,) + (0,) * 3)
    in_specs = [
        pl.BlockSpec((2, tl, d), tile(lambda s: s)),
        pl.BlockSpec((2, tl, d), tile(lambda s: jnp.minimum(s + 1, steps - 1))),
        pl.BlockSpec((1, SUBLANES, d), seq(lambda s: 2 * s)),
        pl.BlockSpec((1, SUBLANES, d), seq(lambda s: 2 * s + 1)),
        pl.BlockSpec((1, N_PAIRS, LANES, LANES), seq4(lambda s: 2 * s)),
        pl.BlockSpec((1, N_PAIRS, LANES, LANES), seq4(lambda s: 2 * s + 1)),
    ] + [_full_spec(w.shape) for w in wts]
    out_shape = [
        jax.ShapeDtypeStruct((g, tl, d), F32),
        jax.ShapeDtypeStruct((g, tl, d // 2), jnp.uint32),
        jax.ShapeDtypeStruct((ROUTER_ROWS, g * tl), F32),
        jax.ShapeDtypeStruct((g, SUBLANES, d), F32),
        jax.ShapeDtypeStruct((g, N_PAIRS, LANES, LANES), F32),
    ]
    out_specs = [
        pl.BlockSpec((2, tl, d), lambda s: (s, 0, 0)),
        pl.BlockSpec((2, tl, d // 2), lambda s: (s, 0, 0)),
        pl.BlockSpec((ROUTER_ROWS, 2 * tl), lambda s: (0, s)),
        pl.BlockSpec((2, SUBLANES, d), lambda s: (s, 0, 0)),
        pl.BlockSpec((2, N_PAIRS, LANES, LANES), lambda s: (s, 0, 0, 0)),
    ]
    if emit_gv:
        out_shape.append(jax.ShapeDtypeStruct((g, tl, GMLP_WIDTH), F32))
        out_specs.append(pl.BlockSpec((2, tl, GMLP_WIDTH), lambda s: (s, 0, 0)))
    xt = x.reshape(g, tl, d)
    outs = pl.pallas_call(
        functools.partial(_mixer_kernel, tl=tl, nt=nt, l_valid=l_valid, emit_gv=emit_gv),
        grid=(steps,),
        in_specs=in_specs,
        out_specs=out_specs,
        out_shape=out_shape,
        scratch_shapes=[
            pltpu.VMEM((tl, C_END), F32),
            pltpu.VMEM((tl, C_END), F32),
            pltpu.VMEM((SSD_HEADS, tl), F32),
            pltpu.VMEM((SSD_HEADS, tl), F32),
            pltpu.VMEM((2, tl + SUBLANES, d), F32),
            pltpu.VMEM((N_PAIRS, LANES, LANES), F32),
            pltpu.VMEM((GMLP_GROUPS, GMLP_CHUNK, GMLP_CHUNK), BF16),
            pltpu.VMEM((2, tl, SSD_WIDTH), F32),
            pltpu.VMEM((2, tl, GMLP_WIDTH), F32),
            pltpu.VMEM((2, tl, GMLP_WIDTH), BF16),
            pltpu.VMEM((2, tl, GMLP_WIDTH), F32),
            pltpu.VMEM((2, tl, SSD_WIDTH), F32),
            pltpu.VMEM((2, tl, SSD_WIDTH), F32),
            pltpu.VMEM((2, SSD_HEADS, tl), F32),
        ],
        compiler_params=pltpu.CompilerParams(
            dimension_semantics=("arbitrary",), vmem_limit_bytes=VMEM_LIMIT),
        name="mixer",
    )(xt, xt, conv0, conv0, ssm0, ssm0, *wts)
    x1, h2p, lt, cvo, sso = outs[:5]
    last = slice(nt - 1, None, nt)
    res = (x1.reshape(bsz, l, d), h2p.reshape(bsz, l, d // 2), lt, cvo[last], sso[last])
    if emit_gv:
        res += (outs[5].reshape(bsz, l, GMLP_WIDTH),)
    return res


def _route_kernel(logit_ref, tri_ref, dest_ref, w_ref, meta_ref, base_ref, keep_ref, *, tr, nt, blk):
    i = pl.program_id(0)
    sub8 = lax.broadcasted_iota(jnp.int32, (SUBLANES, tr), 0).astype(F32)
    sube = lax.broadcasted_iota(jnp.int32, (N_EXPERTS, tr), 0).astype(F32)

    @pl.when(i == 0)
    def _():
        base_ref[...] = jnp.zeros_like(base_ref)

    @pl.when(i < nt)
    def _():
        _route_pass0(logit_ref, tri_ref, base_ref, keep_ref, pl.multiple_of(i * tr, tr), sub8, sube, tr)

    @pl.when(i == nt)
    def _():
        counts = base_ref[...]
        padded = jnp.floor((counts + float(blk - 1)) * (1.0 / blk)) * float(blk)
        sub_e = lax.broadcasted_iota(jnp.int32, (N_EXPERTS, LANES), 0)
        pend = padded
        sh = 1
        while sh < N_EXPERTS:
            pend = pend + jnp.where(sub_e >= sh, pltpu.roll(pend, sh, axis=0), 0.0)
            sh *= 2
        pstart = pend - padded
        lane_e = lax.broadcasted_iota(jnp.int32, (N_EXPERTS, LANES), 1)
        meta = jnp.where(lane_e == 0, counts, jnp.where(lane_e == 1, pstart,
                         jnp.where(lane_e == 2, pend, 0.0)))
        meta_ref[...] = meta.astype(jnp.int32)

        def chunk(c, carry):
            sl = pl.ds(pl.multiple_of(c * tr, tr), tr)
            kept = keep_ref[:, sl]
            ps1 = jnp.sum(jnp.where(sube == kept[0:1], pstart[:, 0:1], 0.0), axis=0, keepdims=True)
            ps2 = jnp.sum(jnp.where(sube == kept[1:2], pstart[:, 0:1], 0.0), axis=0, keepdims=True)
            dest = jnp.where(sub8 == 0, ps1 + kept[2:3],
                             jnp.where(sub8 == 1, ps2 + kept[3:4], 0.0))
            dest_ref[:, sl] = dest.astype(jnp.int32)
            w_ref[:, sl] = jnp.where(sub8 == 0, kept[4:5], jnp.where(sub8 == 1, kept[5:6], 0.0))
            return carry
        lax.fori_loop(0, nt, chunk, 0)


def _route_pass0(logit_ref, tri_ref, base_ref, keep_ref, off, sub8, sube, tr):
    lg = logit_ref[...]
    big = float(SUBLANES)
    gl = jnp.where(sub8 < N_EXPERT_GROUPS, lg[0:SUBLANES], -jnp.inf)
    gmax = jnp.max(gl, axis=0, keepdims=True)
    g_sel = jnp.min(jnp.where(gl == gmax, sub8, big), axis=0, keepdims=True)
    p_group = 1.0 / jnp.sum(jnp.exp(gl - gmax), axis=0, keepdims=True)
    el = lg[SUBLANES:2 * SUBLANES]
    for g in range(1, N_EXPERT_GROUPS):
        el = jnp.where(g_sel == g, lg[(g + 1) * SUBLANES:(g + 2) * SUBLANES], el)
    top1 = jnp.max(el, axis=0, keepdims=True)
    i1 = jnp.min(jnp.where(el == top1, sub8, big), axis=0, keepdims=True)
    el2 = jnp.where(sub8 == i1, -jnp.inf, el)
    top2 = jnp.max(el2, axis=0, keepdims=True)
    i2 = jnp.min(jnp.where(el2 == top2, sub8, big), axis=0, keepdims=True)
    ex = jnp.exp(top2 - top1)
    w1 = p_group * (1.0 / (1.0 + ex))
    w2 = p_group * (ex / (1.0 + ex))
    e1 = g_sel * EXPERTS_PER_GROUP + i1
    e2 = g_sel * EXPERTS_PER_GROUP + i2

    oh1 = (sube == e1)
    oh2 = (sube == e2)
    tri = tri_ref[...]
    cum1 = _dot(jnp.where(oh1, 1.0, 0.0).astype(BF16), tri)
    cum2 = _dot(jnp.where(oh2, 1.0, 0.0).astype(BF16), tri)
    tot1 = jnp.sum(jnp.where(oh1, 1.0, 0.0), axis=1, keepdims=True)
    tot2 = jnp.sum(jnp.where(oh2, 1.0, 0.0), axis=1, keepdims=True)
    base = base_ref[:, 0:1]
    r1 = jnp.sum(jnp.where(oh1, cum1 + base, 0.0), axis=0, keepdims=True)
    r2 = jnp.sum(jnp.where(oh2, cum2 + base + tot1, 0.0), axis=0, keepdims=True)
    base_ref[...] = jnp.broadcast_to(base + tot1 + tot2, base_ref.shape)
    keep_ref[:, pl.ds(off, tr)] = jnp.where(
        sub8 == 0, e1, jnp.where(sub8 == 1, e2, jnp.where(sub8 == 2, r1, jnp.where(
            sub8 == 3, r2, jnp.where(sub8 == 4, w1, jnp.where(sub8 == 5, w2, 0.0))))))


def _route(logits_t, tr, blk):
    rows, t = logits_t.shape
    assert t % tr == 0
    nt = t // tr
    tri = jnp.triu(jnp.ones((tr, tr), BF16), k=1)
    return pl.pallas_call(
        functools.partial(_route_kernel, tr=tr, nt=nt, blk=blk),
        grid=(nt + 1,),
        in_specs=[pl.BlockSpec((rows, tr), lambda i, _n=nt: (0, jnp.minimum(i, _n - 1))),
                  pl.BlockSpec((tr, tr), lambda i: (0, 0))],
        out_specs=[pl.BlockSpec((SUBLANES, t), lambda i: (0, 0)),
                   pl.BlockSpec((SUBLANES, t), lambda i: (0, 0)),
                   pl.BlockSpec((N_EXPERTS, LANES), lambda i: (0, 0))],
        out_shape=[jax.ShapeDtypeStruct((SUBLANES, t), jnp.int32),
                   jax.ShapeDtypeStruct((SUBLANES, t), F32),
                   jax.ShapeDtypeStruct((N_EXPERTS, LANES), jnp.int32)],
        scratch_shapes=[pltpu.VMEM((N_EXPERTS, LANES), F32), pltpu.VMEM((SUBLANES, t), F32)],
        compiler_params=pltpu.CompilerParams(
            dimension_semantics=("arbitrary",), vmem_limit_bytes=VMEM_LIMIT),
        name="route",
    )(logits_t, tri)


def _dispatch_kernel(meta_ref, dest_ref, x1_hbm, xs_hbm, xin, zbuf, lsem, sem, zsem, *,
                     tmd, nsteps, blk, nb):
    i = pl.program_id(0)

    def load_tile(j, slot):
        return pltpu.make_async_copy(x1_hbm.at[pl.ds(pl.multiple_of(j * tmd, tmd), tmd)],
                                     xin.at[slot], lsem.at[slot])

    def wait_step(slot):
        for _ in range(2):
            pltpu.make_async_copy(xin.at[slot], xin.at[slot], sem.at[slot]).wait()

    def pad_rows(e, row_fn, oct_fn):
        start = meta_ref[N_EXPERTS + e] + meta_ref[e]
        start8 = lax.shift_right_logical(start + (SUBLANES - 1), 3)
        lax.fori_loop(start, start8 * SUBLANES, row_fn, 0)
        lax.fori_loop(start8, lax.shift_right_logical(meta_ref[2 * N_EXPERTS + e], 3), oct_fn, 0)

    def zero_row(row):
        return pltpu.make_async_copy(zbuf.at[pl.ds(0, 1)], xs_hbm.at[pl.ds(row, 1)], zsem)

    def zero_oct(o):
        return pltpu.make_async_copy(
            zbuf.at[pl.ds(0, SUBLANES)],
            xs_hbm.at[pl.ds(pl.multiple_of(o * SUBLANES, SUBLANES), SUBLANES)], zsem)

    def zero_block(b):
        return pltpu.make_async_copy(zbuf, xs_hbm.at[pl.ds(pl.multiple_of(b * blk, blk), blk)], zsem)

    @pl.when(i == 0)
    def _():
        zbuf[...] = jnp.zeros_like(zbuf)
        n_used = lax.div(meta_ref[3 * N_EXPERTS - 1], blk)

        def start_row(row, c):
            zero_row(row).start()
            return c

        def wait_row(row, c):
            zero_row(row).wait()
            return c

        def start_oct(o, c):
            zero_oct(o).start()
            return c

        def wait_oct(o, c):
            zero_oct(o).wait()
            return c

        def start_blk(b, c):
            zero_block(b).start()
            return c

        def wait_blk(b, c):
            zero_block(b).wait()
            return c

        for e in range(N_EXPERTS):
            pad_rows(e, start_row, start_oct)
        lax.fori_loop(n_used, nb, start_blk, 0)
        for e in range(N_EXPERTS):
            pad_rows(e, wait_row, wait_oct)
        lax.fori_loop(n_used, nb, wait_blk, 0)

        load_tile(0, 0).start()
        if nsteps > 1:
            load_tile(1, 1).start()

    slot = lax.rem(i, DISPATCH_RING)
    load_tile(i, slot).wait()
    for r in range(tmd):
        for k in range(2):
            pltpu.make_async_copy(xin.at[slot, pl.ds(r, 1)],
                                  xs_hbm.at[pl.ds(dest_ref[0, 0, k * tmd + r], 1)],
                                  sem.at[slot]).start(priority=k)

    @pl.when(i > 0)
    def _():
        wait_step(lax.rem(i + DISPATCH_RING - 1, DISPATCH_RING))

    @pl.when(i + 2 < nsteps)
    def _():
        load_tile(i + 2, lax.rem(i + 2, DISPATCH_RING)).start()

    @pl.when(i == nsteps - 1)
    def _():
        wait_step(slot)


def _dispatch(rows, dest, meta_s, n_slots, tmd, blk):
    t, d = rows.shape
    dt = rows.dtype
    nsteps = t // tmd
    dest3 = dest.reshape(2, nsteps, tmd).transpose(1, 0, 2).reshape(nsteps, 1, 2 * tmd)
    grid_spec = pltpu.PrefetchScalarGridSpec(
        num_scalar_prefetch=1,
        grid=(nsteps,),
        in_specs=[pl.BlockSpec((1, 1, 2 * tmd), lambda i, m: (i, 0, 0), memory_space=pltpu.SMEM),
                  pl.BlockSpec(memory_space=pl.ANY)],
        out_specs=pl.BlockSpec(memory_space=pl.ANY),
        scratch_shapes=[pltpu.VMEM((DISPATCH_RING, tmd, d), dt), pltpu.VMEM((blk, d), dt),
                        pltpu.SemaphoreType.DMA((DISPATCH_RING,)),
                        pltpu.SemaphoreType.DMA((DISPATCH_RING,)), pltpu.SemaphoreType.DMA(())],
    )
    return pl.pallas_call(
        functools.partial(_dispatch_kernel, tmd=tmd, nsteps=nsteps, blk=blk, nb=n_slots // blk),
        grid_spec=grid_spec,
        out_shape=jax.ShapeDtypeStruct((n_slots, d), dt),
        compiler_params=pltpu.CompilerParams(dimension_semantics=("arbitrary",)),
        name="dispatch",
    )(meta_s, dest3, rows)


def _expert_kernel(be_ref, nused_ref, xs_hbm, wg_ref, wu_ref, wd_ref, ys_ref, wgb, wub, wdb,
                   xring, lsem):
    i = pl.program_id(0)
    nused = nused_ref[0]
    changed = jnp.logical_or(i == 0, be_ref[i] != be_ref[jnp.maximum(i - 1, 0)])
    blk = ys_ref.shape[0]
    slot = lax.rem(i, EXPERT_RING)

    def load_block(j):
        s = lax.rem(j, EXPERT_RING)
        return pltpu.make_async_copy(xs_hbm.at[pl.ds(pl.multiple_of(j * blk, blk), blk)],
                                     xring.at[s], lsem.at[s])

    @pl.when(i == 0)
    def _():
        for j in range(EXPERT_RING - 1):
            @pl.when(j < nused)
            def _():
                load_block(j).start()

    @pl.when(i + EXPERT_RING - 1 < nused)
    def _():
        load_block(i + EXPERT_RING - 1).start()

    @pl.when(jnp.logical_and(i < nused, changed))
    def _():
        wgb[...] = wg_ref[0].astype(BF16)
        wub[...] = wu_ref[0].astype(BF16)
        wdb[...] = wd_ref[0].astype(BF16)

    @pl.when(i < nused)
    def _():
        load_block(i).wait()
        cuts = (0, blk // 2, blk) if blk >= 2 * LANES else (0, blk)
        hd = xring.shape[2]

        def unpack(a, b):
            lo, hi = _unpack_bf16_pairs(xring[slot, a:b, :])
            return lo.astype(BF16), hi.astype(BF16)

        def proj(h, w_ref):
            return _dot(h[0], w_ref[0:hd, :]) + _dot(h[1], w_ref[hd:2 * hd, :])

        def act(gu):
            return (_silu(gu[0]) * gu[1]).astype(BF16)

        hbs = [unpack(a, b) for a, b in zip(cuts[:-1], cuts[1:])]
        gus = [(proj(hb, wgb), proj(hb, wub)) for hb in hbs]
        for (a, b), gu in zip(zip(cuts[:-1], cuts[1:]), gus):
            y = _dot(act(gu), wdb[...])
            ys_ref[a:b, :] = _pack_bf16_pairs(y.astype(BF16).astype(F32))

    @pl.when(i >= nused)
    def _():
        ys_ref[...] = jnp.zeros_like(ys_ref)


def _experts(xs, blk_expert, n_used, w_gate, w_up, w_down, blk):
    n_slots, hd = xs.shape
    d = 2 * hd
    nb = n_slots // blk
    de = w_gate.shape[-1]
    grid_spec = pltpu.PrefetchScalarGridSpec(
        num_scalar_prefetch=2,
        grid=(nb,),
        in_specs=[
            pl.BlockSpec(memory_space=pl.ANY),
            pl.BlockSpec((1, d, de), lambda i, be, nu: (be[i], 0, 0)),
            pl.BlockSpec((1, d, de), lambda i, be, nu: (be[i], 0, 0)),
            pl.BlockSpec((1, de, d), lambda i, be, nu: (be[i], 0, 0)),
        ],
        out_specs=pl.BlockSpec((blk, hd), lambda i, be, nu: (i, 0)),
        scratch_shapes=[
            pltpu.VMEM((d, de), BF16),
            pltpu.VMEM((d, de), BF16),
            pltpu.VMEM((de, d), BF16),
            pltpu.VMEM((EXPERT_RING, blk, hd), xs.dtype),
            pltpu.SemaphoreType.DMA((EXPERT_RING,)),
        ],
    )
    return pl.pallas_call(
        _expert_kernel,
        grid_spec=grid_spec,
        out_shape=jax.ShapeDtypeStruct((n_slots, hd), jnp.uint32),
        compiler_params=pltpu.CompilerParams(
            dimension_semantics=("arbitrary",), vmem_limit_bytes=VMEM_LIMIT),
        name="experts",
    )(blk_expert, n_used, xs, w_gate, w_up, w_down)


def _combine_kernel(posc_ref, posn_ref, x1_ref, w_ref, gf_ref, ys_hbm, y_ref, *rest, tq, nsteps):
    bufs, sem = rest[:COMBINE_PARTS], rest[COMBINE_PARTS]
    i = pl.program_id(0)

    def gather(idx_ref, q):
        for r in range(2 * tq):
            pltpu.make_async_copy(ys_hbm.at[pl.ds(idx_ref[0, 0, q * 2 * tq + r], 1)],
                                  bufs[q].at[pl.ds(r, 1)], sem.at[q]).start(priority=r % 2)

    def wait(q):
        pltpu.make_async_copy(bufs[q], bufs[q], sem.at[q]).wait()

    def finish(q):
        rows = slice(q * tq, (q + 1) * tq)
        hd = bufs[q].shape[1]
        w = w_ref[rows, :]
        y0 = _unpack_bf16_pairs(bufs[q][0:tq, :])
        y1 = _unpack_bf16_pairs(bufs[q][tq:2 * tq, :])
        xo = [x1_ref[rows, h * hd:(h + 1) * hd] + w[:, 0:1] * y0[h] + w[:, 1:2] * y1[h]
              for h in range(2)]
        ms = (jnp.sum(xo[0] * xo[0], axis=-1, keepdims=True)
              + jnp.sum(xo[1] * xo[1], axis=-1, keepdims=True)) * (1.0 / (2 * hd))
        r = lax.rsqrt(ms + EPS)
        for h in range(2):
            y_ref[rows, h * hd:(h + 1) * hd] = xo[h] * r * gf_ref[:, h * hd:(h + 1) * hd]

    @pl.when(i == 0)
    def _():
        for q in range(COMBINE_PARTS - 1):
            gather(posc_ref, q)

    for q in range(COMBINE_PARTS):
        wait(q)
        ahead = q + COMBINE_PARTS - 1
        if ahead < COMBINE_PARTS:
            gather(posc_ref, ahead)
        else:
            gather(posn_ref, ahead - COMBINE_PARTS)
        finish(q)

    @pl.when(i == nsteps - 1)
    def _():
        for q in range(COMBINE_PARTS - 1):
            wait(q)


def _combine(x1_flat, dest, w_col, ys, gf, tm):
    t, d = x1_flat.shape
    nsteps = t // tm
    tq = tm // COMBINE_PARTS
    pos3 = dest.reshape(2, nsteps, COMBINE_PARTS, tq).transpose(1, 2, 0, 3).reshape(nsteps, 1, 2 * tm)
    return pl.pallas_call(
        functools.partial(_combine_kernel, tq=tq, nsteps=nsteps),
        grid=(nsteps,),
        in_specs=[
            pl.BlockSpec((1, 1, 2 * tm), lambda i: (i, 0, 0), memory_space=pltpu.SMEM),
            pl.BlockSpec((1, 1, 2 * tm), lambda i, _n=nsteps: (jnp.minimum(i + 1, _n - 1), 0, 0),
                         memory_space=pltpu.SMEM),
            pl.BlockSpec((tm, d), lambda i: (i, 0)),
            pl.BlockSpec((tm, 2), lambda i: (i, 0)),
            pl.BlockSpec((1, d), lambda i: (0, 0)),
            pl.BlockSpec(memory_space=pl.ANY),
        ],
        out_specs=pl.BlockSpec((tm, d), lambda i: (i, 0)),
        out_shape=jax.ShapeDtypeStruct((t, d), F32),
        scratch_shapes=([pltpu.VMEM((2 * tq, ys.shape[1]), ys.dtype) for _ in range(COMBINE_PARTS)]
                        + [pltpu.SemaphoreType.DMA((COMBINE_PARTS,))]),
        compiler_params=pltpu.CompilerParams(
            dimension_semantics=("arbitrary",), vmem_limit_bytes=VMEM_LIMIT),
        name="combine",
    )(pos3, pos3, x1_flat, w_col, gf, ys)


def _moe_and_norm(x1_flat, h2p_flat, logits_t, w_gate, w_up, w_down, gf):
    t, d = x1_flat.shape
    tr = _pick(t, (512, 256, 128))
    tm = _pick(t, (512, 256, 128))
    tmd = _pick(t, (512, 256, 128))
    blk = MOE_BLOCK if 2 * t >= 4 * N_EXPERTS * MOE_BLOCK else MOE_BLOCK_SMALL
    dest8, w8, meta = _route(logits_t, tr, blk)
    dest = dest8[0:2]
    nb = (2 * t + blk - 1) // blk + N_EXPERTS
    pend = meta[:, 2]
    blk_expert = jnp.minimum(
        jnp.sum(jnp.arange(nb, dtype=jnp.int32)[:, None] * blk >= pend[None, :], axis=1),
        N_EXPERTS - 1).astype(jnp.int32)
    n_used = (pend[-1:] // blk).astype(jnp.int32)
    xs = _dispatch(h2p_flat, dest, meta[:, 0:3].T.reshape(3 * N_EXPERTS), nb * blk, tmd, blk)
    ys = _experts(xs, blk_expert, n_used, w_gate, w_up, w_down, blk)
    return _combine(x1_flat, dest, w8[0:2].T, ys, gf, tm)


def _prep_layer_weights(norm1_g, w_in, conv_w, conv_b, dt_bias, a_log, d_skip, ssm_norm_g, ln_v_g,
                        ln_v_b, w_spatial, b_spatial, w_out, norm2_g, w_rg, b_rg, w_re, b_re):
    d = w_in.shape[0]
    conv_dim = SSD_WIDTH + 2 * SSD_GROUPS * SSD_STATE
    o_xbc = SSD_WIDTH
    o_dt = o_xbc + conv_dim
    o_gu = o_dt + SSD_HEADS
    o_gv = o_gu + GMLP_WIDTH
    w_dt = w_in[:, o_dt:o_gu]
    w_in_r = jnp.concatenate(
        [w_in[:, :o_xbc], w_in[:, o_xbc:o_dt], w_in[:, o_gu:o_gv], w_in[:, o_gv:],
         jnp.pad(w_dt, ((0, 0), (0, LANES - SSD_HEADS)))], axis=1).astype(BF16)
    rep = lambda v: jnp.repeat(v, SSD_HEAD_DIM)[None, :]
    col = lambda v: v[:, None]
    narrow = lambda v: jnp.pad(v, (0, LANES - SSD_HEADS))[None, :]
    sel = (jnp.arange(LANES)[:, None] == jnp.arange(SSD_WIDTH)[None, :] // SSD_HEAD_DIM).astype(BF16)
    w_r = jnp.concatenate([w_rg, jnp.zeros((d, SUBLANES - N_EXPERT_GROUPS), F32), w_re], axis=1).T
    b_r = jnp.concatenate([b_rg, jnp.zeros((SUBLANES - N_EXPERT_GROUPS,), F32), b_re])[:, None]
    w_r_hi = w_r.astype(BF16)
    w_r_lo = (w_r - w_r_hi.astype(F32)).astype(BF16)
    bsp = jnp.repeat(b_spatial.T, GMLP_GROUP_DIM, axis=1)
    return (norm1_g[None, :], w_in_r, w_dt.T.astype(BF16), conv_w, conv_b[None, :],
            narrow(dt_bias), narrow(a_log), sel, col(dt_bias), col(a_log), rep(d_skip),
            ssm_norm_g[None, :],
            ln_v_g[None, :], ln_v_b[None, :], w_spatial, bsp, w_out.astype(BF16), norm2_g[None, :],
            w_r_hi, w_r_lo, b_r)


def _state_to_pairs(s):
    b = s.shape[0]
    return s.reshape(b, N_PAIRS, 2, SSD_HEAD_DIM, SSD_STATE).transpose(0, 1, 4, 2, 3).reshape(
        b, N_PAIRS, SSD_STATE, 2 * SSD_HEAD_DIM)


def _pairs_to_state(s):
    b = s.shape[0]
    return s.reshape(b, N_PAIRS, SSD_STATE, 2, SSD_HEAD_DIM).transpose(0, 1, 3, 4, 2).reshape(
        b, SSD_HEADS, SSD_HEAD_DIM, SSD_STATE)


def _pick(n, prefs):
    for p in prefs:
        if n % p == 0:
            return p
    return n


def kernel(x_prompt, x_sample, cache_conv, state_ssm, norm1_g, w_in, conv_w, conv_b, dt_bias, a_log, d_skip, ssm_norm_g, ln_v_g, ln_v_b, w_spatial, b_spatial, w_out, norm2_g, w_router_group, b_router_group, w_router_expert, b_router_expert, w_gate, w_up, w_down, final_norm_g):
    depth = w_in.shape[0]
    assert depth == 1, "the combine kernel fuses the final norm, so only a single layer is supported"
    bp, lp, d = x_prompt.shape
    bs, ls, _ = x_sample.shape
    conv_dim = cache_conv.shape[-1]
    ls_pad = -(-ls // GMLP_CHUNK) * GMLP_CHUNK
    tl_p = _pick(lp, (256, 128))
    gf = final_norm_g[None, :]

    yp = x_prompt
    ys = jnp.pad(x_sample, ((0, 0), (0, ls_pad - ls), (0, 0)))
    conv_p, ssm_p, conv_s, ssm_s, v_s = [], [], [], [], []
    for i in range(depth):
        wts = _prep_layer_weights(
            norm1_g[i], w_in[i], conv_w[i], conv_b[i], dt_bias[i], a_log[i], d_skip[i], ssm_norm_g[i],
            ln_v_g[i], ln_v_b[i], w_spatial[i], b_spatial[i], w_out[i], norm2_g[i],
            w_router_group[i], b_router_group[i], w_router_expert[i], b_router_expert[i])
        gfi = gf

        conv0 = jnp.zeros((bp, SUBLANES, conv_dim), F32)
        ssm0 = jnp.zeros((bp, N_PAIRS, LANES, LANES), F32)
        x1, h2p, lt, cpo, hpo = _mixer(yp, conv0, ssm0, wts, tl=tl_p, l_valid=tl_p, emit_gv=False)
        tp = bp * lp
        outp = _moe_and_norm(x1.reshape(tp, d), h2p.reshape(tp, d // 2), lt, w_gate[i], w_up[i],
                             w_down[i], gfi)
        yp = outp.reshape(bp, lp, d)
        conv_p.append(cpo[:, SUBLANES - (CONV_WIDTH - 1):, :])
        ssm_p.append(_pairs_to_state(hpo))

        conv0 = jnp.pad(cache_conv[i], ((0, 0), (SUBLANES - (CONV_WIDTH - 1), 0), (0, 0)))
        x1, h2p, lt, cso, hso, gv = _mixer(ys, conv0, _state_to_pairs(state_ssm[i]), wts,
                                           tl=ls_pad, l_valid=ls, emit_gv=True)
        tsn = bs * ls
        x1v = x1[:, :ls].reshape(tsn, d)
        h2pv = h2p[:, :ls].reshape(tsn, d // 2)
        ltv = lt.reshape(ROUTER_ROWS, bs, ls_pad)[:, :, :ls].reshape(ROUTER_ROWS, tsn)
        outs = _moe_and_norm(x1v, h2pv, ltv, w_gate[i], w_up[i], w_down[i], gfi)
        ys = outs.reshape(bs, ls, d)
        conv_s.append(cso[:, SUBLANES - (CONV_WIDTH - 1):, :])
        ssm_s.append(_pairs_to_state(hso))
        v_s.append(gv[:, :ls])
    return (yp, ys, jnp.stack(conv_p), jnp.stack(ssm_p), jnp.stack(conv_s), jnp.stack(ssm_s),
            jnp.stack(v_s))
```

```python
import functools

import jax
import jax.numpy as jnp
from jax import lax
from jax.experimental import pallas as pl
from jax.experimental.pallas import tpu as pltpu

F32 = jnp.float32
BF16 = jnp.bfloat16
EPS = 1e-6

SSD_HEADS = 8
SSD_HEAD_DIM = 64
SSD_STATE = 128
SSD_GROUPS = 2
SSD_WIDTH = SSD_HEADS * SSD_HEAD_DIM
SSD_CHUNK = 64
CONV_WIDTH = 4
GMLP_GROUPS = 8
GMLP_GROUP_DIM = 64
GMLP_WIDTH = GMLP_GROUPS * GMLP_GROUP_DIM
GMLP_CHUNK = 128
N_EXPERT_GROUPS = 4
EXPERTS_PER_GROUP = 8
N_EXPERTS = N_EXPERT_GROUPS * EXPERTS_PER_GROUP
ROUTER_ROWS = 8 + N_EXPERTS
MOE_BLOCK = 512
MOE_BLOCK_SMALL = 32
DISPATCH_RING = 3
EXPERT_RING = 3
COMBINE_PARTS = 4

LANES = 128
SUBLANES = 8
N_PAIRS = SSD_HEADS // 2
VMEM_LIMIT = 56 * 1024 * 1024

C_Z = 0
C_XBC = C_Z + SSD_WIDTH
C_GU = C_XBC + SSD_WIDTH + 2 * SSD_GROUPS * SSD_STATE
C_GV = C_GU + GMLP_WIDTH
C_DT = C_GV + GMLP_WIDTH
C_END = C_DT + LANES
PROJ_SLAB = 256


def _silu(v):
    return v * (1.0 / (1.0 + jnp.exp(-v)))


def _gelu_tanh(v):
    c = 0.7978845608028654
    hv = 0.5 * v
    return hv + hv * jnp.tanh(v * (c + (c * 0.044715) * (v * v)))


def _softplus(v):
    return jnp.maximum(v, 0.0) + jnp.log1p(jnp.exp(-jnp.abs(v)))


def _dot(a, b):
    return jnp.dot(a, b, preferred_element_type=F32)


def _dot_nt(a, b):
    return lax.dot_general(a, b, (((1,), (1,)), ((), ())), preferred_element_type=F32)


def _dot_tn(a, b):
    return lax.dot_general(a, b, (((0,), (0,)), ((), ())), preferred_element_type=F32)


def _replicate_heads(v, sel_ref):
    hi = v.astype(BF16)
    r1 = v - hi.astype(F32)
    mid = r1.astype(BF16)
    lo = (r1 - mid.astype(F32)).astype(BF16)
    sel = sel_ref[...]
    return _dot(hi, sel) + _dot(mid, sel) + _dot(lo, sel)


def _mixer_kernel(xb_ref, xf_ref, conv0a_ref, conv0b_ref, ssm0a_ref, ssm0b_ref,
                  g1_ref, win_ref, wdt_ref, convw_ref, convb_ref,
                  dtb_n_ref, alog_n_ref, sel_ref, dtb_col_ref, alog_col_ref, dskip_ref, ssmg_ref,
                  lng_ref, lnb_ref, wsp_ref, bsp_ref, wout_ref, g2_ref, wr_hi_ref, wr_lo_ref,
                  br_ref,
                  x1_ref, h2p_ref, logit_ref, convo_ref, ssmo_ref, *rest,
                  tl, nt, l_valid, emit_gv):
    if emit_gv:
        gvo_ref = rest[0]
        rest = rest[1:]
    else:
        gvo_ref = None
    (pa_ref, pb_ref, da_ref, db_ref, xpad_ref, state_ref, wspm_ref, zs_ref, gus_ref, gvs_ref,
     gvf_ref, dtc_ref, acs_ref, dtt_ref) = rest
    s = pl.program_id(0)

    def drain_pieces(pbuf, dbuf, v):
        def p_xbc():
            xpad_ref[v, SUBLANES:SUBLANES + tl, :] = pbuf[:, C_XBC:C_GU]

        def p_dt():
            dtn = _softplus(pbuf[:, C_DT:C_END] + dtb_n_ref[...])
            row = lax.broadcasted_iota(jnp.int32, (tl, LANES), 0)
            if l_valid < tl:
                dtn = jnp.where(row < l_valid, dtn, 0.0)
            acsn = dtn * (-jnp.exp(alog_n_ref[...]))
            row_in_chunk = jnp.bitwise_and(row, SSD_CHUNK - 1)
            sh = 1
            while sh < SSD_CHUNK:
                acsn = acsn + jnp.where(row_in_chunk >= sh, pltpu.roll(acsn, sh, axis=0), 0.0)
                sh *= 2
            dtc_ref[v] = _replicate_heads(dtn, sel_ref)
            acs_ref[v] = _replicate_heads(acsn, sel_ref)
            dtt_ref[v] = _softplus(dbuf[...] + dtb_col_ref[...])

        def p_z():
            zs_ref[v] = _silu(pbuf[:, C_Z:C_XBC])

        def p_gu():
            gus_ref[v] = _gelu_tanh(pbuf[:, C_GU:C_GV])

        def p_gv():
            gv = _gelu_tanh(pbuf[:, C_GV:C_DT])
            mu = jnp.mean(gv, axis=-1, keepdims=True)
            gvc = gv - mu
            var = jnp.mean(gvc * gvc, axis=-1, keepdims=True)
            gv = gvc * lax.rsqrt(var + EPS) * lng_ref[...] + lnb_ref[...]
            if gvo_ref is not None:
                gvf_ref[v] = gv
            gvs_ref[v] = gv.astype(BF16)
        return [p_xbc, p_dt, p_z, p_gu, p_gv]

    def front_pieces(x, pbuf, dbuf):
        hb = []

        def norm():
            h = x * lax.rsqrt(jnp.mean(x * x, axis=-1, keepdims=True) + EPS) * g1_ref[...]
            hb.append(h.astype(BF16))
            dbuf[...] = _dot_nt(wdt_ref[...], hb[0])

        def slab(c0):
            c1 = min(c0 + PROJ_SLAB, C_END)

            def run():
                pbuf[:, c0:c1] = _dot(hb[0], win_ref[:, c0:c1])
            return run
        return [norm] + [slab(c0) for c0 in range(0, C_END, PROJ_SLAB)]

    def front(x, pbuf, dbuf):
        for piece in front_pieces(x, pbuf, dbuf):
            piece()

    @pl.when(s == 0)
    def _():
        r = lax.broadcasted_iota(jnp.int32, (GMLP_CHUNK, GMLP_CHUNK), 0)
        c = lax.broadcasted_iota(jnp.int32, (GMLP_CHUNK, GMLP_CHUNK), 1)
        for gi in range(GMLP_GROUPS):
            wspm_ref[gi] = jnp.where(r >= c, wsp_ref[gi], 0.0).astype(BF16)
        for v in range(2):
            xpad_ref[v, 0:SUBLANES, :] = jnp.zeros((SUBLANES, xpad_ref.shape[2]), F32)
        state_ref[...] = jnp.zeros_like(state_ref)
        front(xb_ref[0], pa_ref, da_ref)
        front(xb_ref[1], pb_ref, db_ref)
        for piece in drain_pieces(pa_ref, da_ref, 0):
            piece()

    bufs = ((pa_ref, da_ref, conv0a_ref, ssm0a_ref), (pb_ref, db_ref, conv0b_ref, ssm0b_ref))
    for u, (pbuf, dbuf, conv0_ref, ssm0_ref) in enumerate(bufs):
        is_first = lax.rem(2 * s + u, nt) == 0
        xpad_ref[u, 0:SUBLANES, :] = jnp.where(is_first, conv0_ref[0], xpad_ref[u, 0:SUBLANES, :])
        for j in range(N_PAIRS):
            state_ref[j] = jnp.where(is_first, ssm0_ref[0, j], state_ref[j])
        if gvo_ref is not None:
            gvo_ref[u] = gvf_ref[u]

        nxt = bufs[1 - u]
        _mixer_back(
            xb_ref[u], front_pieces(xf_ref[u], pbuf, dbuf), drain_pieces(nxt[0], nxt[1], 1 - u),
            convw_ref=convw_ref,
            convb_ref=convb_ref, alog_col_ref=alog_col_ref, dskip_ref=dskip_ref,
            ssmg_ref=ssmg_ref, bsp_ref=bsp_ref, wout_ref=wout_ref, g2_ref=g2_ref,
            wr_hi_ref=wr_hi_ref, wr_lo_ref=wr_lo_ref, br_ref=br_ref, x1_ref=x1_ref,
            h2p_ref=h2p_ref, logit_ref=logit_ref, convo_ref=convo_ref, ssmo_ref=ssmo_ref, xpad_ref=xpad_ref,
            state_ref=state_ref, wspm_ref=wspm_ref, zs_ref=zs_ref, gus_ref=gus_ref,
            gvs_ref=gvs_ref, dtc_ref=dtc_ref, acs_ref=acs_ref, dtt_ref=dtt_ref,
            u=u, tl=tl, l_valid=l_valid)


def _mixer_back(x, fillers, vfillers, *, convw_ref, convb_ref, alog_col_ref, dskip_ref,
                ssmg_ref, bsp_ref, wout_ref, g2_ref, wr_hi_ref, wr_lo_ref, br_ref, x1_ref, h2p_ref,
                logit_ref, convo_ref, ssmo_ref, xpad_ref, state_ref, wspm_ref, zs_ref, gus_ref,
                gvs_ref, dtc_ref, acs_ref, dtt_ref, u, tl, l_valid):
    fillers = list(fillers)
    vfillers = list(vfillers)
    dt = dtc_ref[u]
    acs = acs_ref[u]
    dtt = dtt_ref[u]

    def fill(n):
        for _ in range(min(n, len(fillers))):
            fillers.pop(0)()

    def vfill(n):
        for _ in range(min(n, len(vfillers))):
            vfillers.pop(0)()

    fill(1)
    conv = convb_ref[...]
    for k in range(CONV_WIDTH):
        off = SUBLANES - (CONV_WIDTH - 1) + k
        conv = conv + xpad_ref[u, off:off + tl, :] * convw_ref[k:k + 1, :]
        fill(1)
    xbc = _silu(conv)
    carry = xpad_ref[u, l_valid:l_valid + SUBLANES, :]
    xpad_ref[1 - u, 0:SUBLANES, :] = carry
    convo_ref[u] = carry

    xs = xbc[:, 0:SSD_WIDTH]
    bm = xbc[:, SSD_WIDTH:SSD_WIDTH + SSD_GROUPS * SSD_STATE].astype(BF16)
    cm = xbc[:, SSD_WIDTH + SSD_GROUPS * SSD_STATE:].astype(BF16)

    lane_t = lax.broadcasted_iota(jnp.int32, (SSD_HEADS, tl), 1)
    if l_valid < tl:
        dtt = jnp.where(lane_t < l_valid, dtt, 0.0)
    acst = dtt * (-jnp.exp(alog_col_ref[...]))
    lane_in_chunk = jnp.bitwise_and(lane_t, SSD_CHUNK - 1)
    sh = 1
    while sh < SSD_CHUNK:
        acst = acst + jnp.where(lane_in_chunk >= sh, pltpu.roll(acst, sh, axis=1), 0.0)
        sh *= 2

    lane = lax.broadcasted_iota(jnp.int32, (SSD_CHUNK, LANES), 1)
    rowc = lax.broadcasted_iota(jnp.int32, (SSD_CHUNK, LANES), 0)
    lo_half = lane < SSD_HEAD_DIM
    causal = rowc >= jnp.bitwise_and(lane, SSD_CHUNK - 1)
    lane1 = lax.broadcasted_iota(jnp.int32, (1, LANES), 1)
    lo_half1 = lane1 < SSD_CHUNK

    y_chunks = []
    for c in range(tl // SSD_CHUNK):
        r0 = c * SSD_CHUNK
        v = acst[:, (c // 2) * LANES:(c // 2 + 1) * LANES]
        vr = pltpu.roll(v, SSD_CHUNK, axis=1)
        v_lo, v_hi = (v, vr) if c % 2 == 0 else (vr, v)
        cb2 = []
        for g in range(SSD_GROUPS):
            cg = cm[r0:r0 + SSD_CHUNK, g * SSD_STATE:(g + 1) * SSD_STATE]
            bg = bm[r0:r0 + SSD_CHUNK, g * SSD_STATE:(g + 1) * SSD_STATE]
            cb2.append(_dot_nt(cg, jnp.concatenate([bg, bg], axis=0)))
        y_pairs = []
        for j in range(N_PAIRS):
            g = j // (N_PAIRS // SSD_GROUPS)
            cg = cm[r0:r0 + SSD_CHUNK, g * SSD_STATE:(g + 1) * SSD_STATE]
            bg = bm[r0:r0 + SSD_CHUNK, g * SSD_STATE:(g + 1) * SSD_STATE]
            sl = slice(j * LANES, (j + 1) * LANES)
            col_a = acs[r0:r0 + SSD_CHUNK, sl]
            row_a = jnp.where(lo_half1, v_lo[2 * j:2 * j + 1, :], v_hi[2 * j + 1:2 * j + 2, :])
            decay = jnp.where(causal, jnp.exp(col_a - row_a), 0.0)
            m = (cb2[g] * decay).astype(BF16)
            xdt = xs[r0:r0 + SSD_CHUNK, sl] * dt[r0:r0 + SSD_CHUNK, sl]
            zbd = jnp.concatenate([jnp.where(lo_half, xdt, 0.0), jnp.where(lo_half, 0.0, xdt)],
                                  axis=0).astype(BF16)
            y_diag = _dot(m, zbd)
            st = state_ref[j]
            y_off = _dot(cg, st.astype(BF16)) * jnp.exp(col_a)
            a_last = acs[r0 + SSD_CHUNK - 1:r0 + SSD_CHUNK, sl]
            zdte = (xdt * jnp.exp(a_last - col_a)).astype(BF16)
            state_ref[j] = st * jnp.exp(a_last) + _dot_tn(bg, zdte)
            y_pairs.append(y_diag + y_off)
        y_chunks.append(jnp.concatenate(y_pairs, axis=1))
        fill(1)
    y = jnp.concatenate(y_chunks, axis=0) if len(y_chunks) > 1 else y_chunks[0]
    ssmo_ref[u] = state_ref[...]

    y = y + xs * dskip_ref[...]
    gated = y * zs_ref[u]
    half = SSD_WIDTH // SSD_GROUPS
    outs = []
    for g in range(SSD_GROUPS):
        gg = gated[:, g * half:(g + 1) * half]
        outs.append(gg * lax.rsqrt(jnp.mean(gg * gg, axis=-1, keepdims=True) + EPS))
    ssd_out = jnp.concatenate(outs, axis=1) * ssmg_ref[...]
    fill(1)

    lane_g = lax.broadcasted_iota(jnp.int32, (GMLP_CHUNK, LANES), 1)
    lo_g = lane_g < GMLP_GROUP_DIM
    mixed_chunks = []
    for q in range(tl // GMLP_CHUNK):
        mixed_pairs = []
        for j in range(GMLP_GROUPS // 2):
            vp = gvs_ref[u, q * GMLP_CHUNK:(q + 1) * GMLP_CHUNK, j * LANES:(j + 1) * LANES]
            r_even = _dot(wspm_ref[2 * j], vp)
            r_odd = _dot(wspm_ref[2 * j + 1], vp)
            mixed_pairs.append(jnp.where(lo_g, r_even, r_odd))
        mixed_chunks.append(jnp.concatenate(mixed_pairs, axis=1) + bsp_ref[...])
    mixed = jnp.concatenate(mixed_chunks, axis=0) if len(mixed_chunks) > 1 else mixed_chunks[0]
    gmlp_out = gus_ref[u] * mixed

    merged = jnp.concatenate([ssd_out, gmlp_out], axis=1).astype(BF16)
    x1 = x + _dot(merged, wout_ref[...])
    x1_ref[u] = x1
    fill(len(fillers))

    h2 = x1 * lax.rsqrt(jnp.mean(x1 * x1, axis=-1, keepdims=True) + EPS) * g2_ref[...]
    h2_hi = h2.astype(BF16)
    h2_hif = h2_hi.astype(F32)
    h2_lo = (h2 - h2_hif).astype(BF16)
    bits = lax.bitcast_convert_type(h2_hif, jnp.uint32)
    hd = bits.shape[1] // 2
    h2p_ref[u] = jnp.bitwise_or(lax.shift_right_logical(bits[:, :hd], jnp.uint32(16)),
                                jnp.bitwise_and(bits[:, hd:], jnp.uint32(0xFFFF0000)))
    logit_ref[:, u * tl:(u + 1) * tl] = (
        _dot_nt(wr_hi_ref[...], h2_hi) + _dot_nt(wr_lo_ref[...], h2_hi)
        + _dot_nt(wr_hi_ref[...], h2_lo) + br_ref[...])
    vfill(len(vfillers))


def _full_spec(shape):
    nd = len(shape)
    return pl.BlockSpec(shape, lambda s, _nd=nd: (0,) * _nd)


def _mixer(x, conv0, ssm0, wts, *, tl, l_valid, emit_gv):
    bsz, l, d = x.shape
    nt = l // tl
    g = bsz * nt
    assert l % tl == 0 and tl % GMLP_CHUNK == 0 and l_valid % SUBLANES == 0
    assert (nt == 1 or l_valid == tl) and g % 2 == 0
    steps = g // 2
    tile = lambda f: (lambda s: (f(s), 0, 0))
    seq = lambda f: (lambda s: (f(s) // nt,) + (0,) * 2)
    seq4 = lambda f: (lambda s: (f(s) // nt,) + (0,) * 3)
    in_specs = [
        pl.BlockSpec((2, tl, d), tile(lambda s: s)),
        pl.BlockSpec((2, tl, d), tile(lambda s: jnp.minimum(s + 1, steps - 1))),
        pl.BlockSpec((1, SUBLANES, d), seq(lambda s: 2 * s)),
        pl.BlockSpec((1, SUBLANES, d), seq(lambda s: 2 * s + 1)),
        pl.BlockSpec((1, N_PAIRS, LANES, LANES), seq4(lambda s: 2 * s)),
        pl.BlockSpec((1, N_PAIRS, LANES, LANES), seq4(lambda s: 2 * s + 1)),
    ] + [_full_spec(w.shape) for w in wts]
    out_shape = [
        jax.ShapeDtypeStruct((g, tl, d), F32),
        jax.ShapeDtypeStruct((g, tl, d // 2), jnp.uint32),
        jax.ShapeDtypeStruct((ROUTER_ROWS, g * tl), F32),
        jax.ShapeDtypeStruct((g, SUBLANES, d), F32),
        jax.ShapeDtypeStruct((g, N_PAIRS, LANES, LANES), F32),
    ]
    out_specs = [
        pl.BlockSpec((2, tl, d), lambda s: (s, 0, 0)),
        pl.BlockSpec((2, tl, d // 2), lambda s: (s, 0, 0)),
        pl.BlockSpec((ROUTER_ROWS, 2 * tl), lambda s: (0, s)),
        pl.BlockSpec((2, SUBLANES, d), lambda s: (s, 0, 0)),
        pl.BlockSpec((2, N_PAIRS, LANES, LANES), lambda s: (s, 0, 0, 0)),
    ]
    if emit_gv:
        out_shape.append(jax.ShapeDtypeStruct((g, tl, GMLP_WIDTH), F32))
        out_specs.append(pl.BlockSpec((2, tl, GMLP_WIDTH), lambda s: (s, 0, 0)))
    xt = x.reshape(g, tl, d)
    outs = pl.pallas_call(
        functools.partial(_mixer_kernel, tl=tl, nt=nt, l_valid=l_valid, emit_gv=emit_gv),
        grid=(steps,),
        in_specs=in_specs,
        out_specs=out_specs,
        out_shape=out_shape,
        scratch_shapes=[
            pltpu.VMEM((tl, C_END), F32),
            pltpu.VMEM((tl, C_END), F32),
            pltpu.VMEM((SSD_HEADS, tl), F32),
            pltpu.VMEM((SSD_HEADS, tl), F32),
            pltpu.VMEM((2, tl + SUBLANES, d), F32),
            pltpu.VMEM((N_PAIRS, LANES, LANES), F32),
            pltpu.VMEM((GMLP_GROUPS, GMLP_CHUNK, GMLP_CHUNK), BF16),
            pltpu.VMEM((2, tl, SSD_WIDTH), F32),
            pltpu.VMEM((2, tl, GMLP_WIDTH), F32),
            pltpu.VMEM((2, tl, GMLP_WIDTH), BF16),
            pltpu.VMEM((2, tl, GMLP_WIDTH), F32),
            pltpu.VMEM((2, tl, SSD_WIDTH), F32),
            pltpu.VMEM((2, tl, SSD_WIDTH), F32),
            pltpu.VMEM((2, SSD_HEADS, tl), F32),
        ],
        compiler_params=pltpu.CompilerParams(
            dimension_semantics=("arbitrary",), vmem_limit_bytes=VMEM_LIMIT),
        name="mixer",
    )(xt, xt, conv0, conv0, ssm0, ssm0, *wts)
    x1, h2p, lt, cvo, sso = outs[:5]
    last = slice(nt - 1, None, nt)
    res = (x1.reshape(bsz, l, d), h2p.reshape(bsz, l, d // 2), lt, cvo[last], sso[last])
    if emit_gv:
        res += (outs[5].reshape(bsz, l, GMLP_WIDTH),)
    return res


def _route_kernel(logit_ref, tri_ref, dest_ref, w_ref, meta_ref, base_ref, keep_ref, *, tr, nt, blk):
    i = pl.program_id(0)
    sub8 = lax.broadcasted_iota(jnp.int32, (SUBLANES, tr), 0).astype(F32)
    sube = lax.broadcasted_iota(jnp.int32, (N_EXPERTS, tr), 0).astype(F32)

    @pl.when(i == 0)
    def _():
        base_ref[...] = jnp.zeros_like(base_ref)

    @pl.when(i < nt)
    def _():
        _route_pass0(logit_ref, tri_ref, base_ref, keep_ref, pl.multiple_of(i * tr, tr), sub8, sube, tr)

    @pl.when(i == nt)
    def _():
        counts = base_ref[...]
        padded = jnp.floor((counts + float(blk - 1)) * (1.0 / blk)) * float(blk)
        sub_e = lax.broadcasted_iota(jnp.int32, (N_EXPERTS, LANES), 0)
        pend = padded
        sh = 1
        while sh < N_EXPERTS:
            pend = pend + jnp.where(sub_e >= sh, pltpu.roll(pend, sh, axis=0), 0.0)
            sh *= 2
        pstart = pend - padded
        lane_e = lax.broadcasted_iota(jnp.int32, (N_EXPERTS, LANES), 1)
        meta = jnp.where(lane_e == 0, counts, jnp.where(lane_e == 1, pstart,
                         jnp.where(lane_e == 2, pend, 0.0)))
        meta_ref[...] = meta.astype(jnp.int32)

        def chunk(c, carry):
            sl = pl.ds(pl.multiple_of(c * tr, tr), tr)
            kept = keep_ref[:, sl]
            ps1 = jnp.sum(jnp.where(sube == kept[0:1], pstart[:, 0:1], 0.0), axis=0, keepdims=True)
            ps2 = jnp.sum(jnp.where(sube == kept[1:2], pstart[:, 0:1], 0.0), axis=0, keepdims=True)
            dest = jnp.where(sub8 == 0, ps1 + kept[2:3],
                             jnp.where(sub8 == 1, ps2 + kept[3:4], 0.0))
            dest_ref[:, sl] = dest.astype(jnp.int32)
            w_ref[:, sl] = jnp.where(sub8 == 0, kept[4:5], jnp.where(sub8 == 1, kept[5:6], 0.0))
            return carry
        lax.fori_loop(0, nt, chunk, 0)


def _route_pass0(logit_ref, tri_ref, base_ref, keep_ref, off, sub8, sube, tr):
    lg = logit_ref[...]
    big = float(SUBLANES)
    gl = jnp.where(sub8 < N_EXPERT_GROUPS, lg[0:SUBLANES], -jnp.inf)
    gmax = jnp.max(gl, axis=0, keepdims=True)
    g_sel = jnp.min(jnp.where(gl == gmax, sub8, big), axis=0, keepdims=True)
    p_group = 1.0 / jnp.sum(jnp.exp(gl - gmax), axis=0, keepdims=True)
    el = lg[SUBLANES:2 * SUBLANES]
    for g in range(1, N_EXPERT_GROUPS):
        el = jnp.where(g_sel == g, lg[(g + 1) * SUBLANES:(g + 2) * SUBLANES], el)
    top1 = jnp.max(el, axis=0, keepdims=True)
    i1 = jnp.min(jnp.where(el == top1, sub8, big), axis=0, keepdims=True)
    el2 = jnp.where(sub8 == i1, -jnp.inf, el)
    top2 = jnp.max(el2, axis=0, keepdims=True)
    i2 = jnp.min(jnp.where(el2 == top2, sub8, big), axis=0, keepdims=True)
    ex = jnp.exp(top2 - top1)
    w1 = p_group * (1.0 / (1.0 + ex))
    w2 = p_group * (ex / (1.0 + ex))
    e1 = g_sel * EXPERTS_PER_GROUP + i1
    e2 = g_sel * EXPERTS_PER_GROUP + i2

    oh1 = (sube == e1)
    oh2 = (sube == e2)
    tri = tri_ref[...]
    cum1 = _dot(jnp.where(oh1, 1.0, 0.0).astype(BF16), tri)
    cum2 = _dot(jnp.where(oh2, 1.0, 0.0).astype(BF16), tri)
    tot1 = jnp.sum(jnp.where(oh1, 1.0, 0.0), axis=1, keepdims=True)
    tot2 = jnp.sum(jnp.where(oh2, 1.0, 0.0), axis=1, keepdims=True)
    base = base_ref[:, 0:1]
    r1 = jnp.sum(jnp.where(oh1, cum1 + base, 0.0), axis=0, keepdims=True)
    r2 = jnp.sum(jnp.where(oh2, cum2 + base + tot1, 0.0), axis=0, keepdims=True)
    base_ref[...] = jnp.broadcast_to(base + tot1 + tot2, base_ref.shape)
    keep_ref[:, pl.ds(off, tr)] = jnp.where(
        sub8 == 0, e1, jnp.where(sub8 == 1, e2, jnp.where(sub8 == 2, r1, jnp.where(
            sub8 == 3, r2, jnp.where(sub8 == 4, w1, jnp.where(sub8 == 5, w2, 0.0))))))


def _route(logits_t, tr, blk):
    rows, t = logits_t.shape
    assert t % tr == 0
    nt = t // tr
    tri = jnp.triu(jnp.ones((tr, tr), BF16), k=1)
    return pl.pallas_call(
        functools.partial(_route_kernel, tr=tr, nt=nt, blk=blk),
        grid=(nt + 1,),
        in_specs=[pl.BlockSpec((rows, tr), lambda i, _n=nt: (0, jnp.minimum(i, _n - 1))),
                  pl.BlockSpec((tr, tr), lambda i: (0, 0))],
        out_specs=[pl.BlockSpec((SUBLANES, t), lambda i: (0, 0)),
                   pl.BlockSpec((SUBLANES, t), lambda i: (0, 0)),
                   pl.BlockSpec((N_EXPERTS, LANES), lambda i: (0, 0))],
        out_shape=[jax.ShapeDtypeStruct((SUBLANES, t), jnp.int32),
                   jax.ShapeDtypeStruct((SUBLANES, t), F32),
                   jax.ShapeDtypeStruct((N_EXPERTS, LANES), jnp.int32)],
        scratch_shapes=[pltpu.VMEM((N_EXPERTS, LANES), F32), pltpu.VMEM((SUBLANES, t), F32)],
        compiler_params=pltpu.CompilerParams(
            dimension_semantics=("arbitrary",), vmem_limit_bytes=VMEM_LIMIT),
        name="route",
    )(logits_t, tri)


def _dispatch_kernel(meta_ref, dest_ref, x1_hbm, xs_hbm, xin, zbuf, lsem, sem, zsem, *,
                     tmd, nsteps, blk, nb):
    i = pl.program_id(0)

    def load_tile(j, slot):
        return pltpu.make_async_copy(x1_hbm.at[pl.ds(pl.multiple_of(j * tmd, tmd), tmd)],
                                     xin.at[slot], lsem.at[slot])

    def wait_step(slot):
        for _ in range(2):
            pltpu.make_async_copy(xin.at[slot], xin.at[slot], sem.at[slot]).wait()

    def pad_rows(e, row_fn, oct_fn):
        start = meta_ref[N_EXPERTS + e] + meta_ref[e]
        start8 = lax.shift_right_logical(start + (SUBLANES - 1), 3)
        lax.fori_loop(start, start8 * SUBLANES, row_fn, 0)
        lax.fori_loop(start8, lax.shift_right_logical(meta_ref[2 * N_EXPERTS + e], 3), oct_fn, 0)

    def zero_row(row):
        return pltpu.make_async_copy(zbuf.at[pl.ds(0, 1)], xs_hbm.at[pl.ds(row, 1)], zsem)

    def zero_oct(o):
        return pltpu.make_async_copy(
            zbuf.at[pl.ds(0, SUBLANES)],
            xs_hbm.at[pl.ds(pl.multiple_of(o * SUBLANES, SUBLANES), SUBLANES)], zsem)

    def zero_block(b):
        return pltpu.make_async_copy(zbuf, xs_hbm.at[pl.ds(pl.multiple_of(b * blk, blk), blk)], zsem)

    @pl.when(i == 0)
    def _():
        zbuf[...] = jnp.zeros_like(zbuf)
        n_used = lax.div(meta_ref[3 * N_EXPERTS - 1], blk)

        def start_row(row, c):
            zero_row(row).start()
            return c

        def wait_row(row, c):
            zero_row(row).wait()
            return c

        def start_oct(o, c):
            zero_oct(o).start()
            return c

        def wait_oct(o, c):
            zero_oct(o).wait()
            return c

        def start_blk(b, c):
            zero_block(b).start()
            return c

        def wait_blk(b, c):
            zero_block(b).wait()
            return c

        for e in range(N_EXPERTS):
            pad_rows(e, start_row, start_oct)
        lax.fori_loop(n_used, nb, start_blk, 0)
        for e in range(N_EXPERTS):
            pad_rows(e, wait_row, wait_oct)
        lax.fori_loop(n_used, nb, wait_blk, 0)

        load_tile(0, 0).start()
        if nsteps > 1:
            load_tile(1, 1).start()

    slot = lax.rem(i, DISPATCH_RING)
    load_tile(i, slot).wait()
    for r in range(tmd):
        for k in range(2):
            pltpu.make_async_copy(xin.at[slot, pl.ds(r, 1)],
                                  xs_hbm.at[pl.ds(dest_ref[0, 0, k * tmd + r], 1)],
                                  sem.at[slot]).start(priority=k)

    @pl.when(i > 0)
    def _():
        wait_step(lax.rem(i + DISPATCH_RING - 1, DISPATCH_RING))

    @pl.when(i + 2 < nsteps)
    def _():
        load_tile(i + 2, lax.rem(i + 2, DISPATCH_RING)).start()

    @pl.when(i == nsteps - 1)
    def _():
        wait_step(slot)


def _dispatch(rows, dest, meta_s, n_slots, tmd, blk):
    t, d = rows.shape
    dt = rows.dtype
    nsteps = t // tmd
    dest3 = dest.reshape(2, nsteps, tmd).transpose(1, 0, 2).reshape(nsteps, 1, 2 * tmd)
    grid_spec = pltpu.PrefetchScalarGridSpec(
        num_scalar_prefetch=1,
        grid=(nsteps,),
        in_specs=[pl.BlockSpec((1, 1, 2 * tmd), lambda i, m: (i, 0, 0), memory_space=pltpu.SMEM),
                  pl.BlockSpec(memory_space=pl.ANY)],
        out_specs=pl.BlockSpec(memory_space=pl.ANY),
        scratch_shapes=[pltpu.VMEM((DISPATCH_RING, tmd, d), dt), pltpu.VMEM((blk, d), dt),
                        pltpu.SemaphoreType.DMA((DISPATCH_RING,)),
                        pltpu.SemaphoreType.DMA((DISPATCH_RING,)), pltpu.SemaphoreType.DMA(())],
    )
    return pl.pallas_call(
        functools.partial(_dispatch_kernel, tmd=tmd, nsteps=nsteps, blk=blk, nb=n_slots // blk),
        grid_spec=grid_spec,
        out_shape=jax.ShapeDtypeStruct((n_slots, d), dt),
        compiler_params=pltpu.CompilerParams(dimension_semantics=("arbitrary",)),
        name="dispatch",
    )(meta_s, dest3, rows)


def _expert_kernel(be_ref, nused_ref, xs_hbm, wg_ref, wu_ref, wd_ref, ys_ref, wgb, wub, wdb,
                   xring, lsem):
    i = pl.program_id(0)
    nused = nused_ref[0]
    changed = jnp.logical_or(i == 0, be_ref[i] != be_ref[jnp.maximum(i - 1, 0)])
    blk = ys_ref.shape[0]
    slot = lax.rem(i, EXPERT_RING)

    def load_block(j):
        s = lax.rem(j, EXPERT_RING)
        return pltpu.make_async_copy(xs_hbm.at[pl.ds(pl.multiple_of(j * blk, blk), blk)],
                                     xring.at[s], lsem.at[s])

    @pl.when(i == 0)
    def _():
        for j in range(EXPERT_RING - 1):
            @pl.when(j < nused)
            def _():
                load_block(j).start()

    @pl.when(i + EXPERT_RING - 1 < nused)
    def _():
        load_block(i + EXPERT_RING - 1).start()

    @pl.when(jnp.logical_and(i < nused, changed))
    def _():
        wgb[...] = wg_ref[0].astype(BF16)
        wub[...] = wu_ref[0].astype(BF16)
        wdb[...] = wd_ref[0].astype(BF16)

    @pl.when(i < nused)
    def _():
        load_block(i).wait()
        cuts = (0, blk // 2, blk) if blk >= 2 * LANES else (0, blk)
        hd = xring.shape[2]

        def unpack(a, b):
            w = xring[slot, a:b, :]
            lo = lax.bitcast_convert_type(lax.shift_left(w, jnp.uint32(16)), F32)
            hi = lax.bitcast_convert_type(jnp.bitwise_and(w, jnp.uint32(0xFFFF0000)), F32)
            return lo.astype(BF16), hi.astype(BF16)

        def proj(h, w_ref):
            return _dot(h[0], w_ref[0:hd, :]) + _dot(h[1], w_ref[hd:2 * hd, :])

        def act(gu):
            return (_silu(gu[0]) * gu[1]).astype(BF16)

        hbs = [unpack(a, b) for a, b in zip(cuts[:-1], cuts[1:])]
        gus = [(proj(hb, wgb), proj(hb, wub)) for hb in hbs]
        for (a, b), gu in zip(zip(cuts[:-1], cuts[1:]), gus):
            ys_ref[a:b, :] = _dot(act(gu), wdb[...])

    @pl.when(i >= nused)
    def _():
        ys_ref[...] = jnp.zeros_like(ys_ref)


def _experts(xs, blk_expert, n_used, w_gate, w_up, w_down, blk):
    n_slots, hd = xs.shape
    d = 2 * hd
    nb = n_slots // blk
    de = w_gate.shape[-1]
    grid_spec = pltpu.PrefetchScalarGridSpec(
        num_scalar_prefetch=2,
        grid=(nb,),
        in_specs=[
            pl.BlockSpec(memory_space=pl.ANY),
            pl.BlockSpec((1, d, de), lambda i, be, nu: (be[i], 0, 0)),
            pl.BlockSpec((1, d, de), lambda i, be, nu: (be[i], 0, 0)),
            pl.BlockSpec((1, de, d), lambda i, be, nu: (be[i], 0, 0)),
        ],
        out_specs=pl.BlockSpec((blk, d), lambda i, be, nu: (i, 0)),
        scratch_shapes=[
            pltpu.VMEM((d, de), BF16),
            pltpu.VMEM((d, de), BF16),
            pltpu.VMEM((de, d), BF16),
            pltpu.VMEM((EXPERT_RING, blk, hd), xs.dtype),
            pltpu.SemaphoreType.DMA((EXPERT_RING,)),
        ],
    )
    return pl.pallas_call(
        _expert_kernel,
        grid_spec=grid_spec,
        out_shape=jax.ShapeDtypeStruct((n_slots, d), F32),
        compiler_params=pltpu.CompilerParams(
            dimension_semantics=("arbitrary",), vmem_limit_bytes=VMEM_LIMIT),
        name="experts",
    )(blk_expert, n_used, xs, w_gate, w_up, w_down)


def _combine_kernel(posc_ref, posn_ref, x1_ref, w_ref, gf_ref, ys_hbm, y_ref, *rest, tq, nsteps):
    bufs, sem = rest[:COMBINE_PARTS], rest[COMBINE_PARTS]
    i = pl.program_id(0)

    def gather(idx_ref, q):
        for r in range(2 * tq):
            pltpu.make_async_copy(ys_hbm.at[pl.ds(idx_ref[0, 0, q * 2 * tq + r], 1)],
                                  bufs[q].at[pl.ds(r, 1)], sem.at[q]).start(priority=r % 2)

    def wait(q):
        pltpu.make_async_copy(bufs[q], bufs[q], sem.at[q]).wait()

    def finish(q):
        rows = slice(q * tq, (q + 1) * tq)
        w = w_ref[rows, :]
        xo = x1_ref[rows, :] + w[:, 0:1] * bufs[q][0:tq, :] + w[:, 1:2] * bufs[q][tq:2 * tq, :]
        y_ref[rows, :] = (xo * lax.rsqrt(jnp.mean(xo * xo, axis=-1, keepdims=True) + EPS)
                          * gf_ref[...])

    @pl.when(i == 0)
    def _():
        for q in range(COMBINE_PARTS - 1):
            gather(posc_ref, q)

    for q in range(COMBINE_PARTS):
        wait(q)
        ahead = q + COMBINE_PARTS - 1
        if ahead < COMBINE_PARTS:
            gather(posc_ref, ahead)
        else:
            gather(posn_ref, ahead - COMBINE_PARTS)
        finish(q)

    @pl.when(i == nsteps - 1)
    def _():
        for q in range(COMBINE_PARTS - 1):
            wait(q)


def _combine(x1_flat, dest, w_col, ys, gf, tm):
    t, d = x1_flat.shape
    nsteps = t // tm
    tq = tm // COMBINE_PARTS
    pos3 = dest.reshape(2, nsteps, COMBINE_PARTS, tq).transpose(1, 2, 0, 3).reshape(nsteps, 1, 2 * tm)
    return pl.pallas_call(
        functools.partial(_combine_kernel, tq=tq, nsteps=nsteps),
        grid=(nsteps,),
        in_specs=[
            pl.BlockSpec((1, 1, 2 * tm), lambda i: (i, 0, 0), memory_space=pltpu.SMEM),
            pl.BlockSpec((1, 1, 2 * tm), lambda i, _n=nsteps: (jnp.minimum(i + 1, _n - 1), 0, 0),
                         memory_space=pltpu.SMEM),
            pl.BlockSpec((tm, d), lambda i: (i, 0)),
            pl.BlockSpec((tm, 2), lambda i: (i, 0)),
            pl.BlockSpec((1, d), lambda i: (0, 0)),
            pl.BlockSpec(memory_space=pl.ANY),
        ],
        out_specs=pl.BlockSpec((tm, d), lambda i: (i, 0)),
        out_shape=jax.ShapeDtypeStruct((t, d), F32),
        scratch_shapes=([pltpu.VMEM((2 * tq, d), F32) for _ in range(COMBINE_PARTS)]
                        + [pltpu.SemaphoreType.DMA((COMBINE_PARTS,))]),
        compiler_params=pltpu.CompilerParams(
            dimension_semantics=("arbitrary",), vmem_limit_bytes=VMEM_LIMIT),
        name="combine",
    )(pos3, pos3, x1_flat, w_col, gf, ys)


def _moe_and_norm(x1_flat, h2p_flat, logits_t, w_gate, w_up, w_down, gf):
    t, d = x1_flat.shape
    tr = _pick(t, (512, 256, 128))
    tm = _pick(t, (1024, 512, 256, 128))
    tmd = _pick(t, (1024, 512, 256, 128))
    blk = MOE_BLOCK if 2 * t >= 4 * N_EXPERTS * MOE_BLOCK else MOE_BLOCK_SMALL
    dest8, w8, meta = _route(logits_t, tr, blk)
    dest = dest8[0:2]
    nb = (2 * t + blk - 1) // blk + N_EXPERTS
    pend = meta[:, 2]
    blk_expert = jnp.minimum(
        jnp.sum(jnp.arange(nb, dtype=jnp.int32)[:, None] * blk >= pend[None, :], axis=1),
        N_EXPERTS - 1).astype(jnp.int32)
    n_used = (pend[-1:] // blk).astype(jnp.int32)
    xs = _dispatch(h2p_flat, dest, meta[:, 0:3].T.reshape(3 * N_EXPERTS), nb * blk, tmd, blk)
    ys = _experts(xs, blk_expert, n_used, w_gate, w_up, w_down, blk)
    return _combine(x1_flat, dest, w8[0:2].T, ys, gf, tm)


def _prep_layer_weights(norm1_g, w_in, conv_w, conv_b, dt_bias, a_log, d_skip, ssm_norm_g, ln_v_g,
                        ln_v_b, w_spatial, b_spatial, w_out, norm2_g, w_rg, b_rg, w_re, b_re):
    d = w_in.shape[0]
    conv_dim = SSD_WIDTH + 2 * SSD_GROUPS * SSD_STATE
    o_xbc = SSD_WIDTH
    o_dt = o_xbc + conv_dim
    o_gu = o_dt + SSD_HEADS
    o_gv = o_gu + GMLP_WIDTH
    w_dt = w_in[:, o_dt:o_gu]
    w_in_r = jnp.concatenate(
        [w_in[:, :o_xbc], w_in[:, o_xbc:o_dt], w_in[:, o_gu:o_gv], w_in[:, o_gv:],
         jnp.pad(w_dt, ((0, 0), (0, LANES - SSD_HEADS)))], axis=1).astype(BF16)
    rep = lambda v: jnp.repeat(v, SSD_HEAD_DIM)[None, :]
    col = lambda v: v[:, None]
    narrow = lambda v: jnp.pad(v, (0, LANES - SSD_HEADS))[None, :]
    sel = (jnp.arange(LANES)[:, None] == jnp.arange(SSD_WIDTH)[None, :] // SSD_HEAD_DIM).astype(BF16)
    w_r = jnp.concatenate([w_rg, jnp.zeros((d, SUBLANES - N_EXPERT_GROUPS), F32), w_re], axis=1).T
    b_r = jnp.concatenate([b_rg, jnp.zeros((SUBLANES - N_EXPERT_GROUPS,), F32), b_re])[:, None]
    w_r_hi = w_r.astype(BF16)
    w_r_lo = (w_r - w_r_hi.astype(F32)).astype(BF16)
    bsp = jnp.repeat(b_spatial.T, GMLP_GROUP_DIM, axis=1)
    return (norm1_g[None, :], w_in_r, w_dt.T.astype(BF16), conv_w, conv_b[None, :],
            narrow(dt_bias), narrow(a_log), sel, col(dt_bias), col(a_log), rep(d_skip),
            ssm_norm_g[None, :],
            ln_v_g[None, :], ln_v_b[None, :], w_spatial, bsp, w_out.astype(BF16), norm2_g[None, :],
            w_r_hi, w_r_lo, b_r)


def _state_to_pairs(s):
    b = s.shape[0]
    return s.reshape(b, N_PAIRS, 2, SSD_HEAD_DIM, SSD_STATE).transpose(0, 1, 4, 2, 3).reshape(
        b, N_PAIRS, SSD_STATE, 2 * SSD_HEAD_DIM)


def _pairs_to_state(s):
    b = s.shape[0]
    return s.reshape(b, N_PAIRS, SSD_STATE, 2, SSD_HEAD_DIM).transpose(0, 1, 3, 4, 2).reshape(
        b, SSD_HEADS, SSD_HEAD_DIM, SSD_STATE)


def _pick(n, prefs):
    for p in prefs:
        if n % p == 0:
            return p
    return n


def kernel(x_prompt, x_sample, cache_conv, state_ssm, norm1_g, w_in, conv_w, conv_b, dt_bias, a_log, d_skip, ssm_norm_g, ln_v_g, ln_v_b, w_spatial, b_spatial, w_out, norm2_g, w_router_group, b_router_group, w_router_expert, b_router_expert, w_gate, w_up, w_down, final_norm_g):
    depth = w_in.shape[0]
    assert depth == 1, "the combine kernel fuses the final norm, so only a single layer is supported"
    bp, lp, d = x_prompt.shape
    bs, ls, _ = x_sample.shape
    conv_dim = cache_conv.shape[-1]
    ls_pad = -(-ls // GMLP_CHUNK) * GMLP_CHUNK
    tl_p = _pick(lp, (256, 128))
    gf = final_norm_g[None, :]

    yp = x_prompt
    ys = jnp.pad(x_sample, ((0, 0), (0, ls_pad - ls), (0, 0)))
    conv_p, ssm_p, conv_s, ssm_s, v_s = [], [], [], [], []
    for i in range(depth):
        wts = _prep_layer_weights(
            norm1_g[i], w_in[i], conv_w[i], conv_b[i], dt_bias[i], a_log[i], d_skip[i], ssm_norm_g[i],
            ln_v_g[i], ln_v_b[i], w_spatial[i], b_spatial[i], w_out[i], norm2_g[i],
            w_router_group[i], b_router_group[i], w_router_expert[i], b_router_expert[i])
        gfi = gf

        conv0 = jnp.zeros((bp, SUBLANES, conv_dim), F32)
        ssm0 = jnp.zeros((bp, N_PAIRS, LANES, LANES), F32)
        x1, h2p, lt, cpo, hpo = _mixer(yp, conv0, ssm0, wts, tl=tl_p, l_valid=tl_p, emit_gv=False)
        tp = bp * lp
        outp = _moe_and_norm(x1.reshape(tp, d), h2p.reshape(tp, d // 2), lt, w_gate[i], w_up[i],
                             w_down[i], gfi)
        yp = outp.reshape(bp, lp, d)
        conv_p.append(cpo[:, SUBLANES - (CONV_WIDTH - 1):, :])
        ssm_p.append(_pairs_to_state(hpo))

        conv0 = jnp.pad(cache_conv[i], ((0, 0), (SUBLANES - (CONV_WIDTH - 1), 0), (0, 0)))
        x1, h2p, lt, cso, hso, gv = _mixer(ys, conv0, _state_to_pairs(state_ssm[i]), wts,
                                           tl=ls_pad, l_valid=ls, emit_gv=True)
        tsn = bs * ls
        x1v = x1[:, :ls].reshape(tsn, d)
        h2pv = h2p[:, :ls].reshape(tsn, d // 2)
        ltv = lt.reshape(ROUTER_ROWS, bs, ls_pad)[:, :, :ls].reshape(ROUTER_ROWS, tsn)
        outs = _moe_and_norm(x1v, h2pv, ltv, w_gate[i], w_up[i], w_down[i], gfi)
        ys = outs.reshape(bs, ls, d)
        conv_s.append(cso[:, SUBLANES - (CONV_WIDTH - 1):, :])
        ssm_s.append(_pairs_to_state(hso))
        v_s.append(gv[:, :ls])
    return (yp, ys, jnp.stack(conv_p), jnp.stack(ssm_p), jnp.stack(conv_s), jnp.stack(ssm_s),
            jnp.stack(v_s))
```

```python
import functools

import jax
import jax.numpy as jnp
from jax import lax
from jax.experimental import pallas as pl
from jax.experimental.pallas import tpu as pltpu

F32 = jnp.float32
BF16 = jnp.bfloat16
EPS = 1e-6

SSD_HEADS = 8
SSD_HEAD_DIM = 64
SSD_STATE = 128
SSD_GROUPS = 2
SSD_WIDTH = SSD_HEADS * SSD_HEAD_DIM
SSD_CHUNK = 64
CONV_WIDTH = 4
GMLP_GROUPS = 8
GMLP_GROUP_DIM = 64
GMLP_WIDTH = GMLP_GROUPS * GMLP_GROUP_DIM
GMLP_CHUNK = 128
N_EXPERT_GROUPS = 4
EXPERTS_PER_GROUP = 8
N_EXPERTS = N_EXPERT_GROUPS * EXPERTS_PER_GROUP
ROUTER_ROWS = 8 + N_EXPERTS
MOE_BLOCK = 512
MOE_BLOCK_SMALL = 32
DISPATCH_RING = 3
EXPERT_RING = 3
COMBINE_PARTS = 4

LANES = 128
SUBLANES = 8
N_PAIRS = SSD_HEADS // 2
VMEM_LIMIT = 56 * 1024 * 1024

C_Z = 0
C_XBC = C_Z + SSD_WIDTH
C_GU = C_XBC + SSD_WIDTH + 2 * SSD_GROUPS * SSD_STATE
C_GV = C_GU + GMLP_WIDTH
C_DT = C_GV + GMLP_WIDTH
C_END = C_DT + LANES
PROJ_SLAB = 256


def _silu(v):
    return v * (1.0 / (1.0 + jnp.exp(-v)))


def _gelu_tanh(v):
    c = 0.7978845608028654
    hv = 0.5 * v
    return hv + hv * jnp.tanh(v * (c + (c * 0.044715) * (v * v)))


def _softplus(v):
    return jnp.maximum(v, 0.0) + jnp.log1p(jnp.exp(-jnp.abs(v)))


def _dot(a, b):
    return jnp.dot(a, b, preferred_element_type=F32)


def _dot_nt(a, b):
    return lax.dot_general(a, b, (((1,), (1,)), ((), ())), preferred_element_type=F32)


def _dot_tn(a, b):
    return lax.dot_general(a, b, (((0,), (0,)), ((), ())), preferred_element_type=F32)


def _replicate_heads(v, sel_ref):
    hi = v.astype(BF16)
    r1 = v - hi.astype(F32)
    mid = r1.astype(BF16)
    lo = (r1 - mid.astype(F32)).astype(BF16)
    sel = sel_ref[...]
    return _dot(hi, sel) + _dot(mid, sel) + _dot(lo, sel)


def _mixer_kernel(xb_ref, xf_ref, conv0a_ref, conv0b_ref, ssm0a_ref, ssm0b_ref,
                  g1_ref, win_ref, wdt_ref, convw_ref, convb_ref,
                  dtb_n_ref, alog_n_ref, sel_ref, dtb_col_ref, alog_col_ref, dskip_ref, ssmg_ref,
                  lng_ref, lnb_ref, wsp_ref, bsp_ref, wout_ref, g2_ref, wr_hi_ref, wr_lo_ref,
                  br_ref,
                  x1_ref, h2p_ref, logit_ref, convo_ref, ssmo_ref, *rest,
                  tl, nt, l_valid, emit_gv):
    if emit_gv:
        gvo_ref = rest[0]
        rest = rest[1:]
    else:
        gvo_ref = None
    (pa_ref, pb_ref, da_ref, db_ref, xpad_ref, state_ref, wspm_ref, zs_ref, gus_ref, gvs_ref,
     gvf_ref, dtc_ref, acs_ref, dtt_ref) = rest
    s = pl.program_id(0)

    def drain_pieces(pbuf, dbuf, v):
        def p_xbc():
            xpad_ref[v, SUBLANES:SUBLANES + tl, :] = pbuf[:, C_XBC:C_GU]

        def p_dt():
            dtn = _softplus(pbuf[:, C_DT:C_END] + dtb_n_ref[...])
            row = lax.broadcasted_iota(jnp.int32, (tl, LANES), 0)
            if l_valid < tl:
                dtn = jnp.where(row < l_valid, dtn, 0.0)
            acsn = dtn * (-jnp.exp(alog_n_ref[...]))
            row_in_chunk = jnp.bitwise_and(row, SSD_CHUNK - 1)
            sh = 1
            while sh < SSD_CHUNK:
                acsn = acsn + jnp.where(row_in_chunk >= sh, pltpu.roll(acsn, sh, axis=0), 0.0)
                sh *= 2
            dtc_ref[v] = _replicate_heads(dtn, sel_ref)
            acs_ref[v] = _replicate_heads(acsn, sel_ref)
            dtt_ref[v] = _softplus(dbuf[...] + dtb_col_ref[...])

        def p_z():
            zs_ref[v] = _silu(pbuf[:, C_Z:C_XBC])

        def p_gu():
            gus_ref[v] = _gelu_tanh(pbuf[:, C_GU:C_GV])

        def p_gv():
            gv = _gelu_tanh(pbuf[:, C_GV:C_DT])
            mu = jnp.mean(gv, axis=-1, keepdims=True)
            gvc = gv - mu
            var = jnp.mean(gvc * gvc, axis=-1, keepdims=True)
            gv = gvc * lax.rsqrt(var + EPS) * lng_ref[...] + lnb_ref[...]
            if gvo_ref is not None:
                gvf_ref[v] = gv
            gvs_ref[v] = gv.astype(BF16)
        return [p_xbc, p_dt, p_z, p_gu, p_gv]

    def front_pieces(x, pbuf, dbuf):
        hb = []

        def norm():
            h = x * lax.rsqrt(jnp.mean(x * x, axis=-1, keepdims=True) + EPS) * g1_ref[...]
            hb.append(h.astype(BF16))
            dbuf[...] = _dot_nt(wdt_ref[...], hb[0])

        def slab(c0):
            c1 = min(c0 + PROJ_SLAB, C_END)

            def run():
                pbuf[:, c0:c1] = _dot(hb[0], win_ref[:, c0:c1])
            return run
        return [norm] + [slab(c0) for c0 in range(0, C_END, PROJ_SLAB)]

    def front(x, pbuf, dbuf):
        for piece in front_pieces(x, pbuf, dbuf):
            piece()

    @pl.when(s == 0)
    def _():
        r = lax.broadcasted_iota(jnp.int32, (GMLP_CHUNK, GMLP_CHUNK), 0)
        c = lax.broadcasted_iota(jnp.int32, (GMLP_CHUNK, GMLP_CHUNK), 1)
        for gi in range(GMLP_GROUPS):
            wspm_ref[gi] = jnp.where(r >= c, wsp_ref[gi], 0.0).astype(BF16)
        for v in range(2):
            xpad_ref[v, 0:SUBLANES, :] = jnp.zeros((SUBLANES, xpad_ref.shape[2]), F32)
        state_ref[...] = jnp.zeros_like(state_ref)
        front(xb_ref[0], pa_ref, da_ref)
        front(xb_ref[1], pb_ref, db_ref)
        for piece in drain_pieces(pa_ref, da_ref, 0):
            piece()

    bufs = ((pa_ref, da_ref, conv0a_ref, ssm0a_ref), (pb_ref, db_ref, conv0b_ref, ssm0b_ref))
    for u, (pbuf, dbuf, conv0_ref, ssm0_ref) in enumerate(bufs):
        is_first = lax.rem(2 * s + u, nt) == 0
        xpad_ref[u, 0:SUBLANES, :] = jnp.where(is_first, conv0_ref[0], xpad_ref[u, 0:SUBLANES, :])
        for j in range(N_PAIRS):
            state_ref[j] = jnp.where(is_first, ssm0_ref[0, j], state_ref[j])
        if gvo_ref is not None:
            gvo_ref[u] = gvf_ref[u]

        nxt = bufs[1 - u]
        _mixer_back(
            xb_ref[u], front_pieces(xf_ref[u], pbuf, dbuf), drain_pieces(nxt[0], nxt[1], 1 - u),
            convw_ref=convw_ref,
            convb_ref=convb_ref, alog_col_ref=alog_col_ref, dskip_ref=dskip_ref,
            ssmg_ref=ssmg_ref, bsp_ref=bsp_ref, wout_ref=wout_ref, g2_ref=g2_ref,
            wr_hi_ref=wr_hi_ref, wr_lo_ref=wr_lo_ref, br_ref=br_ref, x1_ref=x1_ref,
            h2p_ref=h2p_ref, logit_ref=logit_ref, convo_ref=convo_ref, ssmo_ref=ssmo_ref, xpad_ref=xpad_ref,
            state_ref=state_ref, wspm_ref=wspm_ref, zs_ref=zs_ref, gus_ref=gus_ref,
            gvs_ref=gvs_ref, dtc_ref=dtc_ref, acs_ref=acs_ref, dtt_ref=dtt_ref,
            u=u, tl=tl, l_valid=l_valid)


def _mixer_back(x, fillers, vfillers, *, convw_ref, convb_ref, alog_col_ref, dskip_ref,
                ssmg_ref, bsp_ref, wout_ref, g2_ref, wr_hi_ref, wr_lo_ref, br_ref, x1_ref, h2p_ref,
                logit_ref, convo_ref, ssmo_ref, xpad_ref, state_ref, wspm_ref, zs_ref, gus_ref,
                gvs_ref, dtc_ref, acs_ref, dtt_ref, u, tl, l_valid):
    fillers = list(fillers)
    vfillers = list(vfillers)
    dt = dtc_ref[u]
    acs = acs_ref[u]
    dtt = dtt_ref[u]

    def fill(n):
        for _ in range(min(n, len(fillers))):
            fillers.pop(0)()

    def vfill(n):
        for _ in range(min(n, len(vfillers))):
            vfillers.pop(0)()

    fill(1)
    conv = convb_ref[...]
    for k in range(CONV_WIDTH):
        off = SUBLANES - (CONV_WIDTH - 1) + k
        conv = conv + xpad_ref[u, off:off + tl, :] * convw_ref[k:k + 1, :]
        fill(1)
    xbc = _silu(conv)
    carry = xpad_ref[u, l_valid:l_valid + SUBLANES, :]
    xpad_ref[1 - u, 0:SUBLANES, :] = carry
    convo_ref[u] = carry

    xs = xbc[:, 0:SSD_WIDTH]
    bm = xbc[:, SSD_WIDTH:SSD_WIDTH + SSD_GROUPS * SSD_STATE].astype(BF16)
    cm = xbc[:, SSD_WIDTH + SSD_GROUPS * SSD_STATE:].astype(BF16)

    lane_t = lax.broadcasted_iota(jnp.int32, (SSD_HEADS, tl), 1)
    if l_valid < tl:
        dtt = jnp.where(lane_t < l_valid, dtt, 0.0)
    acst = dtt * (-jnp.exp(alog_col_ref[...]))
    lane_in_chunk = jnp.bitwise_and(lane_t, SSD_CHUNK - 1)
    sh = 1
    while sh < SSD_CHUNK:
        acst = acst + jnp.where(lane_in_chunk >= sh, pltpu.roll(acst, sh, axis=1), 0.0)
        sh *= 2

    lane = lax.broadcasted_iota(jnp.int32, (SSD_CHUNK, LANES), 1)
    rowc = lax.broadcasted_iota(jnp.int32, (SSD_CHUNK, LANES), 0)
    lo_half = lane < SSD_HEAD_DIM
    causal = rowc >= jnp.bitwise_and(lane, SSD_CHUNK - 1)
    lane1 = lax.broadcasted_iota(jnp.int32, (1, LANES), 1)
    lo_half1 = lane1 < SSD_CHUNK

    y_chunks = []
    for c in range(tl // SSD_CHUNK):
        r0 = c * SSD_CHUNK
        v = acst[:, (c // 2) * LANES:(c // 2 + 1) * LANES]
        vr = pltpu.roll(v, SSD_CHUNK, axis=1)
        v_lo, v_hi = (v, vr) if c % 2 == 0 else (vr, v)
        cb2 = []
        for g in range(SSD_GROUPS):
            cg = cm[r0:r0 + SSD_CHUNK, g * SSD_STATE:(g + 1) * SSD_STATE]
            bg = bm[r0:r0 + SSD_CHUNK, g * SSD_STATE:(g + 1) * SSD_STATE]
            cb2.append(_dot_nt(cg, jnp.concatenate([bg, bg], axis=0)))
        y_pairs = []
        for j in range(N_PAIRS):
            g = j // (N_PAIRS // SSD_GROUPS)
            cg = cm[r0:r0 + SSD_CHUNK, g * SSD_STATE:(g + 1) * SSD_STATE]
            bg = bm[r0:r0 + SSD_CHUNK, g * SSD_STATE:(g + 1) * SSD_STATE]
            sl = slice(j * LANES, (j + 1) * LANES)
            col_a = acs[r0:r0 + SSD_CHUNK, sl]
            row_a = jnp.where(lo_half1, v_lo[2 * j:2 * j + 1, :], v_hi[2 * j + 1:2 * j + 2, :])
            decay = jnp.where(causal, jnp.exp(col_a - row_a), 0.0)
            m = (cb2[g] * decay).astype(BF16)
            xdt = xs[r0:r0 + SSD_CHUNK, sl] * dt[r0:r0 + SSD_CHUNK, sl]
            zbd = jnp.concatenate([jnp.where(lo_half, xdt, 0.0), jnp.where(lo_half, 0.0, xdt)],
                                  axis=0).astype(BF16)
            y_diag = _dot(m, zbd)
            st = state_ref[j]
            y_off = _dot(cg, st.astype(BF16)) * jnp.exp(col_a)
            a_last = acs[r0 + SSD_CHUNK - 1:r0 + SSD_CHUNK, sl]
            zdte = (xdt * jnp.exp(a_last - col_a)).astype(BF16)
            state_ref[j] = st * jnp.exp(a_last) + _dot_tn(bg, zdte)
            y_pairs.append(y_diag + y_off)
        y_chunks.append(jnp.concatenate(y_pairs, axis=1))
        fill(1)
    y = jnp.concatenate(y_chunks, axis=0) if len(y_chunks) > 1 else y_chunks[0]
    ssmo_ref[u] = state_ref[...]

    y = y + xs * dskip_ref[...]
    gated = y * zs_ref[u]
    half = SSD_WIDTH // SSD_GROUPS
    outs = []
    for g in range(SSD_GROUPS):
        gg = gated[:, g * half:(g + 1) * half]
        outs.append(gg * lax.rsqrt(jnp.mean(gg * gg, axis=-1, keepdims=True) + EPS))
    ssd_out = jnp.concatenate(outs, axis=1) * ssmg_ref[...]
    fill(1)

    lane_g = lax.broadcasted_iota(jnp.int32, (GMLP_CHUNK, LANES), 1)
    lo_g = lane_g < GMLP_GROUP_DIM
    mixed_chunks = []
    for q in range(tl // GMLP_CHUNK):
        mixed_pairs = []
        for j in range(GMLP_GROUPS // 2):
            vp = gvs_ref[u, q * GMLP_CHUNK:(q + 1) * GMLP_CHUNK, j * LANES:(j + 1) * LANES]
            r_even = _dot(wspm_ref[2 * j], vp)
            r_odd = _dot(wspm_ref[2 * j + 1], vp)
            mixed_pairs.append(jnp.where(lo_g, r_even, r_odd))
        mixed_chunks.append(jnp.concatenate(mixed_pairs, axis=1) + bsp_ref[...])
    mixed = jnp.concatenate(mixed_chunks, axis=0) if len(mixed_chunks) > 1 else mixed_chunks[0]
    gmlp_out = gus_ref[u] * mixed

    merged = jnp.concatenate([ssd_out, gmlp_out], axis=1).astype(BF16)
    x1 = x + _dot(merged, wout_ref[...])
    x1_ref[u] = x1
    fill(len(fillers))

    h2 = x1 * lax.rsqrt(jnp.mean(x1 * x1, axis=-1, keepdims=True) + EPS) * g2_ref[...]
    h2_hi = h2.astype(BF16)
    h2_hif = h2_hi.astype(F32)
    h2_lo = (h2 - h2_hif).astype(BF16)
    bits = lax.bitcast_convert_type(h2_hif, jnp.uint32)
    hd = bits.shape[1] // 2
    h2p_ref[u] = jnp.bitwise_or(lax.shift_right_logical(bits[:, :hd], jnp.uint32(16)),
                                jnp.bitwise_and(bits[:, hd:], jnp.uint32(0xFFFF0000)))
    logit_ref[:, u * tl:(u + 1) * tl] = (
        _dot_nt(wr_hi_ref[...], h2_hi) + _dot_nt(wr_lo_ref[...], h2_hi)
        + _dot_nt(wr_hi_ref[...], h2_lo) + br_ref[...])
    vfill(len(vfillers))


def _full_spec(shape):
    nd = len(shape)
    return pl.BlockSpec(shape, lambda s, _nd=nd: (0,) * _nd)


def _mixer(x, conv0, ssm0, wts, *, tl, l_valid, emit_gv):
    bsz, l, d = x.shape
    nt = l // tl
    g = bsz * nt
    assert l % tl == 0 and tl % GMLP_CHUNK == 0 and l_valid % SUBLANES == 0
    assert (nt == 1 or l_valid == tl) and g % 2 == 0
    steps = g // 2
    tile = lambda f: (lambda s: (f(s), 0, 0))
    seq = lambda f: (lambda s: (f(s) // nt,) + (0,) * 2)
    seq4 = lambda f: (lambda s: (f(s) // nt,) + (0,) * 3)
    in_specs = [
        pl.BlockSpec((2, tl, d), tile(lambda s: s)),
        pl.BlockSpec((2, tl, d), tile(lambda s: jnp.minimum(s + 1, steps - 1))),
        pl.BlockSpec((1, SUBLANES, d), seq(lambda s: 2 * s)),
        pl.BlockSpec((1, SUBLANES, d), seq(lambda s: 2 * s + 1)),
        pl.BlockSpec((1, N_PAIRS, LANES, LANES), seq4(lambda s: 2 * s)),
        pl.BlockSpec((1, N_PAIRS, LANES, LANES), seq4(lambda s: 2 * s + 1)),
    ] + [_full_spec(w.shape) for w in wts]
    out_shape = [
        jax.ShapeDtypeStruct((g, tl, d), F32),
        jax.ShapeDtypeStruct((g, tl, d // 2), jnp.uint32),
        jax.ShapeDtypeStruct((ROUTER_ROWS, g * tl), F32),
        jax.ShapeDtypeStruct((g, SUBLANES, d), F32),
        jax.ShapeDtypeStruct((g, N_PAIRS, LANES, LANES), F32),
    ]
    out_specs = [
        pl.BlockSpec((2, tl, d), lambda s: (s, 0, 0)),
        pl.BlockSpec((2, tl, d // 2), lambda s: (s, 0, 0)),
        pl.BlockSpec((ROUTER_ROWS, 2 * tl), lambda s: (0, s)),
        pl.BlockSpec((2, SUBLANES, d), lambda s: (s, 0, 0)),
        pl.BlockSpec((2, N_PAIRS, LANES, LANES), lambda s: (s, 0, 0, 0)),
    ]
    if emit_gv:
        out_shape.append(jax.ShapeDtypeStruct((g, tl, GMLP_WIDTH), F32))
        out_specs.append(pl.BlockSpec((2, tl, GMLP_WIDTH), lambda s: (s, 0, 0)))
    xt = x.reshape(g, tl, d)
    outs = pl.pallas_call(
        functools.partial(_mixer_kernel, tl=tl, nt=nt, l_valid=l_valid, emit_gv=emit_gv),
        grid=(steps,),
        in_specs=in_specs,
        out_specs=out_specs,
        out_shape=out_shape,
        scratch_shapes=[
            pltpu.VMEM((tl, C_END), F32),
            pltpu.VMEM((tl, C_END), F32),
            pltpu.VMEM((SSD_HEADS, tl), F32),
            pltpu.VMEM((SSD_HEADS, tl), F32),
            pltpu.VMEM((2, tl + SUBLANES, d), F32),
            pltpu.VMEM((N_PAIRS, LANES, LANES), F32),
            pltpu.VMEM((GMLP_GROUPS, GMLP_CHUNK, GMLP_CHUNK), BF16),
            pltpu.VMEM((2, tl, SSD_WIDTH), F32),
            pltpu.VMEM((2, tl, GMLP_WIDTH), F32),
            pltpu.VMEM((2, tl, GMLP_WIDTH), BF16),
            pltpu.VMEM((2, tl, GMLP_WIDTH), F32),
            pltpu.VMEM((2, tl, SSD_WIDTH), F32),
            pltpu.VMEM((2, tl, SSD_WIDTH), F32),
            pltpu.VMEM((2, SSD_HEADS, tl), F32),
        ],
        compiler_params=pltpu.CompilerParams(
            dimension_semantics=("arbitrary",), vmem_limit_bytes=VMEM_LIMIT),
        name="mixer",
    )(xt, xt, conv0, conv0, ssm0, ssm0, *wts)
    x1, h2p, lt, cvo, sso = outs[:5]
    last = slice(nt - 1, None, nt)
    res = (x1.reshape(bsz, l, d), h2p.reshape(bsz, l, d // 2), lt, cvo[last], sso[last])
    if emit_gv:
        res += (outs[5].reshape(bsz, l, GMLP_WIDTH),)
    return res


def _route_kernel(logit_ref, tri_ref, dest_ref, w_ref, meta_ref, base_ref, keep_ref, *, tr, nt, blk):
    i = pl.program_id(0)
    sub8 = lax.broadcasted_iota(jnp.int32, (SUBLANES, tr), 0).astype(F32)
    sube = lax.broadcasted_iota(jnp.int32, (N_EXPERTS, tr), 0).astype(F32)

    @pl.when(i == 0)
    def _():
        base_ref[...] = jnp.zeros_like(base_ref)

    @pl.when(i < nt)
    def _():
        _route_pass0(logit_ref, tri_ref, base_ref, keep_ref, pl.multiple_of(i * tr, tr), sub8, sube, tr)

    @pl.when(i == nt)
    def _():
        counts = base_ref[...]
        padded = jnp.floor((counts + float(blk - 1)) * (1.0 / blk)) * float(blk)
        sub_e = lax.broadcasted_iota(jnp.int32, (N_EXPERTS, LANES), 0)
        pend = padded
        sh = 1
        while sh < N_EXPERTS:
            pend = pend + jnp.where(sub_e >= sh, pltpu.roll(pend, sh, axis=0), 0.0)
            sh *= 2
        pstart = pend - padded
        lane_e = lax.broadcasted_iota(jnp.int32, (N_EXPERTS, LANES), 1)
        meta = jnp.where(lane_e == 0, counts, jnp.where(lane_e == 1, pstart,
                         jnp.where(lane_e == 2, pend, 0.0)))
        meta_ref[...] = meta.astype(jnp.int32)

        def chunk(c, carry):
            sl = pl.ds(pl.multiple_of(c * tr, tr), tr)
            kept = keep_ref[:, sl]
            ps1 = jnp.sum(jnp.where(sube == kept[0:1], pstart[:, 0:1], 0.0), axis=0, keepdims=True)
            ps2 = jnp.sum(jnp.where(sube == kept[1:2], pstart[:, 0:1], 0.0), axis=0, keepdims=True)
            dest = jnp.where(sub8 == 0, ps1 + kept[2:3],
                             jnp.where(sub8 == 1, ps2 + kept[3:4], 0.0))
            dest_ref[:, sl] = dest.astype(jnp.int32)
            w_ref[:, sl] = jnp.where(sub8 == 0, kept[4:5], jnp.where(sub8 == 1, kept[5:6], 0.0))
            return carry
        lax.fori_loop(0, nt, chunk, 0)


def _route_pass0(logit_ref, tri_ref, base_ref, keep_ref, off, sub8, sube, tr):
    lg = logit_ref[...]
    big = float(SUBLANES)
    gl = jnp.where(sub8 < N_EXPERT_GROUPS, lg[0:SUBLANES], -jnp.inf)
    gmax = jnp.max(gl, axis=0, keepdims=True)
    g_sel = jnp.min(jnp.where(gl == gmax, sub8, big), axis=0, keepdims=True)
    p_group = 1.0 / jnp.sum(jnp.exp(gl - gmax), axis=0, keepdims=True)
    el = lg[SUBLANES:2 * SUBLANES]
    for g in range(1, N_EXPERT_GROUPS):
        el = jnp.where(g_sel == g, lg[(g + 1) * SUBLANES:(g + 2) * SUBLANES], el)
    top1 = jnp.max(el, axis=0, keepdims=True)
    i1 = jnp.min(jnp.where(el == top1, sub8, big), axis=0, keepdims=True)
    el2 = jnp.where(sub8 == i1, -jnp.inf, el)
    top2 = jnp.max(el2, axis=0, keepdims=True)
    i2 = jnp.min(jnp.where(el2 == top2, sub8, big), axis=0, keepdims=True)
    ex = jnp.exp(top2 - top1)
    w1 = p_group * (1.0 / (1.0 + ex))
    w2 = p_group * (ex / (1.0 + ex))
    e1 = g_sel * EXPERTS_PER_GROUP + i1
    e2 = g_sel * EXPERTS_PER_GROUP + i2

    oh1 = (sube == e1)
    oh2 = (sube == e2)
    tri = tri_ref[...]
    cum1 = _dot(jnp.where(oh1, 1.0, 0.0).astype(BF16), tri)
    cum2 = _dot(jnp.where(oh2, 1.0, 0.0).astype(BF16), tri)
    tot1 = jnp.sum(jnp.where(oh1, 1.0, 0.0), axis=1, keepdims=True)
    tot2 = jnp.sum(jnp.where(oh2, 1.0, 0.0), axis=1, keepdims=True)
    base = base_ref[:, 0:1]
    r1 = jnp.sum(jnp.where(oh1, cum1 + base, 0.0), axis=0, keepdims=True)
    r2 = jnp.sum(jnp.where(oh2, cum2 + base + tot1, 0.0), axis=0, keepdims=True)
    base_ref[...] = jnp.broadcast_to(base + tot1 + tot2, base_ref.shape)
    keep_ref[:, pl.ds(off, tr)] = jnp.where(
        sub8 == 0, e1, jnp.where(sub8 == 1, e2, jnp.where(sub8 == 2, r1, jnp.where(
            sub8 == 3, r2, jnp.where(sub8 == 4, w1, jnp.where(sub8 == 5, w2, 0.0))))))


def _route(logits_t, tr, blk):
    rows, t = logits_t.shape
    assert t % tr == 0
    nt = t // tr
    tri = jnp.triu(jnp.ones((tr, tr), BF16), k=1)
    return pl.pallas_call(
        functools.partial(_route_kernel, tr=tr, nt=nt, blk=blk),
        grid=(nt + 1,),
        in_specs=[pl.BlockSpec((rows, tr), lambda i, _n=nt: (0, jnp.minimum(i, _n - 1))),
                  pl.BlockSpec((tr, tr), lambda i: (0, 0))],
        out_specs=[pl.BlockSpec((SUBLANES, t), lambda i: (0, 0)),
                   pl.BlockSpec((SUBLANES, t), lambda i: (0, 0)),
                   pl.BlockSpec((N_EXPERTS, LANES), lambda i: (0, 0))],
        out_shape=[jax.ShapeDtypeStruct((SUBLANES, t), jnp.int32),
                   jax.ShapeDtypeStruct((SUBLANES, t), F32),
                   jax.ShapeDtypeStruct((N_EXPERTS, LANES), jnp.int32)],
        scratch_shapes=[pltpu.VMEM((N_EXPERTS, LANES), F32), pltpu.VMEM((SUBLANES, t), F32)],
        compiler_params=pltpu.CompilerParams(
            dimension_semantics=("arbitrary",), vmem_limit_bytes=VMEM_LIMIT),
        name="route",
    )(logits_t, tri)


def _dispatch_kernel(meta_ref, dest_ref, x1_hbm, xs_hbm, xin, zbuf, lsem, sem, zsem, *,
                     tmd, nsteps, blk, nb):
    i = pl.program_id(0)

    def load_tile(j, slot):
        return pltpu.make_async_copy(x1_hbm.at[pl.ds(pl.multiple_of(j * tmd, tmd), tmd)],
                                     xin.at[slot], lsem.at[slot])

    def wait_step(slot):
        for _ in range(2):
            pltpu.make_async_copy(xin.at[slot], xin.at[slot], sem.at[slot]).wait()

    def pad_rows(e, row_fn, oct_fn):
        start = meta_ref[N_EXPERTS + e] + meta_ref[e]
        start8 = lax.shift_right_logical(start + (SUBLANES - 1), 3)
        lax.fori_loop(start, start8 * SUBLANES, row_fn, 0)
        lax.fori_loop(start8, lax.shift_right_logical(meta_ref[2 * N_EXPERTS + e], 3), oct_fn, 0)

    def zero_row(row):
        return pltpu.make_async_copy(zbuf.at[pl.ds(0, 1)], xs_hbm.at[pl.ds(row, 1)], zsem)

    def zero_oct(o):
        return pltpu.make_async_copy(
            zbuf.at[pl.ds(0, SUBLANES)],
            xs_hbm.at[pl.ds(pl.multiple_of(o * SUBLANES, SUBLANES), SUBLANES)], zsem)

    def zero_block(b):
        return pltpu.make_async_copy(zbuf, xs_hbm.at[pl.ds(pl.multiple_of(b * blk, blk), blk)], zsem)

    @pl.when(i == 0)
    def _():
        zbuf[...] = jnp.zeros_like(zbuf)
        n_used = lax.div(meta_ref[3 * N_EXPERTS - 1], blk)

        def start_row(row, c):
            zero_row(row).start()
            return c

        def wait_row(row, c):
            zero_row(row).wait()
            return c

        def start_oct(o, c):
            zero_oct(o).start()
            return c

        def wait_oct(o, c):
            zero_oct(o).wait()
            return c

        def start_blk(b, c):
            zero_block(b).start()
            return c

        def wait_blk(b, c):
            zero_block(b).wait()
            return c

        for e in range(N_EXPERTS):
            pad_rows(e, start_row, start_oct)
        lax.fori_loop(n_used, nb, start_blk, 0)
        for e in range(N_EXPERTS):
            pad_rows(e, wait_row, wait_oct)
        lax.fori_loop(n_used, nb, wait_blk, 0)

        load_tile(0, 0).start()
        if nsteps > 1:
            load_tile(1, 1).start()

    slot = lax.rem(i, DISPATCH_RING)
    load_tile(i, slot).wait()
    for r in range(tmd):
        for k in range(2):
            pltpu.make_async_copy(xin.at[slot, pl.ds(r, 1)],
                                  xs_hbm.at[pl.ds(dest_ref[0, 0, k * tmd + r], 1)],
                                  sem.at[slot]).start(priority=k)

    @pl.when(i > 0)
    def _():
        wait_step(lax.rem(i + DISPATCH_RING - 1, DISPATCH_RING))

    @pl.when(i + 2 < nsteps)
    def _():
        load_tile(i + 2, lax.rem(i + 2, DISPATCH_RING)).start()

    @pl.when(i == nsteps - 1)
    def _():
        wait_step(slot)


def _dispatch(rows, dest, meta_s, n_slots, tmd, blk):
    t, d = rows.shape
    dt = rows.dtype
    nsteps = t // tmd
    dest3 = dest.reshape(2, nsteps, tmd).transpose(1, 0, 2).reshape(nsteps, 1, 2 * tmd)
    grid_spec = pltpu.PrefetchScalarGridSpec(
        num_scalar_prefetch=1,
        grid=(nsteps,),
        in_specs=[pl.BlockSpec((1, 1, 2 * tmd), lambda i, m: (i, 0, 0), memory_space=pltpu.SMEM),
                  pl.BlockSpec(memory_space=pl.ANY)],
        out_specs=pl.BlockSpec(memory_space=pl.ANY),
        scratch_shapes=[pltpu.VMEM((DISPATCH_RING, tmd, d), dt), pltpu.VMEM((blk, d), dt),
                        pltpu.SemaphoreType.DMA((DISPATCH_RING,)),
                        pltpu.SemaphoreType.DMA((DISPATCH_RING,)), pltpu.SemaphoreType.DMA(())],
    )
    return pl.pallas_call(
        functools.partial(_dispatch_kernel, tmd=tmd, nsteps=nsteps, blk=blk, nb=n_slots // blk),
        grid_spec=grid_spec,
        out_shape=jax.ShapeDtypeStruct((n_slots, d), dt),
        compiler_params=pltpu.CompilerParams(dimension_semantics=("arbitrary",)),
        name="dispatch",
    )(meta_s, dest3, rows)


def _expert_kernel(be_ref, nused_ref, xs_hbm, wg_ref, wu_ref, wd_ref, ys_ref, wgb, wub, wdb,
                   xring, lsem):
    i = pl.program_id(0)
    nused = nused_ref[0]
    changed = jnp.logical_or(i == 0, be_ref[i] != be_ref[jnp.maximum(i - 1, 0)])
    blk = ys_ref.shape[0]
    slot = lax.rem(i, EXPERT_RING)

    def load_block(j):
        s = lax.rem(j, EXPERT_RING)
        return pltpu.make_async_copy(xs_hbm.at[pl.ds(pl.multiple_of(j * blk, blk), blk)],
                                     xring.at[s], lsem.at[s])

    @pl.when(i == 0)
    def _():
        for j in range(EXPERT_RING - 1):
            @pl.when(j < nused)
            def _():
                load_block(j).start()

    @pl.when(i + EXPERT_RING - 1 < nused)
    def _():
        load_block(i + EXPERT_RING - 1).start()

    @pl.when(jnp.logical_and(i < nused, changed))
    def _():
        wgb[...] = wg_ref[0].astype(BF16)
        wub[...] = wu_ref[0].astype(BF16)
        wdb[...] = wd_ref[0].astype(BF16)

    @pl.when(i < nused)
    def _():
        load_block(i).wait()
        cuts = (0, blk // 2, blk) if blk >= 2 * LANES else (0, blk)
        hd = xring.shape[2]

        def unpack(a, b):
            w = xring[slot, a:b, :]
            lo = lax.bitcast_convert_type(lax.shift_left(w, jnp.uint32(16)), F32)
            hi = lax.bitcast_convert_type(jnp.bitwise_and(w, jnp.uint32(0xFFFF0000)), F32)
            return lo.astype(BF16), hi.astype(BF16)

        def proj(h, w_ref):
            return _dot(h[0], w_ref[0:hd, :]) + _dot(h[1], w_ref[hd:2 * hd, :])

        def act(gu):
            return (_silu(gu[0]) * gu[1]).astype(BF16)

        hbs = [unpack(a, b) for a, b in zip(cuts[:-1], cuts[1:])]
        gus = [(proj(hb, wgb), proj(hb, wub)) for hb in hbs]
        for (a, b), gu in zip(zip(cuts[:-1], cuts[1:]), gus):
            ys_ref[a:b, :] = _dot(act(gu), wdb[...])

    @pl.when(i >= nused)
    def _():
        ys_ref[...] = jnp.zeros_like(ys_ref)


def _experts(xs, blk_expert, n_used, w_gate, w_up, w_down, blk):
    n_slots, hd = xs.shape
    d = 2 * hd
    nb = n_slots // blk
    de = w_gate.shape[-1]
    grid_spec = pltpu.PrefetchScalarGridSpec(
        num_scalar_prefetch=2,
        grid=(nb,),
        in_specs=[
            pl.BlockSpec(memory_space=pl.ANY),
            pl.BlockSpec((1, d, de), lambda i, be, nu: (be[i], 0, 0)),
            pl.BlockSpec((1, d, de), lambda i, be, nu: (be[i], 0, 0)),
            pl.BlockSpec((1, de, d), lambda i, be, nu: (be[i], 0, 0)),
        ],
        out_specs=pl.BlockSpec((blk, d), lambda i, be, nu: (i, 0)),
        scratch_shapes=[
            pltpu.VMEM((d, de), BF16),
            pltpu.VMEM((d, de), BF16),
            pltpu.VMEM((de, d), BF16),
            pltpu.VMEM((EXPERT_RING, blk, hd), xs.dtype),
            pltpu.SemaphoreType.DMA((EXPERT_RING,)),
        ],
    )
    return pl.pallas_call(
        _expert_kernel,
        grid_spec=grid_spec,
        out_shape=jax.ShapeDtypeStruct((n_slots, d), F32),
        compiler_params=pltpu.CompilerParams(
            dimension_semantics=("arbitrary",), vmem_limit_bytes=VMEM_LIMIT),
        name="experts",
    )(blk_expert, n_used, xs, w_gate, w_up, w_down)


def _combine_kernel(posc_ref, posn_ref, x1_ref, w_ref, gf_ref, ys_hbm, y_ref, *rest, tq, nsteps):
    bufs, sem = rest[:COMBINE_PARTS], rest[COMBINE_PARTS]
    i = pl.program_id(0)

    def gather(idx_ref, q):
        for r in range(2 * tq):
            pltpu.make_async_copy(ys_hbm.at[pl.ds(idx_ref[0, 0, q * 2 * tq + r], 1)],
                                  bufs[q].at[pl.ds(r, 1)], sem.at[q]).start(priority=r % 2)

    def wait(q):
        pltpu.make_async_copy(bufs[q], bufs[q], sem.at[q]).wait()

    def finish(q):
        rows = slice(q * tq, (q + 1) * tq)
        w = w_ref[rows, :]
        xo = x1_ref[rows, :] + w[:, 0:1] * bufs[q][0:tq, :] + w[:, 1:2] * bufs[q][tq:2 * tq, :]
        y_ref[rows, :] = (xo * lax.rsqrt(jnp.mean(xo * xo, axis=-1, keepdims=True) + EPS)
                          * gf_ref[...])

    @pl.when(i == 0)
    def _():
        for q in range(COMBINE_PARTS - 1):
            gather(posc_ref, q)

    for q in range(COMBINE_PARTS):
        wait(q)
        ahead = q + COMBINE_PARTS - 1
        if ahead < COMBINE_PARTS:
            gather(posc_ref, ahead)
        else:
            gather(posn_ref, ahead - COMBINE_PARTS)
        finish(q)

    @pl.when(i == nsteps - 1)
    def _():
        for q in range(COMBINE_PARTS - 1):
            wait(q)


def _combine(x1_flat, dest, w_col, ys, gf, tm):
    t, d = x1_flat.shape
    nsteps = t // tm
    tq = tm // COMBINE_PARTS
    pos3 = dest.reshape(2, nsteps, COMBINE_PARTS, tq).transpose(1, 2, 0, 3).reshape(nsteps, 1, 2 * tm)
    return pl.pallas_call(
        functools.partial(_combine_kernel, tq=tq, nsteps=nsteps),
        grid=(nsteps,),
        in_specs=[
            pl.BlockSpec((1, 1, 2 * tm), lambda i: (i, 0, 0), memory_space=pltpu.SMEM),
            pl.BlockSpec((1, 1, 2 * tm), lambda i, _n=nsteps: (jnp.minimum(i + 1, _n - 1), 0, 0),
                         memory_space=pltpu.SMEM),
            pl.BlockSpec((tm, d), lambda i: (i, 0)),
            pl.BlockSpec((tm, 2), lambda i: (i, 0)),
            pl.BlockSpec((1, d), lambda i: (0, 0)),
            pl.BlockSpec(memory_space=pl.ANY),
        ],
        out_specs=pl.BlockSpec((tm, d), lambda i: (i, 0)),
        out_shape=jax.ShapeDtypeStruct((t, d), F32),
        scratch_shapes=([pltpu.VMEM((2 * tq, d), F32) for _ in range(COMBINE_PARTS)]
                        + [pltpu.SemaphoreType.DMA((COMBINE_PARTS,))]),
        compiler_params=pltpu.CompilerParams(
            dimension_semantics=("arbitrary",), vmem_limit_bytes=VMEM_LIMIT),
        name="combine",
    )(pos3, pos3, x1_flat, w_col, gf, ys)


def _moe_and_norm(x1_flat, h2p_flat, logits_t, w_gate, w_up, w_down, gf):
    t, d = x1_flat.shape
    tr = _pick(t, (512, 256, 128))
    tm = _pick(t, (512, 256, 128))
    tmd = _pick(t, (512, 256, 128))
    blk = MOE_BLOCK if 2 * t >= 4 * N_EXPERTS * MOE_BLOCK else MOE_BLOCK_SMALL
    dest8, w8, meta = _route(logits_t, tr, blk)
    dest = dest8[0:2]
    nb = (2 * t + blk - 1) // blk + N_EXPERTS
    pend = meta[:, 2]
    blk_expert = jnp.minimum(
        jnp.sum(jnp.arange(nb, dtype=jnp.int32)[:, None] * blk >= pend[None, :], axis=1),
        N_EXPERTS - 1).astype(jnp.int32)
    n_used = (pend[-1:] // blk).astype(jnp.int32)
    xs = _dispatch(h2p_flat, dest, meta[:, 0:3].T.reshape(3 * N_EXPERTS), nb * blk, tmd, blk)
    ys = _experts(xs, blk_expert, n_used, w_gate, w_up, w_down, blk)
    return _combine(x1_flat, dest, w8[0:2].T, ys, gf, tm)


def _prep_layer_weights(norm1_g, w_in, conv_w, conv_b, dt_bias, a_log, d_skip, ssm_norm_g, ln_v_g,
                        ln_v_b, w_spatial, b_spatial, w_out, norm2_g, w_rg, b_rg, w_re, b_re):
    d = w_in.shape[0]
    conv_dim = SSD_WIDTH + 2 * SSD_GROUPS * SSD_STATE
    o_xbc = SSD_WIDTH
    o_dt = o_xbc + conv_dim
    o_gu = o_dt + SSD_HEADS
    o_gv = o_gu + GMLP_WIDTH
    w_dt = w_in[:, o_dt:o_gu]
    w_in_r = jnp.concatenate(
        [w_in[:, :o_xbc], w_in[:, o_xbc:o_dt], w_in[:, o_gu:o_gv], w_in[:, o_gv:],
         jnp.pad(w_dt, ((0, 0), (0, LANES - SSD_HEADS)))], axis=1).astype(BF16)
    rep = lambda v: jnp.repeat(v, SSD_HEAD_DIM)[None, :]
    col = lambda v: v[:, None]
    narrow = lambda v: jnp.pad(v, (0, LANES - SSD_HEADS))[None, :]
    sel = (jnp.arange(LANES)[:, None] == jnp.arange(SSD_WIDTH)[None, :] // SSD_HEAD_DIM).astype(BF16)
    w_r = jnp.concatenate([w_rg, jnp.zeros((d, SUBLANES - N_EXPERT_GROUPS), F32), w_re], axis=1).T
    b_r = jnp.concatenate([b_rg, jnp.zeros((SUBLANES - N_EXPERT_GROUPS,), F32), b_re])[:, None]
    w_r_hi = w_r.astype(BF16)
    w_r_lo = (w_r - w_r_hi.astype(F32)).astype(BF16)
    bsp = jnp.repeat(b_spatial.T, GMLP_GROUP_DIM, axis=1)
    return (norm1_g[None, :], w_in_r, w_dt.T.astype(BF16), conv_w, conv_b[None, :],
            narrow(dt_bias), narrow(a_log), sel, col(dt_bias), col(a_log), rep(d_skip),
            ssm_norm_g[None, :],
            ln_v_g[None, :], ln_v_b[None, :], w_spatial, bsp, w_out.astype(BF16), norm2_g[None, :],
            w_r_hi, w_r_lo, b_r)


def _state_to_pairs(s):
    b = s.shape[0]
    return s.reshape(b, N_PAIRS, 2, SSD_HEAD_DIM, SSD_STATE).transpose(0, 1, 4, 2, 3).reshape(
        b, N_PAIRS, SSD_STATE, 2 * SSD_HEAD_DIM)


def _pairs_to_state(s):
    b = s.shape[0]
    return s.reshape(b, N_PAIRS, SSD_STATE, 2, SSD_HEAD_DIM).transpose(0, 1, 3, 4, 2).reshape(
        b, SSD_HEADS, SSD_HEAD_DIM, SSD_STATE)


def _pick(n, prefs):
    for p in prefs:
        if n % p == 0:
            return p
    return n


def kernel(x_prompt, x_sample, cache_conv, state_ssm, norm1_g, w_in, conv_w, conv_b, dt_bias, a_log, d_skip, ssm_norm_g, ln_v_g, ln_v_b, w_spatial, b_spatial, w_out, norm2_g, w_router_group, b_router_group, w_router_expert, b_router_expert, w_gate, w_up, w_down, final_norm_g):
    depth = w_in.shape[0]
    assert depth == 1, "the combine kernel fuses the final norm, so only a single layer is supported"
    bp, lp, d = x_prompt.shape
    bs, ls, _ = x_sample.shape
    conv_dim = cache_conv.shape[-1]
    ls_pad = -(-ls // GMLP_CHUNK) * GMLP_CHUNK
    tl_p = _pick(lp, (256, 128))
    gf = final_norm_g[None, :]

    yp = x_prompt
    ys = jnp.pad(x_sample, ((0, 0), (0, ls_pad - ls), (0, 0)))
    conv_p, ssm_p, conv_s, ssm_s, v_s = [], [], [], [], []
    for i in range(depth):
        wts = _prep_layer_weights(
            norm1_g[i], w_in[i], conv_w[i], conv_b[i], dt_bias[i], a_log[i], d_skip[i], ssm_norm_g[i],
            ln_v_g[i], ln_v_b[i], w_spatial[i], b_spatial[i], w_out[i], norm2_g[i],
            w_router_group[i], b_router_group[i], w_router_expert[i], b_router_expert[i])
        gfi = gf

        conv0 = jnp.zeros((bp, SUBLANES, conv_dim), F32)
        ssm0 = jnp.zeros((bp, N_PAIRS, LANES, LANES), F32)
        x1, h2p, lt, cpo, hpo = _mixer(yp, conv0, ssm0, wts, tl=tl_p, l_valid=tl_p, emit_gv=False)
        tp = bp * lp
        outp = _moe_and_norm(x1.reshape(tp, d), h2p.reshape(tp, d // 2), lt, w_gate[i], w_up[i],
                             w_down[i], gfi)
        yp = outp.reshape(bp, lp, d)
        conv_p.append(cpo[:, SUBLANES - (CONV_WIDTH - 1):, :])
        ssm_p.append(_pairs_to_state(hpo))

        conv0 = jnp.pad(cache_conv[i], ((0, 0), (SUBLANES - (CONV_WIDTH - 1), 0), (0, 0)))
        x1, h2p, lt, cso, hso, gv = _mixer(ys, conv0, _state_to_pairs(state_ssm[i]), wts,
                                           tl=ls_pad, l_valid=ls, emit_gv=True)
        tsn = bs * ls
        x1v = x1[:, :ls].reshape(tsn, d)
        h2pv = h2p[:, :ls].reshape(tsn, d // 2)
        ltv = lt.reshape(ROUTER_ROWS, bs, ls_pad)[:, :, :ls].reshape(ROUTER_ROWS, tsn)
        outs = _moe_and_norm(x1v, h2pv, ltv, w_gate[i], w_up[i], w_down[i], gfi)
        ys = outs.reshape(bs, ls, d)
        conv_s.append(cso[:, SUBLANES - (CONV_WIDTH - 1):, :])
        ssm_s.append(_pairs_to_state(hso))
        v_s.append(gv[:, :ls])
    return (yp, ys, jnp.stack(conv_p), jnp.stack(ssm_p), jnp.stack(conv_s), jnp.stack(ssm_s),
            jnp.stack(v_s))
```

```python
import functools

import jax
import jax.numpy as jnp
from jax import lax
from jax.experimental import pallas as pl
from jax.experimental.pallas import tpu as pltpu

F32 = jnp.float32
BF16 = jnp.bfloat16
EPS = 1e-6

SSD_HEADS = 8
SSD_HEAD_DIM = 64
SSD_STATE = 128
SSD_GROUPS = 2
SSD_WIDTH = SSD_HEADS * SSD_HEAD_DIM
SSD_CHUNK = 64
CONV_WIDTH = 4
GMLP_GROUPS = 8
GMLP_GROUP_DIM = 64
GMLP_WIDTH = GMLP_GROUPS * GMLP_GROUP_DIM
GMLP_CHUNK = 128
N_EXPERT_GROUPS = 4
EXPERTS_PER_GROUP = 8
N_EXPERTS = N_EXPERT_GROUPS * EXPERTS_PER_GROUP
ROUTER_ROWS = 8 + N_EXPERTS
MOE_BLOCK = 512
MOE_BLOCK_SMALL = 32
DISPATCH_RING = 3
EXPERT_RING = 3
COMBINE_PARTS = 4

LANES = 128
SUBLANES = 8
N_PAIRS = SSD_HEADS // 2
VMEM_LIMIT = 56 * 1024 * 1024

C_Z = 0
C_XBC = C_Z + SSD_WIDTH
C_GU = C_XBC + SSD_WIDTH + 2 * SSD_GROUPS * SSD_STATE
C_GV = C_GU + GMLP_WIDTH
C_DT = C_GV + GMLP_WIDTH
C_END = C_DT + LANES
PROJ_SLAB = 256


def _silu(v):
    return v * (1.0 / (1.0 + jnp.exp(-v)))


def _gelu_tanh(v):
    c = 0.7978845608028654
    hv = 0.5 * v
    return hv + hv * jnp.tanh(v * (c + (c * 0.044715) * (v * v)))


def _softplus(v):
    return jnp.maximum(v, 0.0) + jnp.log1p(jnp.exp(-jnp.abs(v)))


def _dot(a, b):
    return jnp.dot(a, b, preferred_element_type=F32)


def _dot_nt(a, b):
    return lax.dot_general(a, b, (((1,), (1,)), ((), ())), preferred_element_type=F32)


def _dot_tn(a, b):
    return lax.dot_general(a, b, (((0,), (0,)), ((), ())), preferred_element_type=F32)


def _replicate_heads(v, sel_ref):
    hi = v.astype(BF16)
    r1 = v - hi.astype(F32)
    mid = r1.astype(BF16)
    lo = (r1 - mid.astype(F32)).astype(BF16)
    sel = sel_ref[...]
    return _dot(hi, sel) + _dot(mid, sel) + _dot(lo, sel)


def _mixer_kernel(xb_ref, xf_ref, conv0a_ref, conv0b_ref, ssm0a_ref, ssm0b_ref,
                  g1_ref, win_ref, wdt_ref, convw_ref, convb_ref,
                  dtb_n_ref, alog_n_ref, sel_ref, dtb_col_ref, alog_col_ref, dskip_ref, ssmg_ref,
                  lng_ref, lnb_ref, wsp_ref, bsp_ref, wout_ref, g2_ref, wr_hi_ref, wr_lo_ref,
                  br_ref,
                  x1_ref, h2p_ref, logit_ref, convo_ref, ssmo_ref, *rest,
                  tl, nt, l_valid, emit_gv):
    if emit_gv:
        gvo_ref = rest[0]
        rest = rest[1:]
    else:
        gvo_ref = None
    (pa_ref, pb_ref, da_ref, db_ref, xpad_ref, state_ref, wspm_ref, zs_ref, gus_ref, gvs_ref,
     gvf_ref, dtc_ref, acs_ref, dtt_ref) = rest
    s = pl.program_id(0)

    def drain_pieces(pbuf, dbuf, v):
        def p_xbc():
            xpad_ref[v, SUBLANES:SUBLANES + tl, :] = pbuf[:, C_XBC:C_GU]

        def p_dt():
            dtn = _softplus(pbuf[:, C_DT:C_END] + dtb_n_ref[...])
            row = lax.broadcasted_iota(jnp.int32, (tl, LANES), 0)
            if l_valid < tl:
                dtn = jnp.where(row < l_valid, dtn, 0.0)
            acsn = dtn * (-jnp.exp(alog_n_ref[...]))
            row_in_chunk = jnp.bitwise_and(row, SSD_CHUNK - 1)
            sh = 1
            while sh < SSD_CHUNK:
                acsn = acsn + jnp.where(row_in_chunk >= sh, pltpu.roll(acsn, sh, axis=0), 0.0)
                sh *= 2
            dtc_ref[v] = _replicate_heads(dtn, sel_ref)
            acs_ref[v] = _replicate_heads(acsn, sel_ref)
            dtt_ref[v] = _softplus(dbuf[...] + dtb_col_ref[...])

        def p_z():
            zs_ref[v] = _silu(pbuf[:, C_Z:C_XBC])

        def p_gu():
            gus_ref[v] = _gelu_tanh(pbuf[:, C_GU:C_GV])

        def p_gv():
            gv = _gelu_tanh(pbuf[:, C_GV:C_DT])
            mu = jnp.mean(gv, axis=-1, keepdims=True)
            gvc = gv - mu
            var = jnp.mean(gvc * gvc, axis=-1, keepdims=True)
            gv = gvc * lax.rsqrt(var + EPS) * lng_ref[...] + lnb_ref[...]
            if gvo_ref is not None:
                gvf_ref[v] = gv
            gvs_ref[v] = gv.astype(BF16)
        return [p_xbc, p_dt, p_z, p_gu, p_gv]

    def front_pieces(x, pbuf, dbuf):
        hb = []

        def norm():
            h = x * lax.rsqrt(jnp.mean(x * x, axis=-1, keepdims=True) + EPS) * g1_ref[...]
            hb.append(h.astype(BF16))
            dbuf[...] = _dot_nt(wdt_ref[...], hb[0])

        def slab(c0):
            c1 = min(c0 + PROJ_SLAB, C_END)

            def run():
                pbuf[:, c0:c1] = _dot(hb[0], win_ref[:, c0:c1])
            return run
        return [norm] + [slab(c0) for c0 in range(0, C_END, PROJ_SLAB)]

    def front(x, pbuf, dbuf):
        for piece in front_pieces(x, pbuf, dbuf):
            piece()

    @pl.when(s == 0)
    def _():
        r = lax.broadcasted_iota(jnp.int32, (GMLP_CHUNK, GMLP_CHUNK), 0)
        c = lax.broadcasted_iota(jnp.int32, (GMLP_CHUNK, GMLP_CHUNK), 1)
        for gi in range(GMLP_GROUPS):
            wspm_ref[gi] = jnp.where(r >= c, wsp_ref[gi], 0.0).astype(BF16)
        for v in range(2):
            xpad_ref[v, 0:SUBLANES, :] = jnp.zeros((SUBLANES, xpad_ref.shape[2]), F32)
        state_ref[...] = jnp.zeros_like(state_ref)
        front(xb_ref[0], pa_ref, da_ref)
        front(xb_ref[1], pb_ref, db_ref)
        for piece in drain_pieces(pa_ref, da_ref, 0):
            piece()

    bufs = ((pa_ref, da_ref, conv0a_ref, ssm0a_ref), (pb_ref, db_ref, conv0b_ref, ssm0b_ref))
    for u, (pbuf, dbuf, conv0_ref, ssm0_ref) in enumerate(bufs):
        is_first = lax.rem(2 * s + u, nt) == 0
        xpad_ref[u, 0:SUBLANES, :] = jnp.where(is_first, conv0_ref[0], xpad_ref[u, 0:SUBLANES, :])
        for j in range(N_PAIRS):
            state_ref[j] = jnp.where(is_first, ssm0_ref[0, j], state_ref[j])
        if gvo_ref is not None:
            gvo_ref[u] = gvf_ref[u]

        nxt = bufs[1 - u]
        _mixer_back(
            xb_ref[u], front_pieces(xf_ref[u], pbuf, dbuf), drain_pieces(nxt[0], nxt[1], 1 - u),
            convw_ref=convw_ref,
            convb_ref=convb_ref, alog_col_ref=alog_col_ref, dskip_ref=dskip_ref,
            ssmg_ref=ssmg_ref, bsp_ref=bsp_ref, wout_ref=wout_ref, g2_ref=g2_ref,
            wr_hi_ref=wr_hi_ref, wr_lo_ref=wr_lo_ref, br_ref=br_ref, x1_ref=x1_ref,
            h2p_ref=h2p_ref, logit_ref=logit_ref, convo_ref=convo_ref, ssmo_ref=ssmo_ref, xpad_ref=xpad_ref,
            state_ref=state_ref, wspm_ref=wspm_ref, zs_ref=zs_ref, gus_ref=gus_ref,
            gvs_ref=gvs_ref, dtc_ref=dtc_ref, acs_ref=acs_ref, dtt_ref=dtt_ref,
            u=u, tl=tl, l_valid=l_valid)


def _mixer_back(x, fillers, vfillers, *, convw_ref, convb_ref, alog_col_ref, dskip_ref,
                ssmg_ref, bsp_ref, wout_ref, g2_ref, wr_hi_ref, wr_lo_ref, br_ref, x1_ref, h2p_ref,
                logit_ref, convo_ref, ssmo_ref, xpad_ref, state_ref, wspm_ref, zs_ref, gus_ref,
                gvs_ref, dtc_ref, acs_ref, dtt_ref, u, tl, l_valid):
    fillers = list(fillers)
    vfillers = list(vfillers)
    dt = dtc_ref[u]
    acs = acs_ref[u]
    dtt = dtt_ref[u]

    def fill(n):
        for _ in range(min(n, len(fillers))):
            fillers.pop(0)()

    def vfill(n):
        for _ in range(min(n, len(vfillers))):
            vfillers.pop(0)()

    fill(1)
    conv = convb_ref[...]
    for k in range(CONV_WIDTH):
        off = SUBLANES - (CONV_WIDTH - 1) + k
        conv = conv + xpad_ref[u, off:off + tl, :] * convw_ref[k:k + 1, :]
        fill(1)
    xbc = _silu(conv)
    carry = xpad_ref[u, l_valid:l_valid + SUBLANES, :]
    xpad_ref[1 - u, 0:SUBLANES, :] = carry
    convo_ref[u] = carry

    xs = xbc[:, 0:SSD_WIDTH]
    bm = xbc[:, SSD_WIDTH:SSD_WIDTH + SSD_GROUPS * SSD_STATE].astype(BF16)
    cm = xbc[:, SSD_WIDTH + SSD_GROUPS * SSD_STATE:].astype(BF16)

    lane_t = lax.broadcasted_iota(jnp.int32, (SSD_HEADS, tl), 1)
    if l_valid < tl:
        dtt = jnp.where(lane_t < l_valid, dtt, 0.0)
    acst = dtt * (-jnp.exp(alog_col_ref[...]))
    lane_in_chunk = jnp.bitwise_and(lane_t, SSD_CHUNK - 1)
    sh = 1
    while sh < SSD_CHUNK:
        acst = acst + jnp.where(lane_in_chunk >= sh, pltpu.roll(acst, sh, axis=1), 0.0)
        sh *= 2

    lane = lax.broadcasted_iota(jnp.int32, (SSD_CHUNK, LANES), 1)
    rowc = lax.broadcasted_iota(jnp.int32, (SSD_CHUNK, LANES), 0)
    lo_half = lane < SSD_HEAD_DIM
    causal = rowc >= jnp.bitwise_and(lane, SSD_CHUNK - 1)
    lane1 = lax.broadcasted_iota(jnp.int32, (1, LANES), 1)
    lo_half1 = lane1 < SSD_CHUNK

    y_chunks = []
    for c in range(tl // SSD_CHUNK):
        r0 = c * SSD_CHUNK
        v = acst[:, (c // 2) * LANES:(c // 2 + 1) * LANES]
        vr = pltpu.roll(v, SSD_CHUNK, axis=1)
        v_lo, v_hi = (v, vr) if c % 2 == 0 else (vr, v)
        cb2 = []
        for g in range(SSD_GROUPS):
            cg = cm[r0:r0 + SSD_CHUNK, g * SSD_STATE:(g + 1) * SSD_STATE]
            bg = bm[r0:r0 + SSD_CHUNK, g * SSD_STATE:(g + 1) * SSD_STATE]
            cb2.append(_dot_nt(cg, jnp.concatenate([bg, bg], axis=0)))
        y_pairs = []
        for j in range(N_PAIRS):
            g = j // (N_PAIRS // SSD_GROUPS)
            cg = cm[r0:r0 + SSD_CHUNK, g * SSD_STATE:(g + 1) * SSD_STATE]
            bg = bm[r0:r0 + SSD_CHUNK, g * SSD_STATE:(g + 1) * SSD_STATE]
            sl = slice(j * LANES, (j + 1) * LANES)
            col_a = acs[r0:r0 + SSD_CHUNK, sl]
            row_a = jnp.where(lo_half1, v_lo[2 * j:2 * j + 1, :], v_hi[2 * j + 1:2 * j + 2, :])
            decay = jnp.where(causal, jnp.exp(col_a - row_a), 0.0)
            m = (cb2[g] * decay).astype(BF16)
            xdt = xs[r0:r0 + SSD_CHUNK, sl] * dt[r0:r0 + SSD_CHUNK, sl]
            zbd = jnp.concatenate([jnp.where(lo_half, xdt, 0.0), jnp.where(lo_half, 0.0, xdt)],
                                  axis=0).astype(BF16)
            y_diag = _dot(m, zbd)
            st = state_ref[j]
            y_off = _dot(cg, st.astype(BF16)) * jnp.exp(col_a)
            a_last = acs[r0 + SSD_CHUNK - 1:r0 + SSD_CHUNK, sl]
            zdte = (xdt * jnp.exp(a_last - col_a)).astype(BF16)
            state_ref[j] = st * jnp.exp(a_last) + _dot_tn(bg, zdte)
            y_pairs.append(y_diag + y_off)
        y_chunks.append(jnp.concatenate(y_pairs, axis=1))
        fill(1)
    y = jnp.concatenate(y_chunks, axis=0) if len(y_chunks) > 1 else y_chunks[0]
    ssmo_ref[u] = state_ref[...]

    y = y + xs * dskip_ref[...]
    gated = y * zs_ref[u]
    half = SSD_WIDTH // SSD_GROUPS
    outs = []
    for g in range(SSD_GROUPS):
        gg = gated[:, g * half:(g + 1) * half]
        outs.append(gg * lax.rsqrt(jnp.mean(gg * gg, axis=-1, keepdims=True) + EPS))
    ssd_out = jnp.concatenate(outs, axis=1) * ssmg_ref[...]
    fill(1)

    lane_g = lax.broadcasted_iota(jnp.int32, (GMLP_CHUNK, LANES), 1)
    lo_g = lane_g < GMLP_GROUP_DIM
    mixed_chunks = []
    for q in range(tl // GMLP_CHUNK):
        mixed_pairs = []
        for j in range(GMLP_GROUPS // 2):
            vp = gvs_ref[u, q * GMLP_CHUNK:(q + 1) * GMLP_CHUNK, j * LANES:(j + 1) * LANES]
            r_even = _dot(wspm_ref[2 * j], vp)
            r_odd = _dot(wspm_ref[2 * j + 1], vp)
            mixed_pairs.append(jnp.where(lo_g, r_even, r_odd))
        mixed_chunks.append(jnp.concatenate(mixed_pairs, axis=1) + bsp_ref[...])
    mixed = jnp.concatenate(mixed_chunks, axis=0) if len(mixed_chunks) > 1 else mixed_chunks[0]
    gmlp_out = gus_ref[u] * mixed

    merged = jnp.concatenate([ssd_out, gmlp_out], axis=1).astype(BF16)
    x1 = x + _dot(merged, wout_ref[...])
    x1_ref[u] = x1
    fill(len(fillers))

    h2 = x1 * lax.rsqrt(jnp.mean(x1 * x1, axis=-1, keepdims=True) + EPS) * g2_ref[...]
    h2_hi = h2.astype(BF16)
    h2_hif = h2_hi.astype(F32)
    h2_lo = (h2 - h2_hif).astype(BF16)
    bits = lax.bitcast_convert_type(h2_hif, jnp.uint32)
    hd = bits.shape[1] // 2
    h2p_ref[u] = jnp.bitwise_or(lax.shift_right_logical(bits[:, :hd], jnp.uint32(16)),
                                jnp.bitwise_and(bits[:, hd:], jnp.uint32(0xFFFF0000)))
    logit_ref[:, u * tl:(u + 1) * tl] = (
        _dot_nt(wr_hi_ref[...], h2_hi) + _dot_nt(wr_lo_ref[...], h2_hi)
        + _dot_nt(wr_hi_ref[...], h2_lo) + br_ref[...])
    vfill(len(vfillers))


def _full_spec(shape):
    nd = len(shape)
    return pl.BlockSpec(shape, lambda s, _nd=nd: (0,) * _nd)


def _mixer(x, conv0, ssm0, wts, *, tl, l_valid, emit_gv):
    bsz, l, d = x.shape
    nt = l // tl
    g = bsz * nt
    assert l % tl == 0 and tl % GMLP_CHUNK == 0 and l_valid % SUBLANES == 0
    assert (nt == 1 or l_valid == tl) and g % 2 == 0
    steps = g // 2
    tile = lambda f: (lambda s: (f(s), 0, 0))
    seq = lambda f: (lambda s: (f(s) // nt,) + (0,) * 2)
    seq4 = lambda f: (lambda s: (f(s) // nt,) + (0,) * 3)
    in_specs = [
        pl.BlockSpec((2, tl, d), tile(lambda s: s)),
        pl.BlockSpec((2, tl, d), tile(lambda s: jnp.minimum(s + 1, steps - 1))),
        pl.BlockSpec((1, SUBLANES, d), seq(lambda s: 2 * s)),
        pl.BlockSpec((1, SUBLANES, d), seq(lambda s: 2 * s + 1)),
        pl.BlockSpec((1, N_PAIRS, LANES, LANES), seq4(lambda s: 2 * s)),
        pl.BlockSpec((1, N_PAIRS, LANES, LANES), seq4(lambda s: 2 * s + 1)),
    ] + [_full_spec(w.shape) for w in wts]
    out_shape = [
        jax.ShapeDtypeStruct((g, tl, d), F32),
        jax.ShapeDtypeStruct((g, tl, d // 2), jnp.uint32),
        jax.ShapeDtypeStruct((ROUTER_ROWS, g * tl), F32),
        jax.ShapeDtypeStruct((g, SUBLANES, d), F32),
        jax.ShapeDtypeStruct((g, N_PAIRS, LANES, LANES), F32),
    ]
    out_specs = [
        pl.BlockSpec((2, tl, d), lambda s: (s, 0, 0)),
        pl.BlockSpec((2, tl, d // 2), lambda s: (s, 0, 0)),
        pl.BlockSpec((ROUTER_ROWS, 2 * tl), lambda s: (0, s)),
        pl.BlockSpec((2, SUBLANES, d), lambda s: (s, 0, 0)),
        pl.BlockSpec((2, N_PAIRS, LANES, LANES), lambda s: (s, 0, 0, 0)),
    ]
    if emit_gv:
        out_shape.append(jax.ShapeDtypeStruct((g, tl, GMLP_WIDTH), F32))
        out_specs.append(pl.BlockSpec((2, tl, GMLP_WIDTH), lambda s: (s, 0, 0)))
    xt = x.reshape(g, tl, d)
    outs = pl.pallas_call(
        functools.partial(_mixer_kernel, tl=tl, nt=nt, l_valid=l_valid, emit_gv=emit_gv),
        grid=(steps,),
        in_specs=in_specs,
        out_specs=out_specs,
        out_shape=out_shape,
        scratch_shapes=[
            pltpu.VMEM((tl, C_END), F32),
            pltpu.VMEM((tl, C_END), F32),
            pltpu.VMEM((SSD_HEADS, tl), F32),
            pltpu.VMEM((SSD_HEADS, tl), F32),
            pltpu.VMEM((2, tl + SUBLANES, d), F32),
            pltpu.VMEM((N_PAIRS, LANES, LANES), F32),
            pltpu.VMEM((GMLP_GROUPS, GMLP_CHUNK, GMLP_CHUNK), BF16),
            pltpu.VMEM((2, tl, SSD_WIDTH), F32),
            pltpu.VMEM((2, tl, GMLP_WIDTH), F32),
            pltpu.VMEM((2, tl, GMLP_WIDTH), BF16),
            pltpu.VMEM((2, tl, GMLP_WIDTH), F32),
            pltpu.VMEM((2, tl, SSD_WIDTH), F32),
            pltpu.VMEM((2, tl, SSD_WIDTH), F32),
            pltpu.VMEM((2, SSD_HEADS, tl), F32),
        ],
        compiler_params=pltpu.CompilerParams(
            dimension_semantics=("arbitrary",), vmem_limit_bytes=VMEM_LIMIT),
        name="mixer",
    )(xt, xt, conv0, conv0, ssm0, ssm0, *wts)
    x1, h2p, lt, cvo, sso = outs[:5]
    last = slice(nt - 1, None, nt)
    res = (x1.reshape(bsz, l, d), h2p.reshape(bsz, l, d // 2), lt, cvo[last], sso[last])
    if emit_gv:
        res += (outs[5].reshape(bsz, l, GMLP_WIDTH),)
    return res


def _route_kernel(logit_ref, tri_ref, dest_ref, w_ref, meta_ref, base_ref, keep_ref, *, tr, nt, blk):
    i = pl.program_id(0)
    sub8 = lax.broadcasted_iota(jnp.int32, (SUBLANES, tr), 0).astype(F32)
    sube = lax.broadcasted_iota(jnp.int32, (N_EXPERTS, tr), 0).astype(F32)

    @pl.when(i == 0)
    def _():
        base_ref[...] = jnp.zeros_like(base_ref)

    @pl.when(i < nt)
    def _():
        _route_pass0(logit_ref, tri_ref, base_ref, keep_ref, pl.multiple_of(i * tr, tr), sub8, sube, tr)

    @pl.when(i == nt)
    def _():
        counts = base_ref[...]
        padded = jnp.floor((counts + float(blk - 1)) * (1.0 / blk)) * float(blk)
        sub_e = lax.broadcasted_iota(jnp.int32, (N_EXPERTS, LANES), 0)
        pend = padded
        sh = 1
        while sh < N_EXPERTS:
            pend = pend + jnp.where(sub_e >= sh, pltpu.roll(pend, sh, axis=0), 0.0)
            sh *= 2
        pstart = pend - padded
        lane_e = lax.broadcasted_iota(jnp.int32, (N_EXPERTS, LANES), 1)
        meta = jnp.where(lane_e == 0, counts, jnp.where(lane_e == 1, pstart,
                         jnp.where(lane_e == 2, pend, 0.0)))
        meta_ref[...] = meta.astype(jnp.int32)

        def chunk(c, carry):
            sl = pl.ds(pl.multiple_of(c * tr, tr), tr)
            kept = keep_ref[:, sl]
            ps1 = jnp.sum(jnp.where(sube == kept[0:1], pstart[:, 0:1], 0.0), axis=0, keepdims=True)
            ps2 = jnp.sum(jnp.where(sube == kept[1:2], pstart[:, 0:1], 0.0), axis=0, keepdims=True)
            dest = jnp.where(sub8 == 0, ps1 + kept[2:3],
                             jnp.where(sub8 == 1, ps2 + kept[3:4], 0.0))
            dest_ref[:, sl] = dest.astype(jnp.int32)
            w_ref[:, sl] = jnp.where(sub8 == 0, kept[4:5], jnp.where(sub8 == 1, kept[5:6], 0.0))
            return carry
        lax.fori_loop(0, nt, chunk, 0)


def _route_pass0(logit_ref, tri_ref, base_ref, keep_ref, off, sub8, sube, tr):
    lg = logit_ref[...]
    big = float(SUBLANES)
    gl = jnp.where(sub8 < N_EXPERT_GROUPS, lg[0:SUBLANES], -jnp.inf)
    gmax = jnp.max(gl, axis=0, keepdims=True)
    g_sel = jnp.min(jnp.where(gl == gmax, sub8, big), axis=0, keepdims=True)
    p_group = 1.0 / jnp.sum(jnp.exp(gl - gmax), axis=0, keepdims=True)
    el = lg[SUBLANES:2 * SUBLANES]
    for g in range(1, N_EXPERT_GROUPS):
        el = jnp.where(g_sel == g, lg[(g + 1) * SUBLANES:(g + 2) * SUBLANES], el)
    top1 = jnp.max(el, axis=0, keepdims=True)
    i1 = jnp.min(jnp.where(el == top1, sub8, big), axis=0, keepdims=True)
    el2 = jnp.where(sub8 == i1, -jnp.inf, el)
    top2 = jnp.max(el2, axis=0, keepdims=True)
    i2 = jnp.min(jnp.where(el2 == top2, sub8, big), axis=0, keepdims=True)
    ex = jnp.exp(top2 - top1)
    w1 = p_group * (1.0 / (1.0 + ex))
    w2 = p_group * (ex / (1.0 + ex))
    e1 = g_sel * EXPERTS_PER_GROUP + i1
    e2 = g_sel * EXPERTS_PER_GROUP + i2

    oh1 = (sube == e1)
    oh2 = (sube == e2)
    tri = tri_ref[...]
    cum1 = _dot(jnp.where(oh1, 1.0, 0.0).astype(BF16), tri)
    cum2 = _dot(jnp.where(oh2, 1.0, 0.0).astype(BF16), tri)
    tot1 = jnp.sum(jnp.where(oh1, 1.0, 0.0), axis=1, keepdims=True)
    tot2 = jnp.sum(jnp.where(oh2, 1.0, 0.0), axis=1, keepdims=True)
    base = base_ref[:, 0:1]
    r1 = jnp.sum(jnp.where(oh1, cum1 + base, 0.0), axis=0, keepdims=True)
    r2 = jnp.sum(jnp.where(oh2, cum2 + base + tot1, 0.0), axis=0, keepdims=True)
    base_ref[...] = jnp.broadcast_to(base + tot1 + tot2, base_ref.shape)
    keep_ref[:, pl.ds(off, tr)] = jnp.where(
        sub8 == 0, e1, jnp.where(sub8 == 1, e2, jnp.where(sub8 == 2, r1, jnp.where(
            sub8 == 3, r2, jnp.where(sub8 == 4, w1, jnp.where(sub8 == 5, w2, 0.0))))))


def _route(logits_t, tr, blk):
    rows, t = logits_t.shape
    assert t % tr == 0
    nt = t // tr
    tri = jnp.triu(jnp.ones((tr, tr), BF16), k=1)
    return pl.pallas_call(
        functools.partial(_route_kernel, tr=tr, nt=nt, blk=blk),
        grid=(nt + 1,),
        in_specs=[pl.BlockSpec((rows, tr), lambda i, _n=nt: (0, jnp.minimum(i, _n - 1))),
                  pl.BlockSpec((tr, tr), lambda i: (0, 0))],
        out_specs=[pl.BlockSpec((SUBLANES, t), lambda i: (0, 0)),
                   pl.BlockSpec((SUBLANES, t), lambda i: (0, 0)),
                   pl.BlockSpec((N_EXPERTS, LANES), lambda i: (0, 0))],
        out_shape=[jax.ShapeDtypeStruct((SUBLANES, t), jnp.int32),
                   jax.ShapeDtypeStruct((SUBLANES, t), F32),
                   jax.ShapeDtypeStruct((N_EXPERTS, LANES), jnp.int32)],
        scratch_shapes=[pltpu.VMEM((N_EXPERTS, LANES), F32), pltpu.VMEM((SUBLANES, t), F32)],
        compiler_params=pltpu.CompilerParams(
            dimension_semantics=("arbitrary",), vmem_limit_bytes=VMEM_LIMIT),
        name="route",
    )(logits_t, tri)


def _dispatch_kernel(meta_ref, dest_ref, x1_hbm, xs_hbm, xin, zbuf, lsem, sem, zsem, *,
                     tmd, nsteps, blk, nb):
    i = pl.program_id(0)

    def load_tile(j, slot):
        return pltpu.make_async_copy(x1_hbm.at[pl.ds(pl.multiple_of(j * tmd, tmd), tmd)],
                                     xin.at[slot], lsem.at[slot])

    def wait_step(slot):
        for _ in range(2):
            pltpu.make_async_copy(xin.at[slot], xin.at[slot], sem.at[slot]).wait()

    def pad_rows(e, row_fn, oct_fn):
        start = meta_ref[N_EXPERTS + e] + meta_ref[e]
        start8 = lax.shift_right_logical(start + (SUBLANES - 1), 3)
        lax.fori_loop(start, start8 * SUBLANES, row_fn, 0)
        lax.fori_loop(start8, lax.shift_right_logical(meta_ref[2 * N_EXPERTS + e], 3), oct_fn, 0)

    def zero_row(row):
        return pltpu.make_async_copy(zbuf.at[pl.ds(0, 1)], xs_hbm.at[pl.ds(row, 1)], zsem)

    def zero_oct(o):
        return pltpu.make_async_copy(
            zbuf.at[pl.ds(0, SUBLANES)],
            xs_hbm.at[pl.ds(pl.multiple_of(o * SUBLANES, SUBLANES), SUBLANES)], zsem)

    def zero_block(b):
        return pltpu.make_async_copy(zbuf, xs_hbm.at[pl.ds(pl.multiple_of(b * blk, blk), blk)], zsem)

    @pl.when(i == 0)
    def _():
        zbuf[...] = jnp.zeros_like(zbuf)
        n_used = lax.div(meta_ref[3 * N_EXPERTS - 1], blk)

        def start_row(row, c):
            zero_row(row).start()
            return c

        def wait_row(row, c):
            zero_row(row).wait()
            return c

        def start_oct(o, c):
            zero_oct(o).start()
            return c

        def wait_oct(o, c):
            zero_oct(o).wait()
            return c

        def start_blk(b, c):
            zero_block(b).start()
            return c

        def wait_blk(b, c):
            zero_block(b).wait()
            return c

        for e in range(N_EXPERTS):
            pad_rows(e, start_row, start_oct)
        lax.fori_loop(n_used, nb, start_blk, 0)
        for e in range(N_EXPERTS):
            pad_rows(e, wait_row, wait_oct)
        lax.fori_loop(n_used, nb, wait_blk, 0)

        load_tile(0, 0).start()
        if nsteps > 1:
            load_tile(1, 1).start()

    slot = lax.rem(i, DISPATCH_RING)
    load_tile(i, slot).wait()
    for r in range(tmd):
        for k in range(2):
            pltpu.make_async_copy(xin.at[slot, pl.ds(r, 1)],
                                  xs_hbm.at[pl.ds(dest_ref[0, 0, k * tmd + r], 1)],
                                  sem.at[slot]).start(priority=k)

    @pl.when(i > 0)
    def _():
        wait_step(lax.rem(i + DISPATCH_RING - 1, DISPATCH_RING))

    @pl.when(i + 2 < nsteps)
    def _():
        load_tile(i + 2, lax.rem(i + 2, DISPATCH_RING)).start()

    @pl.when(i == nsteps - 1)
    def _():
        wait_step(slot)


def _dispatch(rows, dest, meta_s, n_slots, tmd, blk):
    t, d = rows.shape
    dt = rows.dtype
    nsteps = t // tmd
    dest3 = dest.reshape(2, nsteps, tmd).transpose(1, 0, 2).reshape(nsteps, 1, 2 * tmd)
    grid_spec = pltpu.PrefetchScalarGridSpec(
        num_scalar_prefetch=1,
        grid=(nsteps,),
        in_specs=[pl.BlockSpec((1, 1, 2 * tmd), lambda i, m: (i, 0, 0), memory_space=pltpu.SMEM),
                  pl.BlockSpec(memory_space=pl.ANY)],
        out_specs=pl.BlockSpec(memory_space=pl.ANY),
        scratch_shapes=[pltpu.VMEM((DISPATCH_RING, tmd, d), dt), pltpu.VMEM((blk, d), dt),
                        pltpu.SemaphoreType.DMA((DISPATCH_RING,)),
                        pltpu.SemaphoreType.DMA((DISPATCH_RING,)), pltpu.SemaphoreType.DMA(())],
    )
    return pl.pallas_call(
        functools.partial(_dispatch_kernel, tmd=tmd, nsteps=nsteps, blk=blk, nb=n_slots // blk),
        grid_spec=grid_spec,
        out_shape=jax.ShapeDtypeStruct((n_slots, d), dt),
        compiler_params=pltpu.CompilerParams(dimension_semantics=("arbitrary",)),
        name="dispatch",
    )(meta_s, dest3, rows)


def _expert_kernel(be_ref, nused_ref, xs_hbm, wg_ref, wu_ref, wd_ref, ys_hbm, wgb, wub, wdb,
                   xring, lsem, yring, ssem, *, nb):
    i = pl.program_id(0)
    nused = nused_ref[0]
    changed = jnp.logical_or(i == 0, be_ref[i] != be_ref[jnp.maximum(i - 1, 0)])
    blk = yring.shape[1]
    slot = lax.rem(i, EXPERT_RING)
    ys_ref = yring.at[slot]

    def store_block(j):
        s = lax.rem(j, EXPERT_RING)
        return pltpu.make_async_copy(yring.at[s],
                                     ys_hbm.at[pl.ds(pl.multiple_of(j * blk, blk), blk)], ssem.at[s])

    @pl.when(i >= EXPERT_RING - 1)
    def _():
        store_block(i - (EXPERT_RING - 1)).wait()

    def load_block(j):
        s = lax.rem(j, EXPERT_RING)
        return pltpu.make_async_copy(xs_hbm.at[pl.ds(pl.multiple_of(j * blk, blk), blk)],
                                     xring.at[s], lsem.at[s])

    @pl.when(i == 0)
    def _():
        for j in range(EXPERT_RING - 1):
            @pl.when(j < nused)
            def _():
                load_block(j).start()

    @pl.when(i + EXPERT_RING - 1 < nused)
    def _():
        load_block(i + EXPERT_RING - 1).start()

    @pl.when(jnp.logical_and(i < nused, changed))
    def _():
        wgb[...] = wg_ref[0].astype(BF16)
        wub[...] = wu_ref[0].astype(BF16)
        wdb[...] = wd_ref[0].astype(BF16)

    @pl.when(i < nused)
    def _():
        load_block(i).wait()
        cuts = (0, blk // 2, blk) if blk >= 2 * LANES else (0, blk)
        hd = xring.shape[2]

        def unpack(a, b):
            w = xring[slot, a:b, :]
            lo = lax.bitcast_convert_type(lax.shift_left(w, jnp.uint32(16)), F32)
            hi = lax.bitcast_convert_type(jnp.bitwise_and(w, jnp.uint32(0xFFFF0000)), F32)
            return lo.astype(BF16), hi.astype(BF16)

        def proj(h, w_ref):
            return _dot(h[0], w_ref[0:hd, :]) + _dot(h[1], w_ref[hd:2 * hd, :])

        def act(gu):
            return (_silu(gu[0]) * gu[1]).astype(BF16)

        hbs = [unpack(a, b) for a, b in zip(cuts[:-1], cuts[1:])]
        gus = [(proj(hb, wgb), proj(hb, wub)) for hb in hbs]
        for (a, b), gu in zip(zip(cuts[:-1], cuts[1:]), gus):
            ys_ref[a:b, :] = _dot(act(gu), wdb[...])

    @pl.when(i >= nused)
    def _():
        ys_ref[...] = jnp.zeros(ys_ref.shape, F32)

    store_block(i).start()

    @pl.when(i == nb - 1)
    def _():
        for back in range(min(EXPERT_RING - 1, nb) - 1, -1, -1):
            store_block(i - back).wait()


def _experts(xs, blk_expert, n_used, w_gate, w_up, w_down, blk):
    n_slots, hd = xs.shape
    d = 2 * hd
    nb = n_slots // blk
    de = w_gate.shape[-1]
    grid_spec = pltpu.PrefetchScalarGridSpec(
        num_scalar_prefetch=2,
        grid=(nb,),
        in_specs=[
            pl.BlockSpec(memory_space=pl.ANY),
            pl.BlockSpec((1, d, de), lambda i, be, nu: (be[i], 0, 0)),
            pl.BlockSpec((1, d, de), lambda i, be, nu: (be[i], 0, 0)),
            pl.BlockSpec((1, de, d), lambda i, be, nu: (be[i], 0, 0)),
        ],
        out_specs=pl.BlockSpec(memory_space=pl.ANY),
        scratch_shapes=[
            pltpu.VMEM((d, de), BF16),
            pltpu.VMEM((d, de), BF16),
            pltpu.VMEM((de, d), BF16),
            pltpu.VMEM((EXPERT_RING, blk, hd), xs.dtype),
            pltpu.SemaphoreType.DMA((EXPERT_RING,)),
            pltpu.VMEM((EXPERT_RING, blk, d), F32),
            pltpu.SemaphoreType.DMA((EXPERT_RING,)),
        ],
    )
    return pl.pallas_call(
        functools.partial(_expert_kernel, nb=nb),
        grid_spec=grid_spec,
        out_shape=jax.ShapeDtypeStruct((n_slots, d), F32),
        compiler_params=pltpu.CompilerParams(
            dimension_semantics=("arbitrary",), vmem_limit_bytes=VMEM_LIMIT),
        name="experts",
    )(blk_expert, n_used, xs, w_gate, w_up, w_down)


def _combine_kernel(posc_ref, posn_ref, x1_ref, w_ref, gf_ref, ys_hbm, y_ref, *rest, tq, nsteps):
    bufs, sem = rest[:COMBINE_PARTS], rest[COMBINE_PARTS]
    i = pl.program_id(0)

    def gather(idx_ref, q):
        for r in range(2 * tq):
            pltpu.make_async_copy(ys_hbm.at[pl.ds(idx_ref[0, 0, q * 2 * tq + r], 1)],
                                  bufs[q].at[pl.ds(r, 1)], sem.at[q]).start(priority=r % 2)

    def wait(q):
        pltpu.make_async_copy(bufs[q], bufs[q], sem.at[q]).wait()

    def finish(q):
        rows = slice(q * tq, (q + 1) * tq)
        w = w_ref[rows, :]
        xo = x1_ref[rows, :] + w[:, 0:1] * bufs[q][0:tq, :] + w[:, 1:2] * bufs[q][tq:2 * tq, :]
        y_ref[rows, :] = (xo * lax.rsqrt(jnp.mean(xo * xo, axis=-1, keepdims=True) + EPS)
                          * gf_ref[...])

    @pl.when(i == 0)
    def _():
        for q in range(COMBINE_PARTS - 1):
            gather(posc_ref, q)

    for q in range(COMBINE_PARTS):
        wait(q)
        ahead = q + COMBINE_PARTS - 1
        if ahead < COMBINE_PARTS:
            gather(posc_ref, ahead)
        else:
            gather(posn_ref, ahead - COMBINE_PARTS)
        finish(q)

    @pl.when(i == nsteps - 1)
    def _():
        for q in range(COMBINE_PARTS - 1):
            wait(q)


def _combine(x1_flat, dest, w_col, ys, gf, tm):
    t, d = x1_flat.shape
    nsteps = t // tm
    tq = tm // COMBINE_PARTS
    pos3 = dest.reshape(2, nsteps, COMBINE_PARTS, tq).transpose(1, 2, 0, 3).reshape(nsteps, 1, 2 * tm)
    return pl.pallas_call(
        functools.partial(_combine_kernel, tq=tq, nsteps=nsteps),
        grid=(nsteps,),
        in_specs=[
            pl.BlockSpec((1, 1, 2 * tm), lambda i: (i, 0, 0), memory_space=pltpu.SMEM),
            pl.BlockSpec((1, 1, 2 * tm), lambda i, _n=nsteps: (jnp.minimum(i + 1, _n - 1), 0, 0),
                         memory_space=pltpu.SMEM),
            pl.BlockSpec((tm, d), lambda i: (i, 0)),
            pl.BlockSpec((tm, 2), lambda i: (i, 0)),
            pl.BlockSpec((1, d), lambda i: (0, 0)),
            pl.BlockSpec(memory_space=pl.ANY),
        ],
        out_specs=pl.BlockSpec((tm, d), lambda i: (i, 0)),
        out_shape=jax.ShapeDtypeStruct((t, d), F32),
        scratch_shapes=([pltpu.VMEM((2 * tq, d), F32) for _ in range(COMBINE_PARTS)]
                        + [pltpu.SemaphoreType.DMA((COMBINE_PARTS,))]),
        compiler_params=pltpu.CompilerParams(
            dimension_semantics=("arbitrary",), vmem_limit_bytes=VMEM_LIMIT),
        name="combine",
    )(pos3, pos3, x1_flat, w_col, gf, ys)


def _moe_and_norm(x1_flat, h2p_flat, logits_t, w_gate, w_up, w_down, gf):
    t, d = x1_flat.shape
    tr = _pick(t, (512, 256, 128))
    tm = _pick(t, (512, 256, 128))
    tmd = _pick(t, (512, 256, 128))
    blk = MOE_BLOCK if 2 * t >= 4 * N_EXPERTS * MOE_BLOCK else MOE_BLOCK_SMALL
    dest8, w8, meta = _route(logits_t, tr, blk)
    dest = dest8[0:2]
    nb = (2 * t + blk - 1) // blk + N_EXPERTS
    pend = meta[:, 2]
    blk_expert = jnp.minimum(
        jnp.sum(jnp.arange(nb, dtype=jnp.int32)[:, None] * blk >= pend[None, :], axis=1),
        N_EXPERTS - 1).astype(jnp.int32)
    n_used = (pend[-1:] // blk).astype(jnp.int32)
    xs = _dispatch(h2p_flat, dest, meta[:, 0:3].T.reshape(3 * N_EXPERTS), nb * blk, tmd, blk)
    ys = _experts(xs, blk_expert, n_used, w_gate, w_up, w_down, blk)
    return _combine(x1_flat, dest, w8[0:2].T, ys, gf, tm)


def _prep_layer_weights(norm1_g, w_in, conv_w, conv_b, dt_bias, a_log, d_skip, ssm_norm_g, ln_v_g,
                        ln_v_b, w_spatial, b_spatial, w_out, norm2_g, w_rg, b_rg, w_re, b_re):
    d = w_in.shape[0]
    conv_dim = SSD_WIDTH + 2 * SSD_GROUPS * SSD_STATE
    o_xbc = SSD_WIDTH
    o_dt = o_xbc + conv_dim
    o_gu = o_dt + SSD_HEADS
    o_gv = o_gu + GMLP_WIDTH
    w_dt = w_in[:, o_dt:o_gu]
    w_in_r = jnp.concatenate(
        [w_in[:, :o_xbc], w_in[:, o_xbc:o_dt], w_in[:, o_gu:o_gv], w_in[:, o_gv:],
         jnp.pad(w_dt, ((0, 0), (0, LANES - SSD_HEADS)))], axis=1).astype(BF16)
    rep = lambda v: jnp.repeat(v, SSD_HEAD_DIM)[None, :]
    col = lambda v: v[:, None]
    narrow = lambda v: jnp.pad(v, (0, LANES - SSD_HEADS))[None, :]
    sel = (jnp.arange(LANES)[:, None] == jnp.arange(SSD_WIDTH)[None, :] // SSD_HEAD_DIM).astype(BF16)
    w_r = jnp.concatenate([w_rg, jnp.zeros((d, SUBLANES - N_EXPERT_GROUPS), F32), w_re], axis=1).T
    b_r = jnp.concatenate([b_rg, jnp.zeros((SUBLANES - N_EXPERT_GROUPS,), F32), b_re])[:, None]
    w_r_hi = w_r.astype(BF16)
    w_r_lo = (w_r - w_r_hi.astype(F32)).astype(BF16)
    bsp = jnp.repeat(b_spatial.T, GMLP_GROUP_DIM, axis=1)
    return (norm1_g[None, :], w_in_r, w_dt.T.astype(BF16), conv_w, conv_b[None, :],
            narrow(dt_bias), narrow(a_log), sel, col(dt_bias), col(a_log), rep(d_skip),
            ssm_norm_g[None, :],
            ln_v_g[None, :], ln_v_b[None, :], w_spatial, bsp, w_out.astype(BF16), norm2_g[None, :],
            w_r_hi, w_r_lo, b_r)


def _state_to_pairs(s):
    b = s.shape[0]
    return s.reshape(b, N_PAIRS, 2, SSD_HEAD_DIM, SSD_STATE).transpose(0, 1, 4, 2, 3).reshape(
        b, N_PAIRS, SSD_STATE, 2 * SSD_HEAD_DIM)


def _pairs_to_state(s):
    b = s.shape[0]
    return s.reshape(b, N_PAIRS, SSD_STATE, 2, SSD_HEAD_DIM).transpose(0, 1, 3, 4, 2).reshape(
        b, SSD_HEADS, SSD_HEAD_DIM, SSD_STATE)


def _pick(n, prefs):
    for p in prefs:
        if n % p == 0:
            return p
    return n


def kernel(x_prompt, x_sample, cache_conv, state_ssm, norm1_g, w_in, conv_w, conv_b, dt_bias, a_log, d_skip, ssm_norm_g, ln_v_g, ln_v_b, w_spatial, b_spatial, w_out, norm2_g, w_router_group, b_router_group, w_router_expert, b_router_expert, w_gate, w_up, w_down, final_norm_g):
    depth = w_in.shape[0]
    assert depth == 1, "the combine kernel fuses the final norm, so only a single layer is supported"
    bp, lp, d = x_prompt.shape
    bs, ls, _ = x_sample.shape
    conv_dim = cache_conv.shape[-1]
    ls_pad = -(-ls // GMLP_CHUNK) * GMLP_CHUNK
    tl_p = _pick(lp, (256, 128))
    gf = final_norm_g[None, :]

    yp = x_prompt
    ys = jnp.pad(x_sample, ((0, 0), (0, ls_pad - ls), (0, 0)))
    conv_p, ssm_p, conv_s, ssm_s, v_s = [], [], [], [], []
    for i in range(depth):
        wts = _prep_layer_weights(
            norm1_g[i], w_in[i], conv_w[i], conv_b[i], dt_bias[i], a_log[i], d_skip[i], ssm_norm_g[i],
            ln_v_g[i], ln_v_b[i], w_spatial[i], b_spatial[i], w_out[i], norm2_g[i],
            w_router_group[i], b_router_group[i], w_router_expert[i], b_router_expert[i])
        gfi = gf

        conv0 = jnp.zeros((bp, SUBLANES, conv_dim), F32)
        ssm0 = jnp.zeros((bp, N_PAIRS, LANES, LANES), F32)
        x1, h2p, lt, cpo, hpo = _mixer(yp, conv0, ssm0, wts, tl=tl_p, l_valid=tl_p, emit_gv=False)
        tp = bp * lp
        outp = _moe_and_norm(x1.reshape(tp, d), h2p.reshape(tp, d // 2), lt, w_gate[i], w_up[i],
                             w_down[i], gfi)
        yp = outp.reshape(bp, lp, d)
        conv_p.append(cpo[:, SUBLANES - (CONV_WIDTH - 1):, :])
        ssm_p.append(_pairs_to_state(hpo))

        conv0 = jnp.pad(cache_conv[i], ((0, 0), (SUBLANES - (CONV_WIDTH - 1), 0), (0, 0)))
        x1, h2p, lt, cso, hso, gv = _mixer(ys, conv0, _state_to_pairs(state_ssm[i]), wts,
                                           tl=ls_pad, l_valid=ls, emit_gv=True)
        tsn = bs * ls
        x1v = x1[:, :ls].reshape(tsn, d)
        h2pv = h2p[:, :ls].reshape(tsn, d // 2)
        ltv = lt.reshape(ROUTER_ROWS, bs, ls_pad)[:, :, :ls].reshape(ROUTER_ROWS, tsn)
        outs = _moe_and_norm(x1v, h2pv, ltv, w_gate[i], w_up[i], w_down[i], gfi)
        ys = outs.reshape(bs, ls, d)
        conv_s.append(cso[:, SUBLANES - (CONV_WIDTH - 1):, :])
        ssm_s.append(_pairs_to_state(hso))
        v_s.append(gv[:, :ls])
    return (yp, ys, jnp.stack(conv_p), jnp.stack(ssm_p), jnp.stack(conv_s), jnp.stack(ssm_s),
            jnp.stack(v_s))
```

```python
import functools

import jax
import jax.numpy as jnp
from jax import lax
from jax.experimental import pallas as pl
from jax.experimental.pallas import tpu as pltpu

F32 = jnp.float32
BF16 = jnp.bfloat16
EPS = 1e-6

SSD_HEADS = 8
SSD_HEAD_DIM = 64
SSD_STATE = 128
SSD_GROUPS = 2
SSD_WIDTH = SSD_HEADS * SSD_HEAD_DIM
SSD_CHUNK = 64
CONV_WIDTH = 4
GMLP_GROUPS = 8
GMLP_GROUP_DIM = 64
GMLP_WIDTH = GMLP_GROUPS * GMLP_GROUP_DIM
GMLP_CHUNK = 128
N_EXPERT_GROUPS = 4
EXPERTS_PER_GROUP = 8
N_EXPERTS = N_EXPERT_GROUPS * EXPERTS_PER_GROUP
ROUTER_ROWS = 8 + N_EXPERTS
MOE_BLOCK = 512
MOE_BLOCK_SMALL = 32
DISPATCH_RING = 3
EXPERT_RING = 3
COMBINE_PARTS = 4

LANES = 128
SUBLANES = 8
N_PAIRS = SSD_HEADS // 2
VMEM_LIMIT = 56 * 1024 * 1024

C_Z = 0
C_XBC = C_Z + SSD_WIDTH
C_GU = C_XBC + SSD_WIDTH + 2 * SSD_GROUPS * SSD_STATE
C_GV = C_GU + GMLP_WIDTH
C_DT = C_GV + GMLP_WIDTH
C_END = C_DT + LANES
PROJ_SLAB = 256


def _silu(v):
    return v * (1.0 / (1.0 + jnp.exp(-v)))


def _gelu_tanh(v):
    c = 0.7978845608028654
    hv = 0.5 * v
    return hv + hv * jnp.tanh(v * (c + (c * 0.044715) * (v * v)))


def _softplus(v):
    return jnp.maximum(v, 0.0) + jnp.log1p(jnp.exp(-jnp.abs(v)))


def _dot(a, b):
    return jnp.dot(a, b, preferred_element_type=F32)


def _dot_nt(a, b):
    return lax.dot_general(a, b, (((1,), (1,)), ((), ())), preferred_element_type=F32)


def _dot_tn(a, b):
    return lax.dot_general(a, b, (((0,), (0,)), ((), ())), preferred_element_type=F32)


def _replicate_heads(v, sel_ref):
    hi = v.astype(BF16)
    r1 = v - hi.astype(F32)
    mid = r1.astype(BF16)
    lo = (r1 - mid.astype(F32)).astype(BF16)
    sel = sel_ref[...]
    return _dot(hi, sel) + _dot(mid, sel) + _dot(lo, sel)


def _mixer_kernel(xb_ref, xf_ref, conv0a_ref, conv0b_ref, ssm0a_ref, ssm0b_ref,
                  g1_ref, win_ref, wdt_ref, convw_ref, convb_ref,
                  dtb_n_ref, alog_n_ref, sel_ref, dtb_col_ref, alog_col_ref, dskip_ref, ssmg_ref,
                  lng_ref, lnb_ref, wsp_ref, bsp_ref, wout_ref, g2_ref, wr_hi_ref, wr_lo_ref,
                  br_ref,
                  x1_ref, h2p_ref, logit_ref, convo_ref, ssmo_ref, *rest,
                  tl, nt, l_valid, emit_gv):
    if emit_gv:
        gvo_ref = rest[0]
        rest = rest[1:]
    else:
        gvo_ref = None
    (pa_ref, pb_ref, da_ref, db_ref, xpad_ref, state_ref, wspm_ref, zs_ref, gus_ref, gvs_ref,
     gvf_ref, dtc_ref, acs_ref, dtt_ref) = rest
    s = pl.program_id(0)

    def drain_pieces(pbuf, dbuf, v):
        def p_xbc():
            xpad_ref[v, SUBLANES:SUBLANES + tl, :] = pbuf[:, C_XBC:C_GU]

        def p_dt():
            dtn = _softplus(pbuf[:, C_DT:C_END] + dtb_n_ref[...])
            row = lax.broadcasted_iota(jnp.int32, (tl, LANES), 0)
            if l_valid < tl:
                dtn = jnp.where(row < l_valid, dtn, 0.0)
            acsn = dtn * (-jnp.exp(alog_n_ref[...]))
            row_in_chunk = jnp.bitwise_and(row, SSD_CHUNK - 1)
            sh = 1
            while sh < SSD_CHUNK:
                acsn = acsn + jnp.where(row_in_chunk >= sh, pltpu.roll(acsn, sh, axis=0), 0.0)
                sh *= 2
            dtc_ref[v] = _replicate_heads(dtn, sel_ref)
            acs_ref[v] = _replicate_heads(acsn, sel_ref)
            dtt_ref[v] = _softplus(dbuf[...] + dtb_col_ref[...])

        def p_z():
            zs_ref[v] = _silu(pbuf[:, C_Z:C_XBC])

        def p_gu():
            gus_ref[v] = _gelu_tanh(pbuf[:, C_GU:C_GV])

        def p_gv():
            gv = _gelu_tanh(pbuf[:, C_GV:C_DT])
            mu = jnp.mean(gv, axis=-1, keepdims=True)
            gvc = gv - mu
            var = jnp.mean(gvc * gvc, axis=-1, keepdims=True)
            gv = gvc * lax.rsqrt(var + EPS) * lng_ref[...] + lnb_ref[...]
            if gvo_ref is not None:
                gvf_ref[v] = gv
            gvs_ref[v] = gv.astype(BF16)
        return [p_xbc, p_dt, p_z, p_gu, p_gv]

    def front_pieces(x, pbuf, dbuf):
        hb = []

        def norm():
            h = x * lax.rsqrt(jnp.mean(x * x, axis=-1, keepdims=True) + EPS) * g1_ref[...]
            hb.append(h.astype(BF16))
            dbuf[...] = _dot_nt(wdt_ref[...], hb[0])

        def slab(c0):
            c1 = min(c0 + PROJ_SLAB, C_END)

            def run():
                pbuf[:, c0:c1] = _dot(hb[0], win_ref[:, c0:c1])
            return run
        return [norm] + [slab(c0) for c0 in range(0, C_END, PROJ_SLAB)]

    def front(x, pbuf, dbuf):
        for piece in front_pieces(x, pbuf, dbuf):
            piece()

    @pl.when(s == 0)
    def _():
        r = lax.broadcasted_iota(jnp.int32, (GMLP_CHUNK, GMLP_CHUNK), 0)
        c = lax.broadcasted_iota(jnp.int32, (GMLP_CHUNK, GMLP_CHUNK), 1)
        for gi in range(GMLP_GROUPS):
            wspm_ref[gi] = jnp.where(r >= c, wsp_ref[gi], 0.0).astype(BF16)
        for v in range(2):
            xpad_ref[v, 0:SUBLANES, :] = jnp.zeros((SUBLANES, xpad_ref.shape[2]), F32)
        state_ref[...] = jnp.zeros_like(state_ref)
        front(xb_ref[0], pa_ref, da_ref)
        front(xb_ref[1], pb_ref, db_ref)
        for piece in drain_pieces(pa_ref, da_ref, 0):
            piece()

    bufs = ((pa_ref, da_ref, conv0a_ref, ssm0a_ref), (pb_ref, db_ref, conv0b_ref, ssm0b_ref))
    for u, (pbuf, dbuf, conv0_ref, ssm0_ref) in enumerate(bufs):
        is_first = lax.rem(2 * s + u, nt) == 0
        xpad_ref[u, 0:SUBLANES, :] = jnp.where(is_first, conv0_ref[0], xpad_ref[u, 0:SUBLANES, :])
        for j in range(N_PAIRS):
            state_ref[j] = jnp.where(is_first, ssm0_ref[0, j], state_ref[j])
        if gvo_ref is not None:
            gvo_ref[u] = gvf_ref[u]

        nxt = bufs[1 - u]
        _mixer_back(
            xb_ref[u], front_pieces(xf_ref[u], pbuf, dbuf), drain_pieces(nxt[0], nxt[1], 1 - u),
            convw_ref=convw_ref,
            convb_ref=convb_ref, alog_col_ref=alog_col_ref, dskip_ref=dskip_ref,
            ssmg_ref=ssmg_ref, bsp_ref=bsp_ref, wout_ref=wout_ref, g2_ref=g2_ref,
            wr_hi_ref=wr_hi_ref, wr_lo_ref=wr_lo_ref, br_ref=br_ref, x1_ref=x1_ref,
            h2p_ref=h2p_ref, logit_ref=logit_ref, convo_ref=convo_ref, ssmo_ref=ssmo_ref, xpad_ref=xpad_ref,
            state_ref=state_ref, wspm_ref=wspm_ref, zs_ref=zs_ref, gus_ref=gus_ref,
            gvs_ref=gvs_ref, dtc_ref=dtc_ref, acs_ref=acs_ref, dtt_ref=dtt_ref,
            u=u, tl=tl, l_valid=l_valid)


def _mixer_back(x, fillers, vfillers, *, convw_ref, convb_ref, alog_col_ref, dskip_ref,
                ssmg_ref, bsp_ref, wout_ref, g2_ref, wr_hi_ref, wr_lo_ref, br_ref, x1_ref, h2p_ref,
                logit_ref, convo_ref, ssmo_ref, xpad_ref, state_ref, wspm_ref, zs_ref, gus_ref,
                gvs_ref, dtc_ref, acs_ref, dtt_ref, u, tl, l_valid):
    fillers = list(fillers)
    vfillers = list(vfillers)
    dt = dtc_ref[u]
    acs = acs_ref[u]
    dtt = dtt_ref[u]

    def fill(n):
        for _ in range(min(n, len(fillers))):
            fillers.pop(0)()

    def vfill(n):
        for _ in range(min(n, len(vfillers))):
            vfillers.pop(0)()

    fill(1)
    conv = convb_ref[...]
    for k in range(CONV_WIDTH):
        off = SUBLANES - (CONV_WIDTH - 1) + k
        conv = conv + xpad_ref[u, off:off + tl, :] * convw_ref[k:k + 1, :]
        fill(1)
    xbc = _silu(conv)
    carry = xpad_ref[u, l_valid:l_valid + SUBLANES, :]
    xpad_ref[1 - u, 0:SUBLANES, :] = carry
    convo_ref[u] = carry

    xs = xbc[:, 0:SSD_WIDTH]
    bm = xbc[:, SSD_WIDTH:SSD_WIDTH + SSD_GROUPS * SSD_STATE].astype(BF16)
    cm = xbc[:, SSD_WIDTH + SSD_GROUPS * SSD_STATE:].astype(BF16)

    lane_t = lax.broadcasted_iota(jnp.int32, (SSD_HEADS, tl), 1)
    if l_valid < tl:
        dtt = jnp.where(lane_t < l_valid, dtt, 0.0)
    acst = dtt * (-jnp.exp(alog_col_ref[...]))
    lane_in_chunk = jnp.bitwise_and(lane_t, SSD_CHUNK - 1)
    sh = 1
    while sh < SSD_CHUNK:
        acst = acst + jnp.where(lane_in_chunk >= sh, pltpu.roll(acst, sh, axis=1), 0.0)
        sh *= 2

    lane = lax.broadcasted_iota(jnp.int32, (SSD_CHUNK, LANES), 1)
    rowc = lax.broadcasted_iota(jnp.int32, (SSD_CHUNK, LANES), 0)
    lo_half = lane < SSD_HEAD_DIM
    causal = rowc >= jnp.bitwise_and(lane, SSD_CHUNK - 1)
    lane1 = lax.broadcasted_iota(jnp.int32, (1, LANES), 1)
    lo_half1 = lane1 < SSD_CHUNK

    y_chunks = []
    for c in range(tl // SSD_CHUNK):
        r0 = c * SSD_CHUNK
        v = acst[:, (c // 2) * LANES:(c // 2 + 1) * LANES]
        vr = pltpu.roll(v, SSD_CHUNK, axis=1)
        v_lo, v_hi = (v, vr) if c % 2 == 0 else (vr, v)
        cb2 = []
        for g in range(SSD_GROUPS):
            cg = cm[r0:r0 + SSD_CHUNK, g * SSD_STATE:(g + 1) * SSD_STATE]
            bg = bm[r0:r0 + SSD_CHUNK, g * SSD_STATE:(g + 1) * SSD_STATE]
            cb2.append(_dot_nt(cg, jnp.concatenate([bg, bg], axis=0)))
        y_pairs = []
        for j in range(N_PAIRS):
            g = j // (N_PAIRS // SSD_GROUPS)
            cg = cm[r0:r0 + SSD_CHUNK, g * SSD_STATE:(g + 1) * SSD_STATE]
            bg = bm[r0:r0 + SSD_CHUNK, g * SSD_STATE:(g + 1) * SSD_STATE]
            sl = slice(j * LANES, (j + 1) * LANES)
            col_a = acs[r0:r0 + SSD_CHUNK, sl]
            row_a = jnp.where(lo_half1, v_lo[2 * j:2 * j + 1, :], v_hi[2 * j + 1:2 * j + 2, :])
            decay = jnp.where(causal, jnp.exp(col_a - row_a), 0.0)
            m = (cb2[g] * decay).astype(BF16)
            xdt = xs[r0:r0 + SSD_CHUNK, sl] * dt[r0:r0 + SSD_CHUNK, sl]
            zbd = jnp.concatenate([jnp.where(lo_half, xdt, 0.0), jnp.where(lo_half, 0.0, xdt)],
                                  axis=0).astype(BF16)
            y_diag = _dot(m, zbd)
            st = state_ref[j]
            y_off = _dot(cg, st.astype(BF16)) * jnp.exp(col_a)
            a_last = acs[r0 + SSD_CHUNK - 1:r0 + SSD_CHUNK, sl]
            zdte = (xdt * jnp.exp(a_last - col_a)).astype(BF16)
            state_ref[j] = st * jnp.exp(a_last) + _dot_tn(bg, zdte)
            y_pairs.append(y_diag + y_off)
        y_chunks.append(jnp.concatenate(y_pairs, axis=1))
        fill(1)
    y = jnp.concatenate(y_chunks, axis=0) if len(y_chunks) > 1 else y_chunks[0]
    ssmo_ref[u] = state_ref[...]

    y = y + xs * dskip_ref[...]
    gated = y * zs_ref[u]
    half = SSD_WIDTH // SSD_GROUPS
    outs = []
    for g in range(SSD_GROUPS):
        gg = gated[:, g * half:(g + 1) * half]
        outs.append(gg * lax.rsqrt(jnp.mean(gg * gg, axis=-1, keepdims=True) + EPS))
    ssd_out = jnp.concatenate(outs, axis=1) * ssmg_ref[...]
    fill(1)

    lane_g = lax.broadcasted_iota(jnp.int32, (GMLP_CHUNK, LANES), 1)
    lo_g = lane_g < GMLP_GROUP_DIM
    mixed_chunks = []
    for q in range(tl // GMLP_CHUNK):
        mixed_pairs = []
        for j in range(GMLP_GROUPS // 2):
            vp = gvs_ref[u, q * GMLP_CHUNK:(q + 1) * GMLP_CHUNK, j * LANES:(j + 1) * LANES]
            r_even = _dot(wspm_ref[2 * j], vp)
            r_odd = _dot(wspm_ref[2 * j + 1], vp)
            mixed_pairs.append(jnp.where(lo_g, r_even, r_odd))
        mixed_chunks.append(jnp.concatenate(mixed_pairs, axis=1) + bsp_ref[...])
    mixed = jnp.concatenate(mixed_chunks, axis=0) if len(mixed_chunks) > 1 else mixed_chunks[0]
    gmlp_out = gus_ref[u] * mixed

    merged = jnp.concatenate([ssd_out, gmlp_out], axis=1).astype(BF16)
    x1 = x + _dot(merged, wout_ref[...])
    x1_ref[u] = x1
    fill(len(fillers))

    h2 = x1 * lax.rsqrt(jnp.mean(x1 * x1, axis=-1, keepdims=True) + EPS) * g2_ref[...]
    h2_hi = h2.astype(BF16)
    h2_hif = h2_hi.astype(F32)
    h2_lo = (h2 - h2_hif).astype(BF16)
    bits = lax.bitcast_convert_type(h2_hif, jnp.uint32)
    hd = bits.shape[1] // 2
    h2p_ref[u] = jnp.bitwise_or(lax.shift_right_logical(bits[:, :hd], jnp.uint32(16)),
                                jnp.bitwise_and(bits[:, hd:], jnp.uint32(0xFFFF0000)))
    logit_ref[:, u * tl:(u + 1) * tl] = (
        _dot_nt(wr_hi_ref[...], h2_hi) + _dot_nt(wr_lo_ref[...], h2_hi)
        + _dot_nt(wr_hi_ref[...], h2_lo) + br_ref[...])
    vfill(len(vfillers))


def _full_spec(shape):
    nd = len(shape)
    return pl.BlockSpec(shape, lambda s, _nd=nd: (0,) * _nd)


def _mixer(x, conv0, ssm0, wts, *, tl, l_valid, emit_gv):
    bsz, l, d = x.shape
    nt = l // tl
    g = bsz * nt
    assert l % tl == 0 and tl % GMLP_CHUNK == 0 and l_valid % SUBLANES == 0
    assert (nt == 1 or l_valid == tl) and g % 2 == 0
    steps = g // 2
    tile = lambda f: (lambda s: (f(s), 0, 0))
    seq = lambda f: (lambda s: (f(s) // nt,) + (0,) * 2)
    seq4 = lambda f: (lambda s: (f(s) // nt,) + (0,) * 3)
    in_specs = [
        pl.BlockSpec((2, tl, d), tile(lambda s: s)),
        pl.BlockSpec((2, tl, d), tile(lambda s: jnp.minimum(s + 1, steps - 1))),
        pl.BlockSpec((1, SUBLANES, d), seq(lambda s: 2 * s)),
        pl.BlockSpec((1, SUBLANES, d), seq(lambda s: 2 * s + 1)),
        pl.BlockSpec((1, N_PAIRS, LANES, LANES), seq4(lambda s: 2 * s)),
        pl.BlockSpec((1, N_PAIRS, LANES, LANES), seq4(lambda s: 2 * s + 1)),
    ] + [_full_spec(w.shape) for w in wts]
    out_shape = [
        jax.ShapeDtypeStruct((g, tl, d), F32),
        jax.ShapeDtypeStruct((g, tl, d // 2), jnp.uint32),
        jax.ShapeDtypeStruct((ROUTER_ROWS, g * tl), F32),
        jax.ShapeDtypeStruct((g, SUBLANES, d), F32),
        jax.ShapeDtypeStruct((g, N_PAIRS, LANES, LANES), F32),
    ]
    out_specs = [
        pl.BlockSpec((2, tl, d), lambda s: (s, 0, 0)),
        pl.BlockSpec((2, tl, d // 2), lambda s: (s, 0, 0)),
        pl.BlockSpec((ROUTER_ROWS, 2 * tl), lambda s: (0, s)),
        pl.BlockSpec((2, SUBLANES, d), lambda s: (s, 0, 0)),
        pl.BlockSpec((2, N_PAIRS, LANES, LANES), lambda s: (s, 0, 0, 0)),
    ]
    if emit_gv:
        out_shape.append(jax.ShapeDtypeStruct((g, tl, GMLP_WIDTH), F32))
        out_specs.append(pl.BlockSpec((2, tl, GMLP_WIDTH), lambda s: (s, 0, 0)))
    xt = x.reshape(g, tl, d)
    outs = pl.pallas_call(
        functools.partial(_mixer_kernel, tl=tl, nt=nt, l_valid=l_valid, emit_gv=emit_gv),
        grid=(steps,),
        in_specs=in_specs,
        out_specs=out_specs,
        out_shape=out_shape,
        scratch_shapes=[
            pltpu.VMEM((tl, C_END), F32),
            pltpu.VMEM((tl, C_END), F32),
            pltpu.VMEM((SSD_HEADS, tl), F32),
            pltpu.VMEM((SSD_HEADS, tl), F32),
            pltpu.VMEM((2, tl + SUBLANES, d), F32),
            pltpu.VMEM((N_PAIRS, LANES, LANES), F32),
            pltpu.VMEM((GMLP_GROUPS, GMLP_CHUNK, GMLP_CHUNK), BF16),
            pltpu.VMEM((2, tl, SSD_WIDTH), F32),
            pltpu.VMEM((2, tl, GMLP_WIDTH), F32),
            pltpu.VMEM((2, tl, GMLP_WIDTH), BF16),
            pltpu.VMEM((2, tl, GMLP_WIDTH), F32),
            pltpu.VMEM((2, tl, SSD_WIDTH), F32),
            pltpu.VMEM((2, tl, SSD_WIDTH), F32),
            pltpu.VMEM((2, SSD_HEADS, tl), F32),
        ],
        compiler_params=pltpu.CompilerParams(
            dimension_semantics=("arbitrary",), vmem_limit_bytes=VMEM_LIMIT),
        name="mixer",
    )(xt, xt, conv0, conv0, ssm0, ssm0, *wts)
    x1, h2p, lt, cvo, sso = outs[:5]
    last = slice(nt - 1, None, nt)
    res = (x1.reshape(bsz, l, d), h2p.reshape(bsz, l, d // 2), lt, cvo[last], sso[last])
    if emit_gv:
        res += (outs[5].reshape(bsz, l, GMLP_WIDTH),)
    return res


def _route_kernel(logit_ref, tri_ref, dest_ref, w_ref, meta_ref, base_ref, keep_ref, *, tr, nt, blk):
    i = pl.program_id(0)
    sub8 = lax.broadcasted_iota(jnp.int32, (SUBLANES, tr), 0).astype(F32)
    sube = lax.broadcasted_iota(jnp.int32, (N_EXPERTS, tr), 0).astype(F32)

    @pl.when(i == 0)
    def _():
        base_ref[...] = jnp.zeros_like(base_ref)

    @pl.when(i < nt)
    def _():
        _route_pass0(logit_ref, tri_ref, base_ref, keep_ref, pl.multiple_of(i * tr, tr), sub8, sube, tr)

    @pl.when(i == nt)
    def _():
        counts = base_ref[...]
        padded = jnp.floor((counts + float(blk - 1)) * (1.0 / blk)) * float(blk)
        sub_e = lax.broadcasted_iota(jnp.int32, (N_EXPERTS, LANES), 0)
        pend = padded
        sh = 1
        while sh < N_EXPERTS:
            pend = pend + jnp.where(sub_e >= sh, pltpu.roll(pend, sh, axis=0), 0.0)
            sh *= 2
        pstart = pend - padded
        lane_e = lax.broadcasted_iota(jnp.int32, (N_EXPERTS, LANES), 1)
        meta = jnp.where(lane_e == 0, counts, jnp.where(lane_e == 1, pstart,
                         jnp.where(lane_e == 2, pend, 0.0)))
        meta_ref[...] = meta.astype(jnp.int32)

        def chunk(c, carry):
            sl = pl.ds(pl.multiple_of(c * tr, tr), tr)
            kept = keep_ref[:, sl]
            ps1 = jnp.sum(jnp.where(sube == kept[0:1], pstart[:, 0:1], 0.0), axis=0, keepdims=True)
            ps2 = jnp.sum(jnp.where(sube == kept[1:2], pstart[:, 0:1], 0.0), axis=0, keepdims=True)
            dest = jnp.where(sub8 == 0, ps1 + kept[2:3],
                             jnp.where(sub8 == 1, ps2 + kept[3:4], 0.0))
            dest_ref[:, sl] = dest.astype(jnp.int32)
            w_ref[:, sl] = jnp.where(sub8 == 0, kept[4:5], jnp.where(sub8 == 1, kept[5:6], 0.0))
            return carry
        lax.fori_loop(0, nt, chunk, 0)


def _route_pass0(logit_ref, tri_ref, base_ref, keep_ref, off, sub8, sube, tr):
    lg = logit_ref[...]
    big = float(SUBLANES)
    gl = jnp.where(sub8 < N_EXPERT_GROUPS, lg[0:SUBLANES], -jnp.inf)
    gmax = jnp.max(gl, axis=0, keepdims=True)
    g_sel = jnp.min(jnp.where(gl == gmax, sub8, big), axis=0, keepdims=True)
    p_group = 1.0 / jnp.sum(jnp.exp(gl - gmax), axis=0, keepdims=True)
    el = lg[SUBLANES:2 * SUBLANES]
    for g in range(1, N_EXPERT_GROUPS):
        el = jnp.where(g_sel == g, lg[(g + 1) * SUBLANES:(g + 2) * SUBLANES], el)
    top1 = jnp.max(el, axis=0, keepdims=True)
    i1 = jnp.min(jnp.where(el == top1, sub8, big), axis=0, keepdims=True)
    el2 = jnp.where(sub8 == i1, -jnp.inf, el)
    top2 = jnp.max(el2, axis=0, keepdims=True)
    i2 = jnp.min(jnp.where(el2 == top2, sub8, big), axis=0, keepdims=True)
    ex = jnp.exp(top2 - top1)
    w1 = p_group * (1.0 / (1.0 + ex))
    w2 = p_group * (ex / (1.0 + ex))
    e1 = g_sel * EXPERTS_PER_GROUP + i1
    e2 = g_sel * EXPERTS_PER_GROUP + i2

    oh1 = (sube == e1)
    oh2 = (sube == e2)
    tri = tri_ref[...]
    cum1 = _dot(jnp.where(oh1, 1.0, 0.0).astype(BF16), tri)
    cum2 = _dot(jnp.where(oh2, 1.0, 0.0).astype(BF16), tri)
    tot1 = jnp.sum(jnp.where(oh1, 1.0, 0.0), axis=1, keepdims=True)
    tot2 = jnp.sum(jnp.where(oh2, 1.0, 0.0), axis=1, keepdims=True)
    base = base_ref[:, 0:1]
    r1 = jnp.sum(jnp.where(oh1, cum1 + base, 0.0), axis=0, keepdims=True)
    r2 = jnp.sum(jnp.where(oh2, cum2 + base + tot1, 0.0), axis=0, keepdims=True)
    base_ref[...] = jnp.broadcast_to(base + tot1 + tot2, base_ref.shape)
    keep_ref[:, pl.ds(off, tr)] = jnp.where(
        sub8 == 0, e1, jnp.where(sub8 == 1, e2, jnp.where(sub8 == 2, r1, jnp.where(
            sub8 == 3, r2, jnp.where(sub8 == 4, w1, jnp.where(sub8 == 5, w2, 0.0))))))


def _route(logits_t, tr, blk):
    rows, t = logits_t.shape
    assert t % tr == 0
    nt = t // tr
    tri = jnp.triu(jnp.ones((tr, tr), BF16), k=1)
    return pl.pallas_call(
        functools.partial(_route_kernel, tr=tr, nt=nt, blk=blk),
        grid=(nt + 1,),
        in_specs=[pl.BlockSpec((rows, tr), lambda i, _n=nt: (0, jnp.minimum(i, _n - 1))),
                  pl.BlockSpec((tr, tr), lambda i: (0, 0))],
        out_specs=[pl.BlockSpec((SUBLANES, t), lambda i: (0, 0)),
                   pl.BlockSpec((SUBLANES, t), lambda i: (0, 0)),
                   pl.BlockSpec((N_EXPERTS, LANES), lambda i: (0, 0))],
        out_shape=[jax.ShapeDtypeStruct((SUBLANES, t), jnp.int32),
                   jax.ShapeDtypeStruct((SUBLANES, t), F32),
                   jax.ShapeDtypeStruct((N_EXPERTS, LANES), jnp.int32)],
        scratch_shapes=[pltpu.VMEM((N_EXPERTS, LANES), F32), pltpu.VMEM((SUBLANES, t), F32)],
        compiler_params=pltpu.CompilerParams(
            dimension_semantics=("arbitrary",), vmem_limit_bytes=VMEM_LIMIT),
        name="route",
    )(logits_t, tri)


def _dispatch_kernel(meta_ref, dest_ref, x1_hbm, xs_hbm, xin, zbuf, lsem, sem, zsem, *,
                     tmd, nsteps, blk, nb):
    i = pl.program_id(0)

    def load_tile(j, slot):
        return pltpu.make_async_copy(x1_hbm.at[pl.ds(pl.multiple_of(j * tmd, tmd), tmd)],
                                     xin.at[slot], lsem.at[slot])

    def wait_step(slot):
        for _ in range(2):
            pltpu.make_async_copy(xin.at[slot], xin.at[slot], sem.at[slot]).wait()

    def pad_rows(e, row_fn, oct_fn):
        start = meta_ref[N_EXPERTS + e] + meta_ref[e]
        start8 = lax.shift_right_logical(start + (SUBLANES - 1), 3)
        lax.fori_loop(start, start8 * SUBLANES, row_fn, 0)
        lax.fori_loop(start8, lax.shift_right_logical(meta_ref[2 * N_EXPERTS + e], 3), oct_fn, 0)

    def zero_row(row):
        return pltpu.make_async_copy(zbuf.at[pl.ds(0, 1)], xs_hbm.at[pl.ds(row, 1)], zsem)

    def zero_oct(o):
        return pltpu.make_async_copy(
            zbuf.at[pl.ds(0, SUBLANES)],
            xs_hbm.at[pl.ds(pl.multiple_of(o * SUBLANES, SUBLANES), SUBLANES)], zsem)

    def zero_block(b):
        return pltpu.make_async_copy(zbuf, xs_hbm.at[pl.ds(pl.multiple_of(b * blk, blk), blk)], zsem)

    def zero_fill(start):
        n_used = lax.div(meta_ref[3 * N_EXPERTS - 1], blk)

        def go(copy):
            def body(idx, c):
                if start:
                    copy(idx).start()
                else:
                    copy(idx).wait()
                return c
            return body

        for e in range(N_EXPERTS):
            pad_rows(e, go(zero_row), go(zero_oct))
        lax.fori_loop(n_used, nb, go(zero_block), 0)

    @pl.when(i == 0)
    def _():
        zbuf[...] = jnp.zeros_like(zbuf)
        zero_fill(True)
        load_tile(0, 0).start()
        if nsteps > 1:
            load_tile(1, 1).start()

    slot = lax.rem(i, DISPATCH_RING)
    load_tile(i, slot).wait()
    for r in range(tmd):
        for k in range(2):
            pltpu.make_async_copy(xin.at[slot, pl.ds(r, 1)],
                                  xs_hbm.at[pl.ds(dest_ref[0, 0, k * tmd + r], 1)],
                                  sem.at[slot]).start(priority=k)

    @pl.when(i > 0)
    def _():
        wait_step(lax.rem(i + DISPATCH_RING - 1, DISPATCH_RING))

    @pl.when(i + 2 < nsteps)
    def _():
        load_tile(i + 2, lax.rem(i + 2, DISPATCH_RING)).start()

    @pl.when(i == nsteps - 1)
    def _():
        wait_step(slot)
        zero_fill(False)


def _dispatch(rows, dest, meta_s, n_slots, tmd, blk):
    t, d = rows.shape
    dt = rows.dtype
    nsteps = t // tmd
    dest3 = dest.reshape(2, nsteps, tmd).transpose(1, 0, 2).reshape(nsteps, 1, 2 * tmd)
    grid_spec = pltpu.PrefetchScalarGridSpec(
        num_scalar_prefetch=1,
        grid=(nsteps,),
        in_specs=[pl.BlockSpec((1, 1, 2 * tmd), lambda i, m: (i, 0, 0), memory_space=pltpu.SMEM),
                  pl.BlockSpec(memory_space=pl.ANY)],
        out_specs=pl.BlockSpec(memory_space=pl.ANY),
        scratch_shapes=[pltpu.VMEM((DISPATCH_RING, tmd, d), dt), pltpu.VMEM((blk, d), dt),
                        pltpu.SemaphoreType.DMA((DISPATCH_RING,)),
                        pltpu.SemaphoreType.DMA((DISPATCH_RING,)), pltpu.SemaphoreType.DMA(())],
    )
    return pl.pallas_call(
        functools.partial(_dispatch_kernel, tmd=tmd, nsteps=nsteps, blk=blk, nb=n_slots // blk),
        grid_spec=grid_spec,
        out_shape=jax.ShapeDtypeStruct((n_slots, d), dt),
        compiler_params=pltpu.CompilerParams(dimension_semantics=("arbitrary",)),
        name="dispatch",
    )(meta_s, dest3, rows)


def _expert_kernel(be_ref, nused_ref, xs_hbm, wg_ref, wu_ref, wd_ref, ys_ref, wgb, wub, wdb,
                   xring, lsem):
    i = pl.program_id(0)
    nused = nused_ref[0]
    changed = jnp.logical_or(i == 0, be_ref[i] != be_ref[jnp.maximum(i - 1, 0)])
    blk = ys_ref.shape[0]
    slot = lax.rem(i, EXPERT_RING)

    def load_block(j):
        s = lax.rem(j, EXPERT_RING)
        return pltpu.make_async_copy(xs_hbm.at[pl.ds(pl.multiple_of(j * blk, blk), blk)],
                                     xring.at[s], lsem.at[s])

    @pl.when(i == 0)
    def _():
        for j in range(EXPERT_RING - 1):
            @pl.when(j < nused)
            def _():
                load_block(j).start()

    @pl.when(i + EXPERT_RING - 1 < nused)
    def _():
        load_block(i + EXPERT_RING - 1).start()

    @pl.when(jnp.logical_and(i < nused, changed))
    def _():
        wgb[...] = wg_ref[0].astype(BF16)
        wub[...] = wu_ref[0].astype(BF16)
        wdb[...] = wd_ref[0].astype(BF16)

    @pl.when(i < nused)
    def _():
        load_block(i).wait()
        cuts = (0, blk // 2, blk) if blk >= 2 * LANES else (0, blk)
        hd = xring.shape[2]

        def unpack(a, b):
            w = xring[slot, a:b, :]
            lo = lax.bitcast_convert_type(lax.shift_left(w, jnp.uint32(16)), F32)
            hi = lax.bitcast_convert_type(jnp.bitwise_and(w, jnp.uint32(0xFFFF0000)), F32)
            return lo.astype(BF16), hi.astype(BF16)

        def proj(h, w_ref):
            return _dot(h[0], w_ref[0:hd, :]) + _dot(h[1], w_ref[hd:2 * hd, :])

        def act(gu):
            return (_silu(gu[0]) * gu[1]).astype(BF16)

        hbs = [unpack(a, b) for a, b in zip(cuts[:-1], cuts[1:])]
        gus = [(proj(hb, wgb), proj(hb, wub)) for hb in hbs]
        for (a, b), gu in zip(zip(cuts[:-1], cuts[1:]), gus):
            ys_ref[a:b, :] = _dot(act(gu), wdb[...])

    @pl.when(i >= nused)
    def _():
        ys_ref[...] = jnp.zeros_like(ys_ref)


def _experts(xs, blk_expert, n_used, w_gate, w_up, w_down, blk):
    n_slots, hd = xs.shape
    d = 2 * hd
    nb = n_slots // blk
    de = w_gate.shape[-1]
    grid_spec = pltpu.PrefetchScalarGridSpec(
        num_scalar_prefetch=2,
        grid=(nb,),
        in_specs=[
            pl.BlockSpec(memory_space=pl.ANY),
            pl.BlockSpec((1, d, de), lambda i, be, nu: (be[i], 0, 0)),
            pl.BlockSpec((1, d, de), lambda i, be, nu: (be[i], 0, 0)),
            pl.BlockSpec((1, de, d), lambda i, be, nu: (be[i], 0, 0)),
        ],
        out_specs=pl.BlockSpec((blk, d), lambda i, be, nu: (i, 0)),
        scratch_shapes=[
            pltpu.VMEM((d, de), BF16),
            pltpu.VMEM((d, de), BF16),
            pltpu.VMEM((de, d), BF16),
            pltpu.VMEM((EXPERT_RING, blk, hd), xs.dtype),
            pltpu.SemaphoreType.DMA((EXPERT_RING,)),
        ],
    )
    return pl.pallas_call(
        _expert_kernel,
        grid_spec=grid_spec,
        out_shape=jax.ShapeDtypeStruct((n_slots, d), F32),
        compiler_params=pltpu.CompilerParams(
            dimension_semantics=("arbitrary",), vmem_limit_bytes=VMEM_LIMIT),
        name="experts",
    )(blk_expert, n_used, xs, w_gate, w_up, w_down)


def _combine_kernel(posc_ref, posn_ref, x1_ref, w_ref, gf_ref, ys_hbm, y_ref, *rest, tq, nsteps):
    bufs, sem = rest[:COMBINE_PARTS], rest[COMBINE_PARTS]
    i = pl.program_id(0)

    def gather(idx_ref, q):
        for r in range(2 * tq):
            pltpu.make_async_copy(ys_hbm.at[pl.ds(idx_ref[0, 0, q * 2 * tq + r], 1)],
                                  bufs[q].at[pl.ds(r, 1)], sem.at[q]).start(priority=r % 2)

    def wait(q):
        pltpu.make_async_copy(bufs[q], bufs[q], sem.at[q]).wait()

    def finish(q):
        rows = slice(q * tq, (q + 1) * tq)
        w = w_ref[rows, :]
        xo = x1_ref[rows, :] + w[:, 0:1] * bufs[q][0:tq, :] + w[:, 1:2] * bufs[q][tq:2 * tq, :]
        y_ref[rows, :] = (xo * lax.rsqrt(jnp.mean(xo * xo, axis=-1, keepdims=True) + EPS)
                          * gf_ref[...])

    @pl.when(i == 0)
    def _():
        for q in range(COMBINE_PARTS - 1):
            gather(posc_ref, q)

    for q in range(COMBINE_PARTS):
        wait(q)
        ahead = q + COMBINE_PARTS - 1
        if ahead < COMBINE_PARTS:
            gather(posc_ref, ahead)
        else:
            gather(posn_ref, ahead - COMBINE_PARTS)
        finish(q)

    @pl.when(i == nsteps - 1)
    def _():
        for q in range(COMBINE_PARTS - 1):
            wait(q)


def _combine(x1_flat, dest, w_col, ys, gf, tm):
    t, d = x1_flat.shape
    nsteps = t // tm
    tq = tm // COMBINE_PARTS
    pos3 = dest.reshape(2, nsteps, COMBINE_PARTS, tq).transpose(1, 2, 0, 3).reshape(nsteps, 1, 2 * tm)
    return pl.pallas_call(
        functools.partial(_combine_kernel, tq=tq, nsteps=nsteps),
        grid=(nsteps,),
        in_specs=[
            pl.BlockSpec((1, 1, 2 * tm), lambda i: (i, 0, 0), memory_space=pltpu.SMEM),
            pl.BlockSpec((1, 1, 2 * tm), lambda i, _n=nsteps: (jnp.minimum(i + 1, _n - 1), 0, 0),
                         memory_space=pltpu.SMEM),
            pl.BlockSpec((tm, d), lambda i: (i, 0)),
            pl.BlockSpec((tm, 2), lambda i: (i, 0)),
            pl.BlockSpec((1, d), lambda i: (0, 0)),
            pl.BlockSpec(memory_space=pl.ANY),
        ],
        out_specs=pl.BlockSpec((tm, d), lambda i: (i, 0)),
        out_shape=jax.ShapeDtypeStruct((t, d), F32),
        scratch_shapes=([pltpu.VMEM((2 * tq, d), F32) for _ in range(COMBINE_PARTS)]
                        + [pltpu.SemaphoreType.DMA((COMBINE_PARTS,))]),
        compiler_params=pltpu.CompilerParams(
            dimension_semantics=("arbitrary",), vmem_limit_bytes=VMEM_LIMIT),
        name="combine",
    )(pos3, pos3, x1_flat, w_col, gf, ys)


def _moe_and_norm(x1_flat, h2p_flat, logits_t, w_gate, w_up, w_down, gf):
    t, d = x1_flat.shape
    tr = _pick(t, (512, 256, 128))
    tm = _pick(t, (512, 256, 128))
    tmd = _pick(t, (512, 256, 128))
    blk = MOE_BLOCK if 2 * t >= 4 * N_EXPERTS * MOE_BLOCK else MOE_BLOCK_SMALL
    dest8, w8, meta = _route(logits_t, tr, blk)
    dest = dest8[0:2]
    nb = (2 * t + blk - 1) // blk + N_EXPERTS
    pend = meta[:, 2]
    blk_expert = jnp.minimum(
        jnp.sum(jnp.arange(nb, dtype=jnp.int32)[:, None] * blk >= pend[None, :], axis=1),
        N_EXPERTS - 1).astype(jnp.int32)
    n_used = (pend[-1:] // blk).astype(jnp.int32)
    xs = _dispatch(h2p_flat, dest, meta[:, 0:3].T.reshape(3 * N_EXPERTS), nb * blk, tmd, blk)
    ys = _experts(xs, blk_expert, n_used, w_gate, w_up, w_down, blk)
    return _combine(x1_flat, dest, w8[0:2].T, ys, gf, tm)


def _prep_layer_weights(norm1_g, w_in, conv_w, conv_b, dt_bias, a_log, d_skip, ssm_norm_g, ln_v_g,
                        ln_v_b, w_spatial, b_spatial, w_out, norm2_g, w_rg, b_rg, w_re, b_re):
    d = w_in.shape[0]
    conv_dim = SSD_WIDTH + 2 * SSD_GROUPS * SSD_STATE
    o_xbc = SSD_WIDTH
    o_dt = o_xbc + conv_dim
    o_gu = o_dt + SSD_HEADS
    o_gv = o_gu + GMLP_WIDTH
    w_dt = w_in[:, o_dt:o_gu]
    w_in_r = jnp.concatenate(
        [w_in[:, :o_xbc], w_in[:, o_xbc:o_dt], w_in[:, o_gu:o_gv], w_in[:, o_gv:],
         jnp.pad(w_dt, ((0, 0), (0, LANES - SSD_HEADS)))], axis=1).astype(BF16)
    rep = lambda v: jnp.repeat(v, SSD_HEAD_DIM)[None, :]
    col = lambda v: v[:, None]
    narrow = lambda v: jnp.pad(v, (0, LANES - SSD_HEADS))[None, :]
    sel = (jnp.arange(LANES)[:, None] == jnp.arange(SSD_WIDTH)[None, :] // SSD_HEAD_DIM).astype(BF16)
    w_r = jnp.concatenate([w_rg, jnp.zeros((d, SUBLANES - N_EXPERT_GROUPS), F32), w_re], axis=1).T
    b_r = jnp.concatenate([b_rg, jnp.zeros((SUBLANES - N_EXPERT_GROUPS,), F32), b_re])[:, None]
    w_r_hi = w_r.astype(BF16)
    w_r_lo = (w_r - w_r_hi.astype(F32)).astype(BF16)
    bsp = jnp.repeat(b_spatial.T, GMLP_GROUP_DIM, axis=1)
    return (norm1_g[None, :], w_in_r, w_dt.T.astype(BF16), conv_w, conv_b[None, :],
            narrow(dt_bias), narrow(a_log), sel, col(dt_bias), col(a_log), rep(d_skip),
            ssm_norm_g[None, :],
            ln_v_g[None, :], ln_v_b[None, :], w_spatial, bsp, w_out.astype(BF16), norm2_g[None, :],
            w_r_hi, w_r_lo, b_r)


def _state_to_pairs(s):
    b = s.shape[0]
    return s.reshape(b, N_PAIRS, 2, SSD_HEAD_DIM, SSD_STATE).transpose(0, 1, 4, 2, 3).reshape(
        b, N_PAIRS, SSD_STATE, 2 * SSD_HEAD_DIM)


def _pairs_to_state(s):
    b = s.shape[0]
    return s.reshape(b, N_PAIRS, SSD_STATE, 2, SSD_HEAD_DIM).transpose(0, 1, 3, 4, 2).reshape(
        b, SSD_HEADS, SSD_HEAD_DIM, SSD_STATE)


def _pick(n, prefs):
    for p in prefs:
        if n % p == 0:
            return p
    return n


def kernel(x_prompt, x_sample, cache_conv, state_ssm, norm1_g, w_in, conv_w, conv_b, dt_bias, a_log, d_skip, ssm_norm_g, ln_v_g, ln_v_b, w_spatial, b_spatial, w_out, norm2_g, w_router_group, b_router_group, w_router_expert, b_router_expert, w_gate, w_up, w_down, final_norm_g):
    depth = w_in.shape[0]
    assert depth == 1, "the combine kernel fuses the final norm, so only a single layer is supported"
    bp, lp, d = x_prompt.shape
    bs, ls, _ = x_sample.shape
    conv_dim = cache_conv.shape[-1]
    ls_pad = -(-ls // GMLP_CHUNK) * GMLP_CHUNK
    tl_p = _pick(lp, (256, 128))
    gf = final_norm_g[None, :]

    yp = x_prompt
    ys = jnp.pad(x_sample, ((0, 0), (0, ls_pad - ls), (0, 0)))
    conv_p, ssm_p, conv_s, ssm_s, v_s = [], [], [], [], []
    for i in range(depth):
        wts = _prep_layer_weights(
            norm1_g[i], w_in[i], conv_w[i], conv_b[i], dt_bias[i], a_log[i], d_skip[i], ssm_norm_g[i],
            ln_v_g[i], ln_v_b[i], w_spatial[i], b_spatial[i], w_out[i], norm2_g[i],
            w_router_group[i], b_router_group[i], w_router_expert[i], b_router_expert[i])
        gfi = gf

        conv0 = jnp.zeros((bp, SUBLANES, conv_dim), F32)
        ssm0 = jnp.zeros((bp, N_PAIRS, LANES, LANES), F32)
        x1, h2p, lt, cpo, hpo = _mixer(yp, conv0, ssm0, wts, tl=tl_p, l_valid=tl_p, emit_gv=False)
        tp = bp * lp
        outp = _moe_and_norm(x1.reshape(tp, d), h2p.reshape(tp, d // 2), lt, w_gate[i], w_up[i],
                             w_down[i], gfi)
        yp = outp.reshape(bp, lp, d)
        conv_p.append(cpo[:, SUBLANES - (CONV_WIDTH - 1):, :])
        ssm_p.append(_pairs_to_state(hpo))

        conv0 = jnp.pad(cache_conv[i], ((0, 0), (SUBLANES - (CONV_WIDTH - 1), 0), (0, 0)))
        x1, h2p, lt, cso, hso, gv = _mixer(ys, conv0, _state_to_pairs(state_ssm[i]), wts,
                                           tl=ls_pad, l_valid=ls, emit_gv=True)
        tsn = bs * ls
        x1v = x1[:, :ls].reshape(tsn, d)
        h2pv = h2p[:, :ls].reshape(tsn, d // 2)
        ltv = lt.reshape(ROUTER_ROWS, bs, ls_pad)[:, :, :ls].reshape(ROUTER_ROWS, tsn)
        outs = _moe_and_norm(x1v, h2pv, ltv, w_gate[i], w_up[i], w_down[i], gfi)
        ys = outs.reshape(bs, ls, d)
        conv_s.append(cso[:, SUBLANES - (CONV_WIDTH - 1):, :])
        ssm_s.append(_pairs_to_state(hso))
        v_s.append(gv[:, :ls])
    return (yp, ys, jnp.stack(conv_p), jnp.stack(ssm_p), jnp.stack(conv_s), jnp.stack(ssm_s),
            jnp.stack(v_s))
```
